```python
import math
import jax
import jax.numpy as jnp
from jax import lax
import numpy as np

D_MODEL = 1024
BATCH = 16
SEQ = 2048
DEPTH = 4

HEAD_DIM = 64
MOBA_HEADS = 4
MOBA_BLOCK = 256
MOBA_TOPK = 3
MOBA_Q_CHUNK = 32
DIFF_HEADS = 4
DIFF_QK_DIM = HEAD_DIM // 2
DIFF_V_DIM = HEAD_DIM
NSA_HEADS = 8
NSA_KV_HEADS = 2
NSA_GROUP = NSA_HEADS // NSA_KV_HEADS
NSA_CMP_BLOCK = 32
NSA_CMP_STRIDE = 16
NSA_CMP_HIDDEN = 256
NSA_SLC_BLOCK = 64
NSA_SLC_TOPK = 16
NSA_LOCAL_BLOCKS = 2
NSA_WINDOW = 512
NSA_Q_CHUNK = 32
ATTN_Q_BLOCK = 128
REL_BUCKETS = 32
REL_MAX_DIST = 128
N_BIAS_HEADS = MOBA_HEADS + DIFF_HEADS + NSA_HEADS
D_FF = 2816
RMS_EPS = 1e-6
NEG_INF = -1e30
FORCE_SCORE = 1e4

MOBA_W = MOBA_HEADS * HEAD_DIM
DIFF_QK_W = DIFF_HEADS * 2 * DIFF_QK_DIM
DIFF_V_W = DIFF_HEADS * DIFF_V_DIM
NSA_Q_W = NSA_HEADS * HEAD_DIM
NSA_KV_W = NSA_KV_HEADS * HEAD_DIM
NSA_GATE_W = 3 * NSA_HEADS
IN_SPLITS = (MOBA_W, MOBA_W, MOBA_W, DIFF_QK_W, DIFF_QK_W, DIFF_V_W, NSA_Q_W, NSA_KV_W, NSA_KV_W, NSA_KV_W, NSA_KV_W, NSA_KV_W, NSA_KV_W, NSA_GATE_W)
IN_WIDTH = 3 * MOBA_W + 2 * DIFF_QK_W + DIFF_V_W + NSA_Q_W + 6 * NSA_KV_W + NSA_GATE_W
MIX_WIDTH = MOBA_W + DIFF_V_W + NSA_Q_W

kernel_name = 'hybrid_moba_diff_nsa_macaron'


def rms_norm(x, g):
    xf = x.astype(jnp.float32)
    y = xf * lax.rsqrt(jnp.mean(xf * xf, axis=-1, keepdims=True) + RMS_EPS)
    return (y * g.astype(jnp.float32)).astype(x.dtype)


def swiglu(x, w_gate, w_up, w_down):
    return (jax.nn.silu(x @ w_gate) * (x @ w_up)) @ w_down


def rel_bucket(dist):
    n = jnp.maximum(dist, 0)
    max_exact = REL_BUCKETS // 2
    n_f = jnp.maximum(n, max_exact).astype(jnp.float32)
    large = max_exact + (jnp.log(n_f / max_exact) / math.log(REL_MAX_DIST / max_exact) * (REL_BUCKETS - max_exact)).astype(jnp.int32)
    return jnp.where(n < max_exact, n, jnp.minimum(large, REL_BUCKETS - 1))


def moba_attention(q, k, v, bias_table):
    B, H, S, dh = q.shape
    n_blk = -(-S // MOBA_BLOCK)
    pad = n_blk * MOBA_BLOCK - S
    kp = jnp.pad(k, ((0, 0), (0, 0), (0, pad), (0, 0)))
    vp = jnp.pad(v, ((0, 0), (0, 0), (0, pad), (0, 0)))
    k_blk = kp.reshape(B, H, n_blk, MOBA_BLOCK, dh)
    v_blk = vp.reshape(B, H, n_blk, MOBA_BLOCK, dh)
    k_mean = jnp.mean(k_blk.astype(jnp.float32), axis=3)
    top = min(MOBA_TOPK, max(n_blk - 1, 1))
    scale = dh ** -0.5
    bi = jnp.arange(B)[:, None, None, None]
    hi = jnp.arange(H)[None, :, None, None]
    blk_ids = jnp.arange(n_blk)
    offs = jnp.arange(MOBA_BLOCK)
    n_sel = top * MOBA_BLOCK

    def chunk(c):
        q0 = c * MOBA_Q_CHUNK
        own = q0 // MOBA_BLOCK
        q_pos = q0 + jnp.arange(MOBA_Q_CHUNK)
        qc = lax.dynamic_slice_in_dim(q, q0, MOBA_Q_CHUNK, axis=2).astype(jnp.float32)
        gate = jnp.einsum('bhqd,bhnd->bhqn', qc, k_mean)
        gate = jnp.where(blk_ids < own, gate, NEG_INF)
        _, idx = lax.top_k(gate, top)
        k_sel = k_blk[bi, hi, idx]
        v_sel = v_blk[bi, hi, idx]
        s_sel = jnp.einsum('bhqd,bhqnkd->bhqnk', qc, k_sel) * scale
        dist_sel = q_pos[None, None, :, None, None] - (idx[..., None] * MOBA_BLOCK + offs)
        s_sel = s_sel + bias_table[rel_bucket(dist_sel), hi[..., None]]
        s_sel = jnp.where((idx < own)[..., None], s_sel, NEG_INF)
        k_own = lax.dynamic_slice_in_dim(kp, own * MOBA_BLOCK, MOBA_BLOCK, axis=2)
        v_own = lax.dynamic_slice_in_dim(vp, own * MOBA_BLOCK, MOBA_BLOCK, axis=2)
        dist_own = q_pos[:, None] - (own * MOBA_BLOCK + offs)[None, :]
        s_own = jnp.einsum('bhqd,bhkd->bhqk', qc, k_own) * scale + jnp.moveaxis(bias_table[rel_bucket(dist_own)], -1, 0)
        s_own = jnp.where(dist_own >= 0, s_own, NEG_INF)
        p = jax.nn.softmax(jnp.concatenate([s_sel.reshape(B, H, MOBA_Q_CHUNK, n_sel), s_own], axis=-1), axis=-1)
        p_sel = p[..., :n_sel].reshape(B, H, MOBA_Q_CHUNK, top, MOBA_BLOCK)
        return jnp.einsum('bhqnk,bhqnkd->bhqd', p_sel, v_sel) + jnp.einsum('bhqk,bhkd->bhqd', p[..., n_sel:], v_own)

    o = lax.map(chunk, jnp.arange(S // MOBA_Q_CHUNK))
    return jnp.moveaxis(o, 0, 2).reshape(B, H, S, dh)


def diff_attention(q, k, v, bias_table, lam_params, subln_g, lambda_init):
    B, S, _ = q.shape
    H = DIFF_HEADS
    qh = q.reshape(B, S, H, 2, DIFF_QK_DIM).transpose(0, 2, 3, 1, 4).astype(jnp.float32)
    kh = k.reshape(B, S, H, 2, DIFF_QK_DIM).transpose(0, 2, 3, 1, 4)
    vh = v.reshape(B, S, H, DIFF_V_DIM).transpose(0, 2, 1, 3)
    lp = lam_params.astype(jnp.float32)
    lam = jnp.exp(jnp.sum(lp[0] * lp[1])) - jnp.exp(jnp.sum(lp[2] * lp[3])) + lambda_init
    scale = DIFF_QK_DIM ** -0.5
    k_pos = jnp.arange(S)

    def chunk(c):
        q0 = c * ATTN_Q_BLOCK
        qc = lax.dynamic_slice_in_dim(qh, q0, ATTN_Q_BLOCK, axis=3)
        dist = (q0 + jnp.arange(ATTN_Q_BLOCK))[:, None] - k_pos[None, :]
        bias = jnp.moveaxis(bias_table[rel_bucket(dist)], -1, 0)
        s = jnp.einsum('bhmqd,bhmkd->bhmqk', qc, kh) * scale + bias[None, :, None]
        p = jax.nn.softmax(jnp.where(dist >= 0, s, NEG_INF), axis=-1)
        a = p[:, :, 0] - lam * p[:, :, 1]
        return jnp.einsum('bhqk,bhkd->bhqd', a, vh)

    o = lax.map(chunk, jnp.arange(S // ATTN_Q_BLOCK))
    o = jnp.moveaxis(o, 0, 2).reshape(B, H, S, DIFF_V_DIM)
    o = rms_norm(o, subln_g) * (1.0 - lambda_init)
    return o.transpose(0, 2, 1, 3).reshape(B, S, H * DIFF_V_DIM)


def nsa_attention(q, kc, vc, ks, vs, kw, vw, gate_logits, bias_table, cmp_pe, cmp_w1, cmp_w2):
    B, S, _ = q.shape
    G, R, dh = NSA_KV_HEADS, NSA_GROUP, HEAD_DIM
    scale = dh ** -0.5
    qg = q.reshape(B, S, G, R, dh).transpose(0, 2, 3, 1, 4).astype(jnp.float32)

    def kv_heads(t):
        return t.reshape(B, S, G, dh).transpose(0, 2, 1, 3)

    kc, vc, ks, vs, kw, vw = (kv_heads(t) for t in (kc, vc, ks, vs, kw, vw))
    tbl = bias_table.reshape(REL_BUCKETS, G, R)
    pos = jnp.arange(S)

    n_cmp = (S - NSA_CMP_BLOCK) // NSA_CMP_STRIDE + 1
    cmp_start = np.arange(n_cmp) * NSA_CMP_STRIDE
    win_idx = cmp_start[:, None] + np.arange(NSA_CMP_BLOCK)[None, :]

    def compress(t, i):
        blocks = t[:, :, win_idx] + cmp_pe[i]
        flat = blocks.reshape(B, G, n_cmp, NSA_CMP_BLOCK * dh)
        return jax.nn.gelu(flat @ cmp_w1[i]) @ cmp_w2[i]

    k_cmp = compress(kc, 0)
    v_cmp = compress(vc, 1)
    cmp_valid = jnp.asarray(cmp_start + NSA_CMP_BLOCK - 1)[None, :] <= pos[:, None]
    s_cmp = jnp.einsum('bgrsd,bgnd->bgrsn', qg, k_cmp) * scale
    p_cmp = jnp.where(cmp_valid, jax.nn.softmax(jnp.where(cmp_valid, s_cmp, NEG_INF), axis=-1), 0.0)
    o_cmp = jnp.einsum('bgrsn,bgnd->bgrsd', p_cmp, v_cmp)

    n_slc = S // NSA_SLC_BLOCK
    slc_start = np.arange(n_slc) * NSA_SLC_BLOCK
    overlap = np.clip(np.minimum(cmp_start[:, None] + NSA_CMP_BLOCK, slc_start[None, :] + NSA_SLC_BLOCK) - np.maximum(cmp_start[:, None], slc_start[None, :]), 0, None) / NSA_CMP_BLOCK
    imp = jnp.einsum('bgrsn,nj->bgsj', p_cmp, jnp.asarray(overlap, jnp.float32))
    blk = jnp.arange(n_slc)[None, :]
    cur = (pos // NSA_SLC_BLOCK)[:, None]
    forced = (blk == 0) | (blk > cur - NSA_LOCAL_BLOCKS)
    score = jnp.where(blk <= cur, jnp.where(forced, FORCE_SCORE, imp), NEG_INF)
    top = min(NSA_SLC_TOPK, n_slc)
    _, sel_idx = lax.top_k(score, top)
    ks_blk = ks.reshape(B, G, n_slc, NSA_SLC_BLOCK, dh)
    vs_blk = vs.reshape(B, G, n_slc, NSA_SLC_BLOCK, dh)
    bi = jnp.arange(B)[:, None, None, None]
    gi = jnp.arange(G)[None, :, None, None]
    gi6 = jnp.arange(G)[None, :, None, None, None, None]
    ri6 = jnp.arange(R)[None, None, :, None, None, None]
    offs = jnp.arange(NSA_SLC_BLOCK)
    n_key = top * NSA_SLC_BLOCK

    def slc_chunk(c):
        q0 = c * NSA_Q_CHUNK
        qc = lax.dynamic_slice_in_dim(qg, q0, NSA_Q_CHUNK, axis=3)
        idx = lax.dynamic_slice_in_dim(sel_idx, q0, NSA_Q_CHUNK, axis=2)
        k_sel = ks_blk[bi, gi, idx]
        v_sel = vs_blk[bi, gi, idx]
        dist = (q0 + jnp.arange(NSA_Q_CHUNK))[None, None, :, None, None] - (idx[..., None] * NSA_SLC_BLOCK + offs)
        s = jnp.einsum('bgrqd,bgqnkd->bgrqnk', qc, k_sel) * scale + tbl[rel_bucket(dist)[:, :, None], gi6, ri6]
        s = jnp.where((dist >= 0)[:, :, None], s, NEG_INF)
        p = jax.nn.softmax(s.reshape(B, G, R, NSA_Q_CHUNK, n_key), axis=-1).reshape(s.shape)
        return jnp.einsum('bgrqnk,bgqnkd->bgrqd', p, v_sel)

    o_slc = lax.map(slc_chunk, jnp.arange(S // NSA_Q_CHUNK))
    o_slc = jnp.moveaxis(o_slc, 0, 3).reshape(B, G, R, S, dh)

    n_k = NSA_WINDOW + ATTN_Q_BLOCK
    kw_p = jnp.pad(kw, ((0, 0), (0, 0), (NSA_WINDOW, 0), (0, 0)))
    vw_p = jnp.pad(vw, ((0, 0), (0, 0), (NSA_WINDOW, 0), (0, 0)))

    def win_chunk(c):
        q0 = c * ATTN_Q_BLOCK
        qc = lax.dynamic_slice_in_dim(qg, q0, ATTN_Q_BLOCK, axis=3)
        kb = lax.dynamic_slice_in_dim(kw_p, q0, n_k, axis=2)
        vb = lax.dynamic_slice_in_dim(vw_p, q0, n_k, axis=2)
        k_pos = q0 - NSA_WINDOW + jnp.arange(n_k)
        dist = (q0 + jnp.arange(ATTN_Q_BLOCK))[:, None] - k_pos[None, :]
        bias = jnp.transpose(tbl[rel_bucket(dist)], (2, 3, 0, 1))
        s = jnp.einsum('bgrqd,bgkd->bgrqk', qc, kb) * scale + bias
        valid = (dist >= 0) & (dist < NSA_WINDOW) & (k_pos >= 0)[None, :]
        p = jax.nn.softmax(jnp.where(valid, s, NEG_INF), axis=-1)
        return jnp.einsum('bgrqk,bgkd->bgrqd', p, vb)

    o_win = lax.map(win_chunk, jnp.arange(S // ATTN_Q_BLOCK))
    o_win = jnp.moveaxis(o_win, 0, 3).reshape(B, G, R, S, dh)

    g = jax.nn.sigmoid(gate_logits.astype(jnp.float32)).reshape(B, S, G, R, 3).transpose(0, 2, 3, 1, 4)
    o = g[..., 0:1] * o_cmp + g[..., 1:2] * o_slc + g[..., 2:3] * o_win
    return o.transpose(0, 3, 1, 2, 4).reshape(B, S, NSA_HEADS * dh)


def hybrid_mixer(h, w_in, w_out, rel_bias, diff_lambda, diff_subln, lambda_init, cmp_pe, cmp_w1, cmp_w2):
    B, S, _ = h.shape
    proj = h @ w_in
    split_points = np.cumsum(IN_SPLITS)[:-1].tolist()
    (mq, mk, mv, dq, dk, dv, nq, nkc, nvc, nks, nvs, nkw, nvw, ng) = jnp.split(proj, split_points, axis=-1)

    def heads(t):
        return t.reshape(B, S, MOBA_HEADS, HEAD_DIM).transpose(0, 2, 1, 3)

    h0, h1 = MOBA_HEADS, MOBA_HEADS + DIFF_HEADS
    o_moba = moba_attention(heads(mq), heads(mk), heads(mv), rel_bias[:, :h0])
    o_moba = o_moba.transpose(0, 2, 1, 3).reshape(B, S, MOBA_W)
    o_diff = diff_attention(dq, dk, dv, rel_bias[:, h0:h1], diff_lambda, diff_subln, lambda_init)
    o_nsa = nsa_attention(nq, nkc, nvc, nks, nvs, nkw, nvw, ng, rel_bias[:, h1:], cmp_pe, cmp_w1, cmp_w2)
    o = jnp.concatenate([o_moba, o_diff, o_nsa], axis=-1).astype(h.dtype)
    return o @ w_out


def setup_inputs(seed: int = 0) -> dict:
    key = jax.random.key(seed)
    ks = jax.random.split(key, 20)

    def nrm(k, shape, scale):
        return jax.random.normal(k, shape, jnp.float32) * scale

    L, HD = NSA_CMP_BLOCK, HEAD_DIM
    return {
        'x': nrm(ks[0], (BATCH, SEQ, D_MODEL), 1.0),
        'rel_bias': nrm(ks[1], (REL_BUCKETS, N_BIAS_HEADS), 0.5),
        'norm_ffn1': 1.0 + nrm(ks[2], (DEPTH, D_MODEL), 0.02),
        'ffn1_gate': nrm(ks[3], (DEPTH, D_MODEL, D_FF), D_MODEL ** -0.5),
        'ffn1_up': nrm(ks[4], (DEPTH, D_MODEL, D_FF), D_MODEL ** -0.5),
        'ffn1_down': nrm(ks[5], (DEPTH, D_FF, D_MODEL), D_FF ** -0.5),
        'norm_mix': 1.0 + nrm(ks[6], (DEPTH, D_MODEL), 0.02),
        'w_in': nrm(ks[7], (DEPTH, D_MODEL, IN_WIDTH), D_MODEL ** -0.5),
        'diff_lambda': nrm(ks[8], (DEPTH, 4, DIFF_QK_DIM), 0.1),
        'diff_subln': 1.0 + nrm(ks[9], (DEPTH, DIFF_V_DIM), 0.02),
        'nsa_cmp_pe': nrm(ks[10], (DEPTH, 2, L, HD), 0.1),
        'nsa_cmp_w1': nrm(ks[11], (DEPTH, 2, L * HD, NSA_CMP_HIDDEN), (L * HD) ** -0.5),
        'nsa_cmp_w2': nrm(ks[12], (DEPTH, 2, NSA_CMP_HIDDEN, HD), NSA_CMP_HIDDEN ** -0.5),
        'w_out': nrm(ks[13], (DEPTH, MIX_WIDTH, D_MODEL), MIX_WIDTH ** -0.5),
        'norm_ffn2': 1.0 + nrm(ks[14], (DEPTH, D_MODEL), 0.02),
        'ffn2_gate': nrm(ks[15], (DEPTH, D_MODEL, D_FF), D_MODEL ** -0.5),
        'ffn2_up': nrm(ks[16], (DEPTH, D_MODEL, D_FF), D_MODEL ** -0.5),
        'ffn2_down': nrm(ks[17], (DEPTH, D_FF, D_MODEL), D_FF ** -0.5),
        'final_norm': 1.0 + nrm(ks[18], (D_MODEL,), 0.02),
    }


def reference(x, rel_bias, norm_ffn1, ffn1_gate, ffn1_up, ffn1_down, norm_mix, w_in, diff_lambda, diff_subln, nsa_cmp_pe, nsa_cmp_w1, nsa_cmp_w2, w_out, norm_ffn2, ffn2_gate, ffn2_up, ffn2_down, final_norm):
    for l in range(DEPTH):
        lambda_init = 0.8 - 0.6 * math.exp(-0.3 * l)
        x = x + 0.5 * swiglu(rms_norm(x, norm_ffn1[l]), ffn1_gate[l], ffn1_up[l], ffn1_down[l])
        x = x + hybrid_mixer(rms_norm(x, norm_mix[l]), w_in[l], w_out[l], rel_bias, diff_lambda[l], diff_subln[l], lambda_init, nsa_cmp_pe[l], nsa_cmp_w1[l], nsa_cmp_w2[l])
        x = x + 0.5 * swiglu(rms_norm(x, norm_ffn2[l]), ffn2_gate[l], ffn2_up[l], ffn2_down[l])
    return rms_norm(x, final_norm)
```

```python
import functools
import math

import numpy as np
import jax
import jax.numpy as jnp
from jax import lax
from jax.experimental import pallas as pl
from jax.experimental.pallas import tpu as pltpu

F32 = jnp.float32
BF16 = jnp.bfloat16

D_MODEL = 1024
HEAD_DIM = 64
MOBA_HEADS = 4
MOBA_BLOCK = 256
MOBA_TOPK = 3
DIFF_HEADS = 4
DIFF_QK_DIM = HEAD_DIM // 2
NSA_HEADS = 8
NSA_KV_HEADS = 2
NSA_GROUP = NSA_HEADS // NSA_KV_HEADS
NSA_CMP_BLOCK = 32
NSA_CMP_STRIDE = 16
NSA_CMP_HIDDEN = 256
NSA_SLC_BLOCK = 64
NSA_SLC_TOPK = 16
NSA_LOCAL_BLOCKS = 2
NSA_WINDOW = 512
REL_BUCKETS = 32
REL_MAX_DIST = 128
D_FF = 2816
RMS_EPS = 1e-6
NEG_INF = -1e30
FORCE_SCORE = 1e4

MOBA_W = MOBA_HEADS * HEAD_DIM
NSA_GATE_W = 3 * NSA_HEADS
QKV_WIDTH = 2816
LANES = 128
ATT_TILE = 256
VMEM_LIMIT = 48 * 1024 * 1024

NT_DIMS = (((1,), (1,)), ((), ()))


def _rms(x, g):
    return x * lax.rsqrt(jnp.mean(x * x, axis=-1, keepdims=True) + RMS_EPS) * g


def _params(*sem):
    return pltpu.CompilerParams(dimension_semantics=sem, vmem_limit_bytes=VMEM_LIMIT)


def _ffn_kernel(x_ref, g_ref, wg_ref, wu_ref, wd_ref, o_ref, xn_ref, acc_ref):
    f = pl.program_id(1)

    @pl.when(f == 0)
    def _():
        xn_ref[...] = _rms(x_ref[...], g_ref[...]).astype(BF16)
        acc_ref[...] = jnp.zeros_like(acc_ref)

    xn = xn_ref[...]
    gate = jnp.dot(xn, wg_ref[...], preferred_element_type=F32)
    up = jnp.dot(xn, wu_ref[...], preferred_element_type=F32)
    h = (gate * jax.nn.sigmoid(gate) * up).astype(BF16)
    acc_ref[...] += jnp.dot(h, wd_ref[...], preferred_element_type=F32)

    @pl.when(f == pl.num_programs(1) - 1)
    def _():
        o_ref[...] = x_ref[...] + 0.5 * acc_ref[...]


def _ffn(x2, g, wg, wu, wd, *, tm=1024, tf=256):
    n, d = x2.shape
    ff = wg.shape[1]
    return pl.pallas_call(
        _ffn_kernel,
        grid=(n // tm, ff // tf),
        in_specs=[
            pl.BlockSpec((tm, d), lambda i, f: (i, 0)),
            pl.BlockSpec((1, d), lambda i, f: (0, 0)),
            pl.BlockSpec((d, tf), lambda i, f: (0, f)),
            pl.BlockSpec((d, tf), lambda i, f: (0, f)),
            pl.BlockSpec((tf, d), lambda i, f: (f, 0)),
        ],
        out_specs=pl.BlockSpec((tm, d), lambda i, f: (i, 0)),
        out_shape=jax.ShapeDtypeStruct((n, d), F32),
        scratch_shapes=[pltpu.VMEM((tm, d), BF16), pltpu.VMEM((tm, d), F32)],
        compiler_params=_params("parallel", "arbitrary"),
        name="ffn",
    )(x2, g.reshape(1, d), wg, wu, wd)


_PROJ_OUTS = (
    ("mq", 0, 4, BF16, HEAD_DIM ** -0.5),
    ("mk", 256, 4, BF16, 1.0),
    ("mv", 512, 4, BF16, 1.0),
    ("dq", 768, 4, BF16, 1.0),
    ("dk", 1024, 4, BF16, 1.0),
    ("dv", 1280, 4, BF16, 1.0),
    ("nq", 1536, 8, BF16, HEAD_DIM ** -0.5),
    ("kc", 2048, 2, F32, 1.0),
    ("vc", 2176, 2, F32, 1.0),
    ("ks", 2304, 2, BF16, 1.0),
    ("vs", 2432, 2, BF16, 1.0),
    ("kw", 2560, 2, BF16, 1.0),
    ("vw", 2688, 2, BF16, 1.0),
)


def _inproj_kernel(x_ref, g_ref, w_ref, wgate_ref, *out_refs):
    xn = _rms(x_ref[0], g_ref[...]).astype(BF16)
    for (_, col, heads, dtype, scale), o_ref in zip(_PROJ_OUTS, out_refs[:-1]):
        width = heads * HEAD_DIM
        p = jnp.dot(xn, w_ref[:, col:col + width], preferred_element_type=F32)
        if scale != 1.0:
            p = p * scale
        for h in range(heads):
            o_ref[0, h] = p[:, h * HEAD_DIM:(h + 1) * HEAD_DIM].astype(dtype)
    out_refs[-1][0] = jnp.dot(xn, wgate_ref[...], preferred_element_type=F32)


def _inproj(x, g, w_qkv, w_gate, *, ts=512):
    b, s, d = x.shape
    out_shape = [jax.ShapeDtypeStruct((b, heads, s, HEAD_DIM), dtype) for _, _, heads, dtype, _ in _PROJ_OUTS]
    out_specs = [pl.BlockSpec((1, heads, ts, HEAD_DIM), lambda bi, i: (bi, 0, i, 0)) for _, _, heads, _, _ in _PROJ_OUTS]
    out_shape.append(jax.ShapeDtypeStruct((b, s, LANES), F32))
    out_specs.append(pl.BlockSpec((1, ts, LANES), lambda bi, i: (bi, i, 0)))
    outs = pl.pallas_call(
        _inproj_kernel,
        grid=(b, s // ts),
        in_specs=[
            pl.BlockSpec((1, ts, d), lambda bi, i: (bi, i, 0)),
            pl.BlockSpec((1, d), lambda bi, i: (0, 0)),
            pl.BlockSpec(w_qkv.shape, lambda bi, i: (0, 0)),
            pl.BlockSpec(w_gate.shape, lambda bi, i: (0, 0)),
        ],
        out_specs=out_specs,
        out_shape=out_shape,
        compiler_params=_params("parallel", "parallel"),
        name="inproj",
    )(x, g.reshape(1, d), w_qkv, w_gate)
    names = [o[0] for o in _PROJ_OUTS] + ["gate"]
    return dict(zip(names, outs))


def _flash_step(q, k, v, bias, mask, state, scale=None):
    m, l, acc = state
    s = lax.dot_general(q, k, NT_DIMS, preferred_element_type=F32)
    if scale is not None:
        s = s * scale
    s = s + bias
    if mask is not None:
        s = jnp.where(mask, s, NEG_INF)
    m_new = jnp.maximum(m, jnp.max(s, axis=-1, keepdims=True))
    alpha = jnp.exp(m - m_new)
    p = jnp.exp(s - m_new)
    l = alpha * l + jnp.sum(p, axis=-1, keepdims=True)
    acc = alpha * acc + jnp.dot(p.astype(BF16), v, preferred_element_type=F32)
    return m_new, l, acc


def _init_state(rows, dv):
    return (jnp.full((rows, 1), NEG_INF, F32), jnp.zeros((rows, 1), F32), jnp.zeros((rows, dv), F32))


def _tile(ref, j, t=ATT_TILE):
    return ref[pl.ds(pl.multiple_of(j * t, t), t), :]


def _rank_select(score, topk):
    t, w = score.shape
    col_id = lax.broadcasted_iota(jnp.int32, (t, w), 1)
    rank = jnp.zeros((t, w), jnp.int32)
    return col_id, rank


def _topk_mask(score, n_cols, topk):
    t, w = score.shape
    col_id = lax.broadcasted_iota(jnp.int32, (t, w), 1)
    rank = jnp.zeros((t, w), jnp.int32)
    for c in range(n_cols):
        other = score[:, c:c + 1]
        beats = (other > score) | ((other == score) & (col_id > c))
        rank = rank + beats.astype(jnp.int32)
    return (rank < topk) & (col_id < n_cols)


def _moba_kernel(far_ref, q_ref, k_ref, v_ref, near_ref, avg_ref, expand_ref, o_ref, sel_ref):
    i = pl.program_id(1)
    t = ATT_TILE
    row = lax.broadcasted_iota(jnp.int32, (t, t), 0)
    col = lax.broadcasted_iota(jnp.int32, (t, t), 1)
    causal = row >= col
    blk = lax.broadcasted_iota(jnp.int32, (t, LANES), 1)
    outs = []
    for h in range(MOBA_HEADS):
        q = q_ref[0, h]
        k_ref_h = k_ref.at[0, h]
        v_ref_h = v_ref.at[0, h]
        k_mean = jnp.dot(avg_ref[...], k_ref_h[...], preferred_element_type=F32)
        km_hi = k_mean.astype(BF16)
        km_lo = (k_mean - km_hi.astype(F32)).astype(BF16)
        gate = (lax.dot_general(q, km_hi, NT_DIMS, preferred_element_type=F32)
                + lax.dot_general(q, km_lo, NT_DIMS, preferred_element_type=F32))
        gate = jnp.where(blk < i, gate, NEG_INF)
        sel = _topk_mask(gate, S_BLOCKS_MOBA, MOBA_TOPK) & (blk < i)
        sel_ref[...] = jnp.dot(sel.astype(BF16), expand_ref[...], preferred_element_type=F32)

        state = _init_state(t, HEAD_DIM)
        state = _flash_step(q, _tile(k_ref_h, i), _tile(v_ref_h, i), near_ref[h, 0], causal, state)

        def near_body(j, st):
            mask = _tile_lanes(sel_ref, j) > 0.5
            return _flash_step(q, _tile(k_ref_h, j), _tile(v_ref_h, j), near_ref[h, 1], mask, st)

        def far_body(j, st):
            mask = _tile_lanes(sel_ref, j) > 0.5
            return _flash_step(q, _tile(k_ref_h, j), _tile(v_ref_h, j), far_ref[h], mask, st)

        prev = jnp.maximum(i - 1, 0)
        state = lax.fori_loop(prev, i, near_body, state)
        state = lax.fori_loop(0, prev, far_body, state)
        _, l, acc = state
        outs.append(acc / l)
    o_ref[0] = jnp.concatenate(outs, axis=-1).astype(o_ref.dtype)


S_BLOCKS_MOBA = 8


def _tile_lanes(ref, j, t=ATT_TILE):
    return ref[:, pl.ds(pl.multiple_of(j * t, t), t)]


def _moba(mq, mk, mv, bias_near, bias_far, avg, expand):
    b, h, s, dh = mq.shape
    t = ATT_TILE
    assert MOBA_BLOCK == t and s // t == S_BLOCKS_MOBA and s % t == 0
    return pl.pallas_call(
        _moba_kernel,
        grid=(b, s // t),
        in_specs=[
            pl.BlockSpec(memory_space=pltpu.SMEM),
            pl.BlockSpec((1, h, t, dh), lambda bi, i: (bi, 0, i, 0)),
            pl.BlockSpec((1, h, s, dh), lambda bi, i: (bi, 0, 0, 0)),
            pl.BlockSpec((1, h, s, dh), lambda bi, i: (bi, 0, 0, 0)),
            pl.BlockSpec(bias_near.shape, lambda bi, i: (0, 0, 0, 0)),
            pl.BlockSpec(avg.shape, lambda bi, i: (0, 0)),
            pl.BlockSpec(expand.shape, lambda bi, i: (0, 0)),
        ],
        out_specs=pl.BlockSpec((1, t, h * dh), lambda bi, i: (bi, i, 0)),
        out_shape=jax.ShapeDtypeStruct((b, s, h * dh), BF16),
        scratch_shapes=[pltpu.VMEM((t, s), F32)],
        compiler_params=_params("parallel", "arbitrary"),
        name="moba",
    )(bias_far, mq, mk, mv, bias_near, avg, expand)


def _diff_kernel(far_ref, q_ref, k_ref, v_ref, near_ref, lam_ref, subln_ref, o_ref, *, lambda_init):
    i = pl.program_id(1)
    t = ATT_TILE
    row = lax.broadcasted_iota(jnp.int32, (t, t), 0)
    col = lax.broadcasted_iota(jnp.int32, (t, t), 1)
    causal = row >= col
    lane = lax.broadcasted_iota(jnp.int32, (t, HEAD_DIM), 1)
    scale = DIFF_QK_DIM ** -0.5
    lp = lam_ref[...]
    lam = (jnp.exp(jnp.sum(lp[0:1] * lp[1:2], axis=-1, keepdims=True))
           - jnp.exp(jnp.sum(lp[2:3] * lp[3:4], axis=-1, keepdims=True)) + lambda_init)
    outs = []
    for h in range(DIFF_HEADS):
        q = q_ref[0, h]
        qs = (jnp.where(lane < DIFF_QK_DIM, q, jnp.zeros_like(q)), jnp.where(lane >= DIFF_QK_DIM, q, jnp.zeros_like(q)))
        k_ref_h = k_ref.at[0, h]
        v_ref_h = v_ref.at[0, h]

        def step(j, bias, mask, states):
            k = _tile(k_ref_h, j)
            v = _tile(v_ref_h, j)
            return tuple(_flash_step(qm, k, v, bias, mask, st, scale=scale) for qm, st in zip(qs, states))

        states = (_init_state(t, HEAD_DIM), _init_state(t, HEAD_DIM))
        states = step(i, near_ref[h, 0], causal, states)
        prev = jnp.maximum(i - 1, 0)
        states = lax.fori_loop(prev, i, lambda j, st: step(j, near_ref[h, 1], None, st), states)
        states = lax.fori_loop(0, prev, lambda j, st: step(j, far_ref[h], None, st), states)
        (_, l1, a1), (_, l2, a2) = states
        o = a1 / l1 - lam * (a2 / l2)
        o = _rms(o, subln_ref[...]) * (1.0 - lambda_init)
        outs.append(o)
    o_ref[0] = jnp.concatenate(outs, axis=-1).astype(o_ref.dtype)


def _diff(dq, dk, dv, bias_near, bias_far, lam_params, subln_g, lambda_init):
    b, h, s, dh = dq.shape
    t = ATT_TILE
    return pl.pallas_call(
        functools.partial(_diff_kernel, lambda_init=lambda_init),
        grid=(b, s // t),
        in_specs=[
            pl.BlockSpec(memory_space=pltpu.SMEM),
            pl.BlockSpec((1, h, t, dh), lambda bi, i: (bi, 0, i, 0)),
            pl.BlockSpec((1, h, s, dh), lambda bi, i: (bi, 0, 0, 0)),
            pl.BlockSpec((1, h, s, dh), lambda bi, i: (bi, 0, 0, 0)),
            pl.BlockSpec(bias_near.shape, lambda bi, i: (0, 0, 0, 0)),
            pl.BlockSpec(lam_params.shape, lambda bi, i: (0, 0)),
            pl.BlockSpec((1, dh), lambda bi, i: (0, 0)),
        ],
        out_specs=pl.BlockSpec((1, t, h * dh), lambda bi, i: (bi, i, 0)),
        out_shape=jax.ShapeDtypeStruct((b, s, h * dh), BF16),
        compiler_params=_params("parallel", "arbitrary"),
        name="diff",
    )(bias_far, dq, dk, dv, bias_near, lam_params, subln_g.reshape(1, dh))


def _compress_kernel(kc_ref, vc_ref, pe_ref, w1_ref, w2_ref, ko_ref, vo_ref):
    half = NSA_CMP_STRIDE * HEAD_DIM
    for t, (c_ref, o_ref) in enumerate(((kc_ref, ko_ref), (vc_ref, vo_ref))):
        c = c_ref[0, 0]
        top = (c + pe_ref[2 * t:2 * t + 1]).astype(BF16)
        bot = (c + pe_ref[2 * t + 1:2 * t + 2]).astype(BF16)
        a = jnp.dot(top, w1_ref[t, :half], preferred_element_type=F32)
        bm = jnp.dot(bot, w1_ref[t, half:], preferred_element_type=F32)
        n = bm.shape[0]
        hid = a + pltpu.roll(bm, n - 1, 0)
        hid = jax.nn.gelu(hid).astype(BF16)
        o_ref[0, 0] = jnp.dot(hid, w2_ref[t], preferred_element_type=F32)


def _compress(kc, vc, pe4, w1, w2):
    b, g, s, dh = kc.shape
    n_chunk = s // NSA_CMP_STRIDE
    width = NSA_CMP_STRIDE * dh
    kc = kc.reshape(b, g, n_chunk, width)
    vc = vc.reshape(b, g, n_chunk, width)
    spec_in = pl.BlockSpec((1, 1, n_chunk, width), lambda bi, gi: (bi, gi, 0, 0))
    spec_out = pl.BlockSpec((1, 1, n_chunk, dh), lambda bi, gi: (bi, gi, 0, 0))
    return pl.pallas_call(
        _compress_kernel,
        grid=(b, g),
        in_specs=[
            spec_in, spec_in,
            pl.BlockSpec(pe4.shape, lambda bi, gi: (0, 0)),
            pl.BlockSpec(w1.shape, lambda bi, gi: (0, 0, 0)),
            pl.BlockSpec(w2.shape, lambda bi, gi: (0, 0, 0)),
        ],
        out_specs=[spec_out, spec_out],
        out_shape=[jax.ShapeDtypeStruct((b, g, n_chunk, dh), F32)] * 2,
        compiler_params=_params("parallel", "parallel"),
        name="nsa_compress",
    )(kc, vc, pe4, w1, w2)


def _nsa_kernel(far_ref, q_ref, kcmp_ref, vcmp_ref, ks_ref, vs_ref, kw_ref, vw_ref, gate_ref, near_ref,
                overlap_ref, expand_ref, o_ref, sel_ref):
    gi = pl.program_id(1)
    i = pl.program_id(2)
    t = ATT_TILE
    row = lax.broadcasted_iota(jnp.int32, (t, t), 0)
    col = lax.broadcasted_iota(jnp.int32, (t, t), 1)
    causal = row >= col
    lane = lax.broadcasted_iota(jnp.int32, (t, LANES), 1)
    q_pos = i * t + lax.broadcasted_iota(jnp.int32, (t, LANES), 0)
    n_cmp = kcmp_ref.shape[2] - 1
    cmp_valid = (lane * NSA_CMP_STRIDE + (NSA_CMP_BLOCK - 1) <= q_pos) & (lane < n_cmp)
    k_cmp = kcmp_ref[0, 0].astype(BF16)
    v_cmp = vcmp_ref[0, 0].astype(BF16)
    ks_ref_g, vs_ref_g = ks_ref.at[0, 0], vs_ref.at[0, 0]
    kw_ref_g, vw_ref_g = kw_ref.at[0, 0], vw_ref.at[0, 0]

    o_cmp = []
    p_sum = jnp.zeros((t, LANES), F32)
    for r in range(NSA_GROUP):
        s = lax.dot_general(q_ref[0, r], k_cmp, NT_DIMS, preferred_element_type=F32)
        m = jnp.max(jnp.where(cmp_valid, s, NEG_INF), axis=-1, keepdims=True)
        e = jnp.where(cmp_valid, jnp.exp(s - m), 0.0)
        l = jnp.sum(e, axis=-1, keepdims=True)
        p = e / jnp.where(l > 0.0, l, 1.0)
        p_sum = p_sum + p
        o_cmp.append(jnp.dot(p.astype(BF16), v_cmp, preferred_element_type=F32))

    ps_hi = p_sum.astype(BF16)
    ps_lo = (p_sum - ps_hi.astype(F32)).astype(BF16)
    imp = (jnp.dot(ps_hi, overlap_ref[...], preferred_element_type=F32)
           + jnp.dot(ps_lo, overlap_ref[...], preferred_element_type=F32))
    cur = q_pos // NSA_SLC_BLOCK
    forced = (lane == 0) | (lane > cur - NSA_LOCAL_BLOCKS)
    score = jnp.where(lane <= cur, jnp.where(forced, FORCE_SCORE, imp), NEG_INF)
    n_slc = expand_ref.shape[1] // NSA_SLC_BLOCK
    sel = _topk_mask(score, n_slc, min(NSA_SLC_TOPK, n_slc))
    sel_ref[...] = jnp.dot(sel.astype(BF16), expand_ref[...], preferred_element_type=F32)

    gates = jax.nn.sigmoid(gate_ref[0])
    prev = jnp.maximum(i - 1, 0)
    prev2 = jnp.maximum(i - 2, 0)
    outs = []
    for r in range(NSA_GROUP):
        q = q_ref[0, r]
        far = far_ref[gi * NSA_GROUP + r]
        near_diag = near_ref[0, r, 0]
        near_prev = near_ref[0, r, 1]

        st = _init_state(t, HEAD_DIM)
        st = _flash_step(q, _tile(ks_ref_g, i), _tile(vs_ref_g, i), near_diag,
                         causal & (_tile_lanes(sel_ref, i) > 0.5), st)
        st = lax.fori_loop(prev, i, lambda j, c: _flash_step(
            q, _tile(ks_ref_g, j), _tile(vs_ref_g, j), near_prev, _tile_lanes(sel_ref, j) > 0.5, c), st)
        st = lax.fori_loop(0, prev, lambda j, c: _flash_step(
            q, _tile(ks_ref_g, j), _tile(vs_ref_g, j), far, _tile_lanes(sel_ref, j) > 0.5, c), st)
        o_slc = st[2] / st[1]

        st = _init_state(t, HEAD_DIM)
        st = _flash_step(q, _tile(kw_ref_g, i), _tile(vw_ref_g, i), near_diag, causal, st)
        st = lax.fori_loop(prev, i, lambda j, c: _flash_step(
            q, _tile(kw_ref_g, j), _tile(vw_ref_g, j), near_prev, None, c), st)
        st = lax.fori_loop(prev2, prev, lambda j, c: _flash_step(
            q, _tile(kw_ref_g, j), _tile(vw_ref_g, j), far, row < col, c), st)
        o_win = st[2] / st[1]

        c0 = gi * (3 * NSA_GROUP) + 3 * r
        gsel = [jnp.sum(jnp.where(lane == c0 + br, gates, 0.0), axis=-1, keepdims=True) for br in range(3)]
        outs.append(gsel[0] * o_cmp[r] + gsel[1] * o_slc + gsel[2] * o_win)
    o_ref[0] = jnp.concatenate(outs, axis=-1).astype(o_ref.dtype)


def _nsa(nq, k_cmp, v_cmp, ks, vs, kw, vw, gate, bias_near, bias_far, overlap, expand):
    b, _, s, dh = nq.shape
    g, r, t = NSA_KV_HEADS, NSA_GROUP, ATT_TILE
    assert NSA_WINDOW == 2 * t and s % t == 0
    kv_spec = pl.BlockSpec((1, 1, s, dh), lambda bi, gi, i: (bi, gi, 0, 0))
    cmp_spec = pl.BlockSpec((1, 1) + k_cmp.shape[2:], lambda bi, gi, i: (bi, gi, 0, 0))
    return pl.pallas_call(
        _nsa_kernel,
        grid=(b, g, s // t),
        in_specs=[
            pl.BlockSpec(memory_space=pltpu.SMEM),
            pl.BlockSpec((1, r, t, dh), lambda bi, gi, i: (bi, gi, i, 0)),
            cmp_spec, cmp_spec, kv_spec, kv_spec, kv_spec, kv_spec,
            pl.BlockSpec((1, t, LANES), lambda bi, gi, i: (bi, i, 0)),
            pl.BlockSpec((1, r, 2, t, t), lambda bi, gi, i: (gi, 0, 0, 0, 0)),
            pl.BlockSpec(overlap.shape, lambda bi, gi, i: (0, 0)),
            pl.BlockSpec(expand.shape, lambda bi, gi, i: (0, 0)),
        ],
        out_specs=pl.BlockSpec((1, t, r * dh), lambda bi, gi, i: (bi, i, gi)),
        out_shape=jax.ShapeDtypeStruct((b, s, g * r * dh), BF16),
        scratch_shapes=[pltpu.VMEM((t, s), F32)],
        compiler_params=_params("parallel", "parallel", "arbitrary"),
        name="nsa",
    )(bias_far, nq, k_cmp, v_cmp, ks, vs, kw, vw, gate, bias_near, overlap, expand)


def _outproj_kernel(x_ref, om_ref, od_ref, on_ref, w_ref, o_ref):
    wm = om_ref.shape[-1]
    wd = od_ref.shape[-1]
    y = jnp.dot(om_ref[0], w_ref[:wm], preferred_element_type=F32)
    y = y + jnp.dot(od_ref[0], w_ref[wm:wm + wd], preferred_element_type=F32)
    y = y + jnp.dot(on_ref[0], w_ref[wm + wd:], preferred_element_type=F32)
    o_ref[0] = x_ref[0] + y


def _outproj(x, o_m, o_d, o_n, w, *, ts=512):
    b, s, d = x.shape

    def spec(a):
        return pl.BlockSpec((1, ts, a.shape[-1]), lambda bi, i: (bi, i, 0))

    return pl.pallas_call(
        _outproj_kernel,
        grid=(b, s // ts),
        in_specs=[spec(x), spec(o_m), spec(o_d), spec(o_n), pl.BlockSpec(w.shape, lambda bi, i: (0, 0))],
        out_specs=spec(x),
        out_shape=jax.ShapeDtypeStruct((b, s, d), F32),
        compiler_params=_params("parallel", "parallel"),
        name="outproj",
    )(x, o_m, o_d, o_n, w)


def _final_norm_kernel(x_ref, g_ref, o_ref):
    o_ref[...] = _rms(x_ref[...], g_ref[...])


def _final_norm(x2, g, *, tm=1024):
    n, d = x2.shape
    return pl.pallas_call(
        _final_norm_kernel,
        grid=(n // tm,),
        in_specs=[pl.BlockSpec((tm, d), lambda i: (i, 0)), pl.BlockSpec((1, d), lambda i: (0, 0))],
        out_specs=pl.BlockSpec((tm, d), lambda i: (i, 0)),
        out_shape=jax.ShapeDtypeStruct((n, d), F32),
        compiler_params=_params("parallel"),
        name="final_norm",
    )(x2, g.reshape(1, d))


def _rel_bucket(dist):
    n = jnp.maximum(dist, 0)
    max_exact = REL_BUCKETS // 2
    n_f = jnp.maximum(n, max_exact).astype(F32)
    large = max_exact + (jnp.log(n_f / max_exact) / math.log(REL_MAX_DIST / max_exact)
                         * (REL_BUCKETS - max_exact)).astype(jnp.int32)
    return jnp.where(n < max_exact, n, jnp.minimum(large, REL_BUCKETS - 1))


def _bias_tiles(rel_bias):
    t = ATT_TILE
    assert t + 1 >= REL_MAX_DIST
    d = jnp.arange(t)[:, None] - jnp.arange(t)[None, :]
    dist = jnp.stack([d, d + t])
    near = jnp.moveaxis(rel_bias[_rel_bucket(dist)], -1, 0)
    return near, rel_bias[REL_BUCKETS - 1]


def _const_tables(s):
    n_moba = s // MOBA_BLOCK
    avg = np.zeros((LANES, s), np.float32)
    moba_expand = np.zeros((LANES, s), np.float32)
    for j in range(n_moba):
        avg[j, j * MOBA_BLOCK:(j + 1) * MOBA_BLOCK] = 1.0 / MOBA_BLOCK
        moba_expand[j, j * MOBA_BLOCK:(j + 1) * MOBA_BLOCK] = 1.0
    n_cmp = (s - NSA_CMP_BLOCK) // NSA_CMP_STRIDE + 1
    n_slc = s // NSA_SLC_BLOCK
    cmp_start = np.arange(n_cmp) * NSA_CMP_STRIDE
    slc_start = np.arange(n_slc) * NSA_SLC_BLOCK
    ov = np.clip(np.minimum(cmp_start[:, None] + NSA_CMP_BLOCK, slc_start[None, :] + NSA_SLC_BLOCK)
                 - np.maximum(cmp_start[:, None], slc_start[None, :]), 0, None) / NSA_CMP_BLOCK
    overlap = np.zeros((LANES, LANES), np.float32)
    overlap[:n_cmp, :n_slc] = ov
    slc_expand = np.zeros((LANES, s), np.float32)
    for j in range(n_slc):
        slc_expand[j, j * NSA_SLC_BLOCK:(j + 1) * NSA_SLC_BLOCK] = 1.0
    return tuple(jnp.asarray(a, BF16) for a in (avg, moba_expand, overlap, slc_expand))


def kernel(x, rel_bias, norm_ffn1, ffn1_gate, ffn1_up, ffn1_down, norm_mix, w_in, diff_lambda, diff_subln, nsa_cmp_pe, nsa_cmp_w1, nsa_cmp_w2, w_out, norm_ffn2, ffn2_gate, ffn2_up, ffn2_down, final_norm):
    b, s, d = x.shape
    depth = w_in.shape[0]
    assert d == D_MODEL and n_cmp_rows(s) <= LANES and s // NSA_SLC_BLOCK <= LANES
    h0, h1 = MOBA_HEADS, MOBA_HEADS + DIFF_HEADS
    bias_near, bias_far = _bias_tiles(rel_bias.astype(F32))
    nsa_near = bias_near[h1:].reshape(NSA_KV_HEADS, NSA_GROUP, 2, ATT_TILE, ATT_TILE)
    avg, moba_expand, overlap, slc_expand = _const_tables(s)
    bf = lambda a: a.astype(BF16)

    for l in range(depth):
        lambda_init = 0.8 - 0.6 * math.exp(-0.3 * l)
        x = _ffn(x.reshape(b * s, d), norm_ffn1[l], bf(ffn1_gate[l]), bf(ffn1_up[l]), bf(ffn1_down[l])).reshape(b, s, d)

        w_l = w_in[l]
        w_gate = jnp.pad(w_l[:, QKV_WIDTH:], ((0, 0), (0, LANES - NSA_GATE_W)))
        p = _inproj(x, norm_mix[l], bf(w_l[:, :QKV_WIDTH]), bf(w_gate))
        o_moba = _moba(p["mq"], p["mk"], p["mv"], bias_near[:h0], bias_far[:h0], avg, moba_expand)
        o_diff = _diff(p["dq"], p["dk"], p["dv"], bias_near[h0:h1], bias_far[h0:h1],
                       diff_lambda[l].astype(F32), diff_subln[l].astype(F32), lambda_init)
        pe4 = nsa_cmp_pe[l].astype(F32).reshape(4, NSA_CMP_STRIDE * HEAD_DIM)
        k_cmp, v_cmp = _compress(p["kc"], p["vc"], pe4, bf(nsa_cmp_w1[l]), bf(nsa_cmp_w2[l]))
        o_nsa = _nsa(p["nq"], k_cmp, v_cmp, p["ks"], p["vs"], p["kw"], p["vw"], p["gate"],
                     nsa_near, bias_far[h1:], overlap, slc_expand)
        x = _outproj(x, o_moba, o_diff, o_nsa, bf(w_out[l]))

        x = _ffn(x.reshape(b * s, d), norm_ffn2[l], bf(ffn2_gate[l]), bf(ffn2_up[l]), bf(ffn2_down[l])).reshape(b, s, d)
    return _final_norm(x.reshape(b * s, d), final_norm).reshape(b, s, d)


def n_cmp_rows(s):
    return s // NSA_CMP_STRIDE
```

```python
import functools
import math

import numpy as np
import jax
import jax.numpy as jnp
from jax import lax
from jax.experimental import pallas as pl
from jax.experimental.pallas import tpu as pltpu

F32 = jnp.float32
BF16 = jnp.bfloat16

D_MODEL = 1024
HEAD_DIM = 64
MOBA_HEADS = 4
MOBA_BLOCK = 256
MOBA_TOPK = 3
DIFF_HEADS = 4
DIFF_QK_DIM = HEAD_DIM // 2
NSA_HEADS = 8
NSA_KV_HEADS = 2
NSA_GROUP = NSA_HEADS // NSA_KV_HEADS
NSA_CMP_BLOCK = 32
NSA_CMP_STRIDE = 16
NSA_CMP_HIDDEN = 256
NSA_SLC_BLOCK = 64
NSA_SLC_TOPK = 16
NSA_LOCAL_BLOCKS = 2
NSA_WINDOW = 512
REL_BUCKETS = 32
REL_MAX_DIST = 128
D_FF = 2816
RMS_EPS = 1e-6
NEG_INF = -1e30
FORCE_SCORE = 1e4

NSA_GATE_W = 3 * NSA_HEADS
NSA_GATE_ROWS = 16
LANES = 128
SUBLANES = 8
ATT_TILE = 256
VMEM_LIMIT = 48 * 1024 * 1024
LOG2E = math.log2(math.e)

NT_DIMS = (((1,), (1,)), ((), ()))


def _rms(x, g):
    return x * lax.rsqrt(jnp.mean(x * x, axis=-1, keepdims=True) + RMS_EPS) * g


def _params(*sem):
    return pltpu.CompilerParams(dimension_semantics=sem, vmem_limit_bytes=VMEM_LIMIT)


def _ffn_kernel(x_ref, g_ref, wg_ref, wu_ref, wd_ref, o_ref, xn_ref, acc_ref):
    f = pl.program_id(1)

    @pl.when(f == 0)
    def _():
        xn_ref[...] = _rms(x_ref[...], g_ref[...]).astype(BF16)
        acc_ref[...] = jnp.zeros_like(acc_ref)

    xn = xn_ref[...]
    gate = jnp.dot(xn, wg_ref[...], preferred_element_type=F32)
    up = jnp.dot(xn, wu_ref[...], preferred_element_type=F32)
    h = (gate * jax.nn.sigmoid(gate) * up).astype(BF16)
    acc_ref[...] += jnp.dot(h, wd_ref[...], preferred_element_type=F32)

    @pl.when(f == pl.num_programs(1) - 1)
    def _():
        o_ref[...] = x_ref[...] + 0.5 * acc_ref[...]


def _ffn(x2, g, wg, wu, wd, *, tm=1024, tf=256):
    n, d = x2.shape
    ff = wg.shape[1]
    return pl.pallas_call(
        _ffn_kernel,
        grid=(n // tm, ff // tf),
        in_specs=[
            pl.BlockSpec((tm, d), lambda i, f: (i, 0)),
            pl.BlockSpec((1, d), lambda i, f: (0, 0)),
            pl.BlockSpec((d, tf), lambda i, f: (0, f)),
            pl.BlockSpec((d, tf), lambda i, f: (0, f)),
            pl.BlockSpec((tf, d), lambda i, f: (f, 0)),
        ],
        out_specs=pl.BlockSpec((tm, d), lambda i, f: (i, 0)),
        out_shape=jax.ShapeDtypeStruct((n, d), F32),
        scratch_shapes=[pltpu.VMEM((tm, d), BF16), pltpu.VMEM((tm, d), F32)],
        compiler_params=_params("parallel", "arbitrary"),
        name="ffn",
    )(x2, g.reshape(1, d), wg, wu, wd)


_ROW_OUTS = (
    ("mq", 0, 4, BF16, HEAD_DIM ** -0.5),
    ("mk", 256, 4, BF16, 1.0),
    ("dq", 768, 4, BF16, 1.0),
    ("dk", 1024, 4, BF16, 1.0),
    ("nq", 1536, 8, BF16, HEAD_DIM ** -0.5),
    ("kc", 2048, 2, F32, 1.0),
    ("vc", 2176, 2, F32, 1.0),
    ("ks", 2304, 2, BF16, 1.0),
    ("kw", 2560, 2, BF16, 1.0),
)
_COL_OUTS = (
    ("mvT", 512, 256, BF16),
    ("dvT", 1280, 256, BF16),
    ("vsT", 2432, 128, BF16),
    ("vwT", 2688, 128, BF16),
    ("gateT", None, NSA_KV_HEADS * NSA_GATE_ROWS, F32),
)


def _inproj_kernel(x_ref, g_ref, w_ref, wt_ref, *out_refs):
    xn = _rms(x_ref[0], g_ref[...]).astype(BF16)
    col = 0
    for (_, _, heads, dtype, scale), o_ref in zip(_ROW_OUTS, out_refs):
        width = heads * HEAD_DIM
        p = jnp.dot(xn, w_ref[:, col:col + width], preferred_element_type=F32)
        if scale != 1.0:
            p = p * scale
        for h in range(heads):
            o_ref[0, h] = p[:, h * HEAD_DIM:(h + 1) * HEAD_DIM].astype(dtype)
        col += width
    pt = lax.dot_general(wt_ref[...], xn, NT_DIMS, preferred_element_type=F32)
    row = 0
    for (_, _, rows, dtype), o_ref in zip(_COL_OUTS, out_refs[len(_ROW_OUTS):]):
        o_ref[0] = pt[row:row + rows].astype(dtype)
        row += rows


def _inproj_weights(w_l):
    w_rows = jnp.concatenate([w_l[:, c:c + h * HEAD_DIM] for _, c, h, _, _ in _ROW_OUTS], axis=1)
    gate_cols = w_l[:, w_l.shape[1] - NSA_GATE_W:]
    per_group = 3 * NSA_GROUP
    gate_t = jnp.zeros((NSA_KV_HEADS * NSA_GATE_ROWS, w_l.shape[0]), w_l.dtype)
    for g in range(NSA_KV_HEADS):
        gate_t = gate_t.at[g * NSA_GATE_ROWS:g * NSA_GATE_ROWS + per_group].set(
            gate_cols[:, g * per_group:(g + 1) * per_group].T)
    w_cols = jnp.concatenate([w_l[:, c:c + r].T for _, c, r, _ in _COL_OUTS[:-1]] + [gate_t], axis=0)
    return w_rows.astype(BF16), w_cols.astype(BF16)


def _inproj(x, g, w_rows, w_cols, *, ts=512):
    b, s, d = x.shape
    out_shape = [jax.ShapeDtypeStruct((b, heads, s, HEAD_DIM), dtype) for _, _, heads, dtype, _ in _ROW_OUTS]
    out_specs = [pl.BlockSpec((1, heads, ts, HEAD_DIM), lambda bi, i: (bi, 0, i, 0)) for _, _, heads, _, _ in _ROW_OUTS]
    out_shape += [jax.ShapeDtypeStruct((b, rows, s), dtype) for _, _, rows, dtype in _COL_OUTS]
    out_specs += [pl.BlockSpec((1, rows, ts), lambda bi, i: (bi, 0, i)) for _, _, rows, _ in _COL_OUTS]
    outs = pl.pallas_call(
        _inproj_kernel,
        grid=(b, s // ts),
        in_specs=[
            pl.BlockSpec((1, ts, d), lambda bi, i: (bi, i, 0)),
            pl.BlockSpec((1, d), lambda bi, i: (0, 0)),
            pl.BlockSpec(w_rows.shape, lambda bi, i: (0, 0)),
            pl.BlockSpec(w_cols.shape, lambda bi, i: (0, 0)),
        ],
        out_specs=out_specs,
        out_shape=out_shape,
        compiler_params=_params("parallel", "parallel"),
        name="inproj",
    )(x, g.reshape(1, d), w_rows, w_cols)
    names = [o[0] for o in _ROW_OUTS] + [o[0] for o in _COL_OUTS]
    return dict(zip(names, outs))


def _tile_step(chains, s_ref, p_ref, c=LOG2E):
    tk, tq = s_ref.shape[1:]
    chunk = 4 * SUBLANES
    scaled = []
    for ci, (k, q, _, bias, mask, (m, _, _)) in enumerate(chains):
        s = lax.dot_general(k, q, NT_DIMS, preferred_element_type=F32)
        if bias is not None:
            s = s + bias
        if mask is not None:
            s = jnp.where(mask, s, NEG_INF)
        s_ref[ci] = s
        m_new = jnp.maximum(m, jnp.max(s, axis=0, keepdims=True))
        scaled.append((m_new, jnp.exp2((m - m_new) * c)))
    sums = [jnp.zeros((SUBLANES, tq), F32) for _ in chains]
    for r in range(tk // chunk):
        for ci in range(len(chains)):
            p = jnp.exp2((s_ref[ci, r * chunk:(r + 1) * chunk, :] - scaled[ci][0]) * c)
            for g in range(chunk // SUBLANES):
                sums[ci] = sums[ci] + p[g * SUBLANES:(g + 1) * SUBLANES, :]
            p_ref[ci, r * chunk:(r + 1) * chunk, :] = p.astype(BF16)
    out = []
    for ci, (_, _, v_t, _, _, (_, l, acc)) in enumerate(chains):
        m_new, alpha = scaled[ci]
        l = alpha * l + jnp.sum(sums[ci], axis=0, keepdims=True)
        acc = alpha * acc + jnp.dot(v_t, p_ref[ci], preferred_element_type=F32)
        out.append((m_new, l, acc))
    return out


def _init_state(tq, dv):
    return (jnp.full((1, tq), NEG_INF, F32), jnp.zeros((1, tq), F32), jnp.zeros((dv, tq), F32))


def _rows(ref, j, t=ATT_TILE):
    return ref[pl.ds(pl.multiple_of(j * t, t), t), :]


def _cols(ref, j, t=ATT_TILE):
    return ref[:, pl.ds(pl.multiple_of(j * t, t), t)]


def _topk_rows(score, n_rows, topk):
    row_id = lax.broadcasted_iota(jnp.int32, score.shape, 0)
    rank = jnp.zeros(score.shape, jnp.int32)
    for c in range(n_rows):
        other = score[c:c + 1, :]
        beats = (other > score) | ((other == score) & (row_id > c))
        rank = rank + beats.astype(jnp.int32)
    return (rank < topk) & (row_id < n_rows)


def _store_replicated(ref, rows, n_rows):
    for r in range(n_rows):
        ref[SUBLANES * r:SUBLANES * (r + 1), :] = jnp.broadcast_to(rows[r:r + 1, :], (SUBLANES, rows.shape[1]))


def _replicated_row(ref, r):
    return ref[pl.ds(pl.multiple_of(r * SUBLANES, SUBLANES), SUBLANES), :][0:1, :]


def _moba_kernel(q_ref, k_ref, vt_ref, near_ref, avg_ref, o_ref, kmean_ref, sel_ref, s_ref, p_ref):
    i = pl.program_id(1)
    t = ATT_TILE
    n_blk = k_ref.shape[2] // MOBA_BLOCK
    heads = range(MOBA_HEADS)

    @pl.when(i == 0)
    def _():
        for h in heads:
            kmean_ref[h] = jnp.dot(avg_ref[...], k_ref[0, h], preferred_element_type=F32)

    key = lax.broadcasted_iota(jnp.int32, (t, t), 0)
    qry = lax.broadcasted_iota(jnp.int32, (t, t), 1)
    causal = key <= qry
    blk = lax.broadcasted_iota(jnp.int32, (kmean_ref.shape[1], t), 0)
    qs = [q_ref[0, h] for h in heads]

    for h in heads:
        k_mean = kmean_ref[h]
        km_hi = k_mean.astype(BF16)
        km_lo = (k_mean - km_hi.astype(F32)).astype(BF16)
        gate = (lax.dot_general(km_hi, qs[h], NT_DIMS, preferred_element_type=F32)
                + lax.dot_general(km_lo, qs[h], NT_DIMS, preferred_element_type=F32))
        gate = jnp.where(blk < i, gate, NEG_INF)
        sel = _topk_rows(gate, n_blk, MOBA_TOPK) & (blk < i)
        _store_replicated(sel_ref.at[h], sel.astype(F32), n_blk)

    def step(j, near_idx, diag, states):
        chains = []
        for h in heads:
            k = _rows(k_ref.at[0, h], j)
            v_t = _cols(vt_ref.at[0, h * HEAD_DIM:(h + 1) * HEAD_DIM], j)
            bias = None if near_idx is None else near_ref[h, near_idx]
            mask = causal if diag else _replicated_row(sel_ref.at[h], j) > 0.5
            chains.append((k, qs[h], v_t, bias, mask, states[h]))
        return tuple(_tile_step(chains, s_ref, p_ref))

    states = tuple(_init_state(t, HEAD_DIM) for _ in heads)
    states = step(i, 0, True, states)
    prev = jnp.maximum(i - 1, 0)
    states = lax.fori_loop(prev, i, lambda j, st: step(j, 1, False, st), states)
    states = lax.fori_loop(0, prev, lambda j, st: step(j, None, False, st), states)
    o_t = jnp.concatenate([acc / l for _, l, acc in states], axis=0)
    o_ref[0] = o_t.T.astype(o_ref.dtype)


def _moba(mq, mk, mv_t, bias_near, avg):
    b, h, s, dh = mq.shape
    t = ATT_TILE
    assert MOBA_BLOCK == t and s % t == 0 and s // t <= avg.shape[0]
    return pl.pallas_call(
        _moba_kernel,
        grid=(b, s // t),
        in_specs=[
            pl.BlockSpec((1, h, t, dh), lambda bi, i: (bi, 0, i, 0)),
            pl.BlockSpec((1, h, s, dh), lambda bi, i: (bi, 0, 0, 0)),
            pl.BlockSpec((1, h * dh, s), lambda bi, i: (bi, 0, 0)),
            pl.BlockSpec(bias_near.shape, lambda bi, i: (0, 0, 0, 0)),
            pl.BlockSpec(avg.shape, lambda bi, i: (0, 0)),
        ],
        out_specs=pl.BlockSpec((1, t, h * dh), lambda bi, i: (bi, i, 0)),
        out_shape=jax.ShapeDtypeStruct((b, s, h * dh), BF16),
        scratch_shapes=[pltpu.VMEM((h, avg.shape[0], dh), F32),
                        pltpu.VMEM((h, SUBLANES * (s // t), t), F32),
                        pltpu.VMEM((h, t, t), F32), pltpu.VMEM((h, t, t), BF16)],
        compiler_params=_params("parallel", "arbitrary"),
        name="moba",
    )(mq, mk, mv_t, bias_near, avg)


def _diff_kernel(q_ref, k_ref, vt_ref, near_ref, lam_ref, subln_ref, o_ref, s_ref, p_ref, *, lambda_init):
    i = pl.program_id(1)
    t = ATT_TILE
    heads = range(DIFF_HEADS)
    key = lax.broadcasted_iota(jnp.int32, (t, t), 0)
    qry = lax.broadcasted_iota(jnp.int32, (t, t), 1)
    causal = key <= qry
    lane = lax.broadcasted_iota(jnp.int32, (t, HEAD_DIM), 1)
    c = DIFF_QK_DIM ** -0.5 * LOG2E
    lp = lam_ref[...]
    lam = (jnp.exp(jnp.sum(lp[0:1] * lp[1:2], axis=-1, keepdims=True))
           - jnp.exp(jnp.sum(lp[2:3] * lp[3:4], axis=-1, keepdims=True)) + lambda_init)
    qs = []
    for h in heads:
        q = q_ref[0, h]
        qs.append((jnp.where(lane < DIFF_QK_DIM, q, jnp.zeros_like(q)), jnp.where(lane >= DIFF_QK_DIM, q, jnp.zeros_like(q))))

    def step(j, near_idx, diag, states):
        chains = []
        for h in heads:
            k = _rows(k_ref.at[0, h], j)
            v_t = _cols(vt_ref.at[0, h * HEAD_DIM:(h + 1) * HEAD_DIM], j)
            bias = None if near_idx is None else near_ref[h, near_idx]
            mask = causal if diag else None
            chains += [(k, qs[h][mp], v_t, bias, mask, states[h][mp]) for mp in range(2)]
        out = _tile_step(chains, s_ref, p_ref, c=c)
        return tuple((out[2 * h], out[2 * h + 1]) for h in heads)

    states = tuple((_init_state(t, HEAD_DIM), _init_state(t, HEAD_DIM)) for _ in heads)
    states = step(i, 0, True, states)
    prev = jnp.maximum(i - 1, 0)
    states = lax.fori_loop(prev, i, lambda j, st: step(j, 1, False, st), states)
    states = lax.fori_loop(0, prev, lambda j, st: step(j, None, False, st), states)
    outs = []
    for h in heads:
        (_, l1, a1), (_, l2, a2) = states[h]
        o = a1 / l1 - lam * (a2 / l2)
        o = o * lax.rsqrt(jnp.mean(o * o, axis=0, keepdims=True) + RMS_EPS) * subln_ref[...]
        outs.append(o * (1.0 - lambda_init))
    o_ref[0] = jnp.concatenate(outs, axis=0).T.astype(o_ref.dtype)


def _diff(dq, dk, dv_t, bias_near, lam_params, subln_g, lambda_init):
    b, h, s, dh = dq.shape
    t = ATT_TILE
    return pl.pallas_call(
        functools.partial(_diff_kernel, lambda_init=lambda_init),
        grid=(b, s // t),
        in_specs=[
            pl.BlockSpec((1, h, t, dh), lambda bi, i: (bi, 0, i, 0)),
            pl.BlockSpec((1, h, s, dh), lambda bi, i: (bi, 0, 0, 0)),
            pl.BlockSpec((1, h * dh, s), lambda bi, i: (bi, 0, 0)),
            pl.BlockSpec(bias_near.shape, lambda bi, i: (0, 0, 0, 0)),
            pl.BlockSpec(lam_params.shape, lambda bi, i: (0, 0)),
            pl.BlockSpec((dh, 1), lambda bi, i: (0, 0)),
        ],
        out_specs=pl.BlockSpec((1, t, h * dh), lambda bi, i: (bi, i, 0)),
        out_shape=jax.ShapeDtypeStruct((b, s, h * dh), BF16),
        scratch_shapes=[pltpu.VMEM((2 * h, t, t), F32), pltpu.VMEM((2 * h, t, t), BF16)],
        compiler_params=_params("parallel", "arbitrary"),
        name="diff",
    )(dq, dk, dv_t, bias_near, lam_params, subln_g.reshape(dh, 1))


def _compress_kernel(kc_ref, vc_ref, pe_ref, w1_ref, w2k_ref, w2vt_ref, ko_ref, vo_ref):
    half = NSA_CMP_STRIDE * HEAD_DIM

    def hidden(t, c_ref):
        c = c_ref[0, 0]
        top = (c + pe_ref[2 * t:2 * t + 1]).astype(BF16)
        bot = (c + pe_ref[2 * t + 1:2 * t + 2]).astype(BF16)
        a = jnp.dot(top, w1_ref[t, :half], preferred_element_type=F32)
        bm = jnp.dot(bot, w1_ref[t, half:], preferred_element_type=F32)
        hid = a + pltpu.roll(bm, bm.shape[0] - 1, 0)
        return jax.nn.gelu(hid).astype(BF16)

    ko_ref[0, 0] = jnp.dot(hidden(0, kc_ref), w2k_ref[...], preferred_element_type=F32)
    vo_ref[0, 0] = lax.dot_general(w2vt_ref[...], hidden(1, vc_ref), NT_DIMS, preferred_element_type=F32)


def _compress(kc, vc, pe4, w1, w2k, w2v_t):
    b, g, s, dh = kc.shape
    n_chunk = s // NSA_CMP_STRIDE
    width = NSA_CMP_STRIDE * dh
    kc = kc.reshape(b, g, n_chunk, width)
    vc = vc.reshape(b, g, n_chunk, width)
    spec_in = pl.BlockSpec((1, 1, n_chunk, width), lambda bi, gi: (bi, gi, 0, 0))
    return pl.pallas_call(
        _compress_kernel,
        grid=(b, g),
        in_specs=[
            spec_in, spec_in,
            pl.BlockSpec(pe4.shape, lambda bi, gi: (0, 0)),
            pl.BlockSpec(w1.shape, lambda bi, gi: (0, 0, 0)),
            pl.BlockSpec(w2k.shape, lambda bi, gi: (0, 0)),
            pl.BlockSpec(w2v_t.shape, lambda bi, gi: (0, 0)),
        ],
        out_specs=[pl.BlockSpec((1, 1, n_chunk, dh), lambda bi, gi: (bi, gi, 0, 0)),
                   pl.BlockSpec((1, 1, dh, n_chunk), lambda bi, gi: (bi, gi, 0, 0))],
        out_shape=[jax.ShapeDtypeStruct((b, g, n_chunk, dh), F32), jax.ShapeDtypeStruct((b, g, dh, n_chunk), F32)],
        compiler_params=_params("parallel", "parallel"),
        name="nsa_compress",
    )(kc, vc, pe4, w1, w2k, w2v_t)


def _nsa_kernel(q_ref, kcmp_ref, vcmpt_ref, ks_ref, vst_ref, kw_ref, vwt_ref, gate_ref, near_ref, overlap_ref,
                o_ref, sel_ref, s_ref, p_ref):
    i = pl.program_id(2)
    t = ATT_TILE
    heads = range(NSA_GROUP)
    n_chunk = kcmp_ref.shape[2]
    n_slc = ks_ref.shape[2] // NSA_SLC_BLOCK
    per_tile = t // NSA_SLC_BLOCK
    key = lax.broadcasted_iota(jnp.int32, (t, t), 0)
    qry = lax.broadcasted_iota(jnp.int32, (t, t), 1)
    causal = key <= qry
    qs = [q_ref[0, r] for r in heads]
    ks_g, vst_g = ks_ref.at[0, 0], vst_ref.at[0]
    kw_g, vwt_g = kw_ref.at[0, 0], vwt_ref.at[0]

    cmp_id = lax.broadcasted_iota(jnp.int32, (n_chunk, t), 0)
    q_pos = i * t + lax.broadcasted_iota(jnp.int32, (n_chunk, t), 1)
    cmp_valid = (cmp_id * NSA_CMP_STRIDE + (NSA_CMP_BLOCK - 1) <= q_pos) & (cmp_id < n_chunk - 1)
    k_cmp = kcmp_ref[0, 0].astype(BF16)
    v_cmp_t = vcmpt_ref[0, 0].astype(BF16)
    o_cmp = []
    p_sum = jnp.zeros((n_chunk, t), F32)
    for r in heads:
        s = lax.dot_general(k_cmp, qs[r], NT_DIMS, preferred_element_type=F32)
        m = jnp.max(jnp.where(cmp_valid, s, NEG_INF), axis=0, keepdims=True)
        e = jnp.where(cmp_valid, jnp.exp(s - m), 0.0)
        l = jnp.sum(e, axis=0, keepdims=True)
        p = e / jnp.where(l > 0.0, l, 1.0)
        p_sum = p_sum + p
        o_cmp.append(jnp.dot(v_cmp_t, p.astype(BF16), preferred_element_type=F32))

    ps_hi = p_sum.astype(BF16)
    ps_lo = (p_sum - ps_hi.astype(F32)).astype(BF16)
    imp = (jnp.dot(overlap_ref[...], ps_hi, preferred_element_type=F32)
           + jnp.dot(overlap_ref[...], ps_lo, preferred_element_type=F32))
    blk = lax.broadcasted_iota(jnp.int32, (n_slc, t), 0)
    cur = (i * t + lax.broadcasted_iota(jnp.int32, (n_slc, t), 1)) // NSA_SLC_BLOCK
    forced = (blk == 0) | (blk > cur - NSA_LOCAL_BLOCKS)
    score = jnp.where(blk <= cur, jnp.where(forced, FORCE_SCORE, imp), NEG_INF)
    sel = _topk_rows(score, n_slc, min(NSA_SLC_TOPK, n_slc))
    _store_replicated(sel_ref, sel.astype(F32), n_slc)

    def sel_mask(j):
        slabs = [jnp.broadcast_to(_replicated_row(sel_ref, j * per_tile + bb), (NSA_SLC_BLOCK, t)) for bb in range(per_tile)]
        return jnp.concatenate(slabs, axis=0) > 0.5

    def step(j, near_idx, slc_mask, win_mask, slc, win):
        chains = []
        for r in heads:
            bias = None if near_idx is None else near_ref[0, r, near_idx]
            if slc is not None:
                chains.append((_rows(ks_g, j), qs[r], _cols(vst_g, j), bias, slc_mask, slc[r]))
            if win is not None:
                chains.append((_rows(kw_g, j), qs[r], _cols(vwt_g, j), bias, win_mask, win[r]))
        out = _tile_step(chains, s_ref, p_ref)
        if slc is None:
            return (), tuple(out)
        if win is None:
            return tuple(out), ()
        return tuple(out[0::2]), tuple(out[1::2])

    slc = tuple(_init_state(t, HEAD_DIM) for _ in heads)
    win = tuple(_init_state(t, HEAD_DIM) for _ in heads)
    slc, win = step(i, 0, causal & sel_mask(i), causal, slc, win)
    prev = jnp.maximum(i - 1, 0)
    prev2 = jnp.maximum(i - 2, 0)
    slc, win = lax.fori_loop(prev, i, lambda j, c: step(j, 1, sel_mask(j), None, c[0], c[1]), (slc, win))
    slc = lax.fori_loop(0, prev, lambda j, c: step(j, None, sel_mask(j), None, c, None)[0], slc)
    win = lax.fori_loop(prev2, prev, lambda j, c: step(j, None, None, key > qry, None, c)[1], win)

    gates = jax.nn.sigmoid(gate_ref[0])
    outs = []
    for r in heads:
        g_cmp, g_slc, g_win = (gates[3 * r + br:3 * r + br + 1, :] for br in range(3))
        outs.append(g_cmp * o_cmp[r] + g_slc * (slc[r][2] / slc[r][1]) + g_win * (win[r][2] / win[r][1]))
    o_ref[0] = jnp.concatenate(outs, axis=0).T.astype(o_ref.dtype)


def _nsa(nq, k_cmp, v_cmp_t, ks, vs_t, kw, vw_t, gate_t, bias_near, overlap_t):
    b, _, s, dh = nq.shape
    g, r, t = NSA_KV_HEADS, NSA_GROUP, ATT_TILE
    assert NSA_WINDOW == 2 * t and s % t == 0 and t % NSA_SLC_BLOCK == 0
    k_spec = pl.BlockSpec((1, 1, s, dh), lambda bi, gi, i: (bi, gi, 0, 0))
    vt_spec = pl.BlockSpec((1, dh, s), lambda bi, gi, i: (bi, gi, 0))
    return pl.pallas_call(
        _nsa_kernel,
        grid=(b, g, s // t),
        in_specs=[
            pl.BlockSpec((1, r, t, dh), lambda bi, gi, i: (bi, gi, i, 0)),
            pl.BlockSpec((1, 1) + k_cmp.shape[2:], lambda bi, gi, i: (bi, gi, 0, 0)),
            pl.BlockSpec((1, 1) + v_cmp_t.shape[2:], lambda bi, gi, i: (bi, gi, 0, 0)),
            k_spec, vt_spec, k_spec, vt_spec,
            pl.BlockSpec((1, NSA_GATE_ROWS, t), lambda bi, gi, i: (bi, gi, i)),
            pl.BlockSpec((1, r, 2, t, t), lambda bi, gi, i: (gi, 0, 0, 0, 0)),
            pl.BlockSpec(overlap_t.shape, lambda bi, gi, i: (0, 0)),
        ],
        out_specs=pl.BlockSpec((1, t, r * dh), lambda bi, gi, i: (bi, i, gi)),
        out_shape=jax.ShapeDtypeStruct((b, s, g * r * dh), BF16),
        scratch_shapes=[pltpu.VMEM((SUBLANES * (s // NSA_SLC_BLOCK), t), F32),
                        pltpu.VMEM((2 * r, t, t), F32), pltpu.VMEM((2 * r, t, t), BF16)],
        compiler_params=_params("parallel", "parallel", "arbitrary"),
        name="nsa",
    )(nq, k_cmp, v_cmp_t, ks, vs_t, kw, vw_t, gate_t, bias_near, overlap_t)


def _outproj_kernel(x_ref, om_ref, od_ref, on_ref, w_ref, o_ref):
    wm = om_ref.shape[-1]
    wd = od_ref.shape[-1]
    y = jnp.dot(om_ref[0], w_ref[:wm], preferred_element_type=F32)
    y = y + jnp.dot(od_ref[0], w_ref[wm:wm + wd], preferred_element_type=F32)
    y = y + jnp.dot(on_ref[0], w_ref[wm + wd:], preferred_element_type=F32)
    o_ref[0] = x_ref[0] + y


def _outproj(x, o_m, o_d, o_n, w, *, ts=512):
    b, s, d = x.shape

    def spec(a):
        return pl.BlockSpec((1, ts, a.shape[-1]), lambda bi, i: (bi, i, 0))

    return pl.pallas_call(
        _outproj_kernel,
        grid=(b, s // ts),
        in_specs=[spec(x), spec(o_m), spec(o_d), spec(o_n), pl.BlockSpec(w.shape, lambda bi, i: (0, 0))],
        out_specs=spec(x),
        out_shape=jax.ShapeDtypeStruct((b, s, d), F32),
        compiler_params=_params("parallel", "parallel"),
        name="outproj",
    )(x, o_m, o_d, o_n, w)


def _final_norm_kernel(x_ref, g_ref, o_ref):
    o_ref[...] = _rms(x_ref[...], g_ref[...])


def _final_norm(x2, g, *, tm=1024):
    n, d = x2.shape
    return pl.pallas_call(
        _final_norm_kernel,
        grid=(n // tm,),
        in_specs=[pl.BlockSpec((tm, d), lambda i: (i, 0)), pl.BlockSpec((1, d), lambda i: (0, 0))],
        out_specs=pl.BlockSpec((tm, d), lambda i: (i, 0)),
        out_shape=jax.ShapeDtypeStruct((n, d), F32),
        compiler_params=_params("parallel"),
        name="final_norm",
    )(x2, g.reshape(1, d))


def _rel_bucket(dist):
    n = jnp.maximum(dist, 0)
    max_exact = REL_BUCKETS // 2
    n_f = jnp.maximum(n, max_exact).astype(F32)
    large = max_exact + (jnp.log(n_f / max_exact) / math.log(REL_MAX_DIST / max_exact)
                         * (REL_BUCKETS - max_exact)).astype(jnp.int32)
    return jnp.where(n < max_exact, n, jnp.minimum(large, REL_BUCKETS - 1))


def _bias_tiles(rel_bias):
    t = ATT_TILE
    assert t >= REL_MAX_DIST
    heads = rel_bias.shape[1]
    width = 2 * t + 1
    by_dist = rel_bias[_rel_bucket(jnp.arange(width))] - rel_bias[REL_BUCKETS - 1]
    skew = jnp.broadcast_to(by_dist.T[:, None, :], (heads, t, width)).reshape(heads, t * width)
    skew = skew[:, :t * (width - 1)].reshape(heads, t, width - 1)
    return jnp.stack([skew[:, :, :t], skew[:, :, t:]], axis=1)


def _const_tables(s):
    n_moba = s // MOBA_BLOCK
    avg = np.zeros((2 * SUBLANES, s), np.float32)
    for j in range(n_moba):
        avg[j, j * MOBA_BLOCK:(j + 1) * MOBA_BLOCK] = 1.0 / MOBA_BLOCK
    n_cmp = (s - NSA_CMP_BLOCK) // NSA_CMP_STRIDE + 1
    n_slc = s // NSA_SLC_BLOCK
    cmp_start = np.arange(n_cmp) * NSA_CMP_STRIDE
    slc_start = np.arange(n_slc) * NSA_SLC_BLOCK
    ov = np.clip(np.minimum(cmp_start[:, None] + NSA_CMP_BLOCK, slc_start[None, :] + NSA_SLC_BLOCK)
                 - np.maximum(cmp_start[:, None], slc_start[None, :]), 0, None) / NSA_CMP_BLOCK
    overlap_t = np.zeros((n_slc, s // NSA_CMP_STRIDE), np.float32)
    overlap_t[:, :n_cmp] = ov.T
    return jnp.asarray(avg, BF16), jnp.asarray(overlap_t, BF16)


def kernel(x, rel_bias, norm_ffn1, ffn1_gate, ffn1_up, ffn1_down, norm_mix, w_in, diff_lambda, diff_subln, nsa_cmp_pe, nsa_cmp_w1, nsa_cmp_w2, w_out, norm_ffn2, ffn2_gate, ffn2_up, ffn2_down, final_norm):
    b, s, d = x.shape
    depth = w_in.shape[0]
    assert d == D_MODEL
    h0, h1 = MOBA_HEADS, MOBA_HEADS + DIFF_HEADS
    bias_near = _bias_tiles(rel_bias.astype(F32))
    moba_near = bias_near[:h0]
    diff_near = bias_near[h0:h1] / (DIFF_QK_DIM ** -0.5)
    nsa_near = bias_near[h1:].reshape(NSA_KV_HEADS, NSA_GROUP, 2, ATT_TILE, ATT_TILE)
    avg, overlap_t = _const_tables(s)
    bf = lambda a: a.astype(BF16)

    for l in range(depth):
        lambda_init = 0.8 - 0.6 * math.exp(-0.3 * l)
        x = _ffn(x.reshape(b * s, d), norm_ffn1[l], bf(ffn1_gate[l]), bf(ffn1_up[l]), bf(ffn1_down[l])).reshape(b, s, d)

        p = _inproj(x, norm_mix[l], *_inproj_weights(w_in[l]))
        o_moba = _moba(p["mq"], p["mk"], p["mvT"], moba_near, avg)
        o_diff = _diff(p["dq"], p["dk"], p["dvT"], diff_near, diff_lambda[l].astype(F32), diff_subln[l].astype(F32),
                       lambda_init)
        pe4 = nsa_cmp_pe[l].astype(F32).reshape(4, NSA_CMP_STRIDE * HEAD_DIM)
        k_cmp, v_cmp_t = _compress(p["kc"], p["vc"], pe4, bf(nsa_cmp_w1[l]), bf(nsa_cmp_w2[l, 0]), bf(nsa_cmp_w2[l, 1].T))
        o_nsa = _nsa(p["nq"], k_cmp, v_cmp_t, p["ks"], p["vsT"], p["kw"], p["vwT"], p["gateT"], nsa_near, overlap_t)
        x = _outproj(x, o_moba, o_diff, o_nsa, bf(w_out[l]))

        x = _ffn(x.reshape(b * s, d), norm_ffn2[l], bf(ffn2_gate[l]), bf(ffn2_up[l]), bf(ffn2_down[l])).reshape(b, s, d)
    return _final_norm(x.reshape(b * s, d), final_norm).reshape(b, s, d)
```

```python
import functools
import math

import numpy as np
import jax
import jax.numpy as jnp
from jax import lax
from jax.experimental import pallas as pl
from jax.experimental.pallas import tpu as pltpu

F32 = jnp.float32
BF16 = jnp.bfloat16

D_MODEL = 1024
HEAD_DIM = 64
MOBA_HEADS = 4
MOBA_BLOCK = 256
MOBA_TOPK = 3
DIFF_HEADS = 4
DIFF_QK_DIM = HEAD_DIM // 2
NSA_HEADS = 8
NSA_KV_HEADS = 2
NSA_GROUP = NSA_HEADS // NSA_KV_HEADS
NSA_CMP_BLOCK = 32
NSA_CMP_STRIDE = 16
NSA_CMP_HIDDEN = 256
NSA_SLC_BLOCK = 64
NSA_SLC_TOPK = 16
NSA_LOCAL_BLOCKS = 2
NSA_WINDOW = 512
REL_BUCKETS = 32
REL_MAX_DIST = 128
D_FF = 2816
RMS_EPS = 1e-6
NEG_INF = -1e30
FORCE_SCORE = 1e4

NSA_GATE_W = 3 * NSA_HEADS
NSA_GATE_ROWS = 16
LANES = 128
SUBLANES = 8
ATT_TILE = 256
VMEM_LIMIT = 48 * 1024 * 1024
LOG2E = math.log2(math.e)

NT_DIMS = (((1,), (1,)), ((), ()))


def _rms(x, g):
    return x * lax.rsqrt(jnp.mean(x * x, axis=-1, keepdims=True) + RMS_EPS) * g


def _params(*sem):
    return pltpu.CompilerParams(dimension_semantics=sem, vmem_limit_bytes=VMEM_LIMIT)


def _ffn_kernel(x_ref, g_ref, wg_ref, wu_ref, wd_ref, o_ref, xn_ref, acc_ref):
    f = pl.program_id(1)

    @pl.when(f == 0)
    def _():
        xn_ref[...] = _rms(x_ref[...], g_ref[...]).astype(BF16)
        acc_ref[...] = jnp.zeros_like(acc_ref)

    xn = xn_ref[...]
    gate = jnp.dot(xn, wg_ref[...], preferred_element_type=F32)
    up = jnp.dot(xn, wu_ref[...], preferred_element_type=F32)
    h = (gate * jax.nn.sigmoid(gate) * up).astype(BF16)
    acc_ref[...] += jnp.dot(h, wd_ref[...], preferred_element_type=F32)

    @pl.when(f == pl.num_programs(1) - 1)
    def _():
        o_ref[...] = x_ref[...] + 0.5 * acc_ref[...]


def _ffn(x2, g, wg, wu, wd, *, tm=1024, tf=256):
    n, d = x2.shape
    ff = wg.shape[1]
    return pl.pallas_call(
        _ffn_kernel,
        grid=(n // tm, ff // tf),
        in_specs=[
            pl.BlockSpec((tm, d), lambda i, f: (i, 0)),
            pl.BlockSpec((1, d), lambda i, f: (0, 0)),
            pl.BlockSpec((d, tf), lambda i, f: (0, f)),
            pl.BlockSpec((d, tf), lambda i, f: (0, f)),
            pl.BlockSpec((tf, d), lambda i, f: (f, 0)),
        ],
        out_specs=pl.BlockSpec((tm, d), lambda i, f: (i, 0)),
        out_shape=jax.ShapeDtypeStruct((n, d), F32),
        scratch_shapes=[pltpu.VMEM((tm, d), BF16), pltpu.VMEM((tm, d), F32)],
        compiler_params=_params("parallel", "arbitrary"),
        name="ffn",
    )(x2, g.reshape(1, d), wg, wu, wd)


_ROW_OUTS = (
    ("mq", 0, 4, BF16, HEAD_DIM ** -0.5, None),
    ("mk", 256, 4, BF16, 1.0, 0),
    ("dq", 768, 4, BF16, 1.0, None),
    ("dk", 1024, 4, BF16, 1.0, None),
    ("nq", 1536, 8, BF16, HEAD_DIM ** -0.5, None),
    ("kc", 2048, 2, F32, 1.0, None),
    ("vc", 2176, 2, F32, 1.0, None),
    ("ks", 2304, 2, BF16, 1.0, 1),
    ("kw", 2560, 2, BF16, 1.0, None),
)
_MEMBER_BLOCKS = (MOBA_BLOCK, NSA_SLC_BLOCK)
_COL_OUTS = (
    ("mvT", 512, 256, BF16),
    ("dvT", 1280, 256, BF16),
    ("vsT", 2432, 128, BF16),
    ("vwT", 2688, 128, BF16),
    ("gateT", None, NSA_KV_HEADS * NSA_GATE_ROWS, F32),
)


def _inproj_kernel(x_ref, g_ref, w_ref, wt_ref, member_ref, *out_refs):
    xn = _rms(x_ref[0], g_ref[...]).astype(BF16)
    col = 0
    for (_, _, heads, dtype, scale, member), o_ref in zip(_ROW_OUTS, out_refs):
        width = heads * HEAD_DIM
        p = jnp.dot(xn, w_ref[:, col:col + width], preferred_element_type=F32)
        if scale != 1.0:
            p = p * scale
        for h in range(heads):
            head = p[:, h * HEAD_DIM:(h + 1) * HEAD_DIM].astype(dtype)
            o_ref[0, h] = head if member is None else jnp.concatenate([head, member_ref[member]], axis=1)
        col += width
    pt = lax.dot_general(wt_ref[...], xn, NT_DIMS, preferred_element_type=F32)
    row = 0
    for (_, _, rows, dtype), o_ref in zip(_COL_OUTS, out_refs[len(_ROW_OUTS):]):
        o_ref[0] = pt[row:row + rows].astype(dtype)
        row += rows


def _inproj_weights(w_l):
    w_rows = jnp.concatenate([w_l[:, c:c + h * HEAD_DIM] for _, c, h, _, _, _ in _ROW_OUTS], axis=1)
    gate_cols = w_l[:, w_l.shape[1] - NSA_GATE_W:]
    per_group = 3 * NSA_GROUP
    gate_t = jnp.zeros((NSA_KV_HEADS * NSA_GATE_ROWS, w_l.shape[0]), w_l.dtype)
    for g in range(NSA_KV_HEADS):
        gate_t = gate_t.at[g * NSA_GATE_ROWS:g * NSA_GATE_ROWS + per_group].set(
            gate_cols[:, g * per_group:(g + 1) * per_group].T)
    w_cols = jnp.concatenate([w_l[:, c:c + r].T for _, c, r, _ in _COL_OUTS[:-1]] + [gate_t], axis=0)
    return w_rows.astype(BF16), w_cols.astype(BF16)


def _inproj(x, g, w_rows, w_cols, member, *, ts=512):
    b, s, d = x.shape
    widths = [HEAD_DIM if member is None else 2 * HEAD_DIM for _, _, _, _, _, member in _ROW_OUTS]
    out_shape = [jax.ShapeDtypeStruct((b, o[2], s, w), o[3]) for o, w in zip(_ROW_OUTS, widths)]
    out_specs = [pl.BlockSpec((1, o[2], ts, w), lambda bi, i: (bi, 0, i, 0)) for o, w in zip(_ROW_OUTS, widths)]
    out_shape += [jax.ShapeDtypeStruct((b, rows, s), dtype) for _, _, rows, dtype in _COL_OUTS]
    out_specs += [pl.BlockSpec((1, rows, ts), lambda bi, i: (bi, 0, i)) for _, _, rows, _ in _COL_OUTS]
    outs = pl.pallas_call(
        _inproj_kernel,
        grid=(b, s // ts),
        in_specs=[
            pl.BlockSpec((1, ts, d), lambda bi, i: (bi, i, 0)),
            pl.BlockSpec((1, d), lambda bi, i: (0, 0)),
            pl.BlockSpec(w_rows.shape, lambda bi, i: (0, 0)),
            pl.BlockSpec(w_cols.shape, lambda bi, i: (0, 0)),
            pl.BlockSpec((member.shape[0], ts, HEAD_DIM), lambda bi, i: (0, i, 0)),
        ],
        out_specs=out_specs,
        out_shape=out_shape,
        compiler_params=_params("parallel", "parallel"),
        name="inproj",
    )(x, g.reshape(1, d), w_rows, w_cols, member)
    names = [o[0] for o in _ROW_OUTS] + [o[0] for o in _COL_OUTS]
    return dict(zip(names, outs))


ONES_ROWS = 16
EXP_ROWS = 32


def _scores(chains, s_ref, c):
    out = []
    for ci, (k, q, adds, m) in enumerate(chains):
        s = lax.dot_general(k, q, NT_DIMS, preferred_element_type=F32)
        for a in adds:
            s = s + a
        s_ref[ci] = s
        m_new = jnp.maximum(m, jnp.max(s, axis=0, keepdims=True))
        out.append((m_new, jnp.exp2((m - m_new) * c)))
    return out


def _accumulate(chains, s_ref, p_ref, c):
    tk = s_ref.shape[1]
    out = []
    for ci, (v_t, m_new, alpha, acc) in enumerate(chains):
        for r in range(tk // EXP_ROWS):
            rows = slice(r * EXP_ROWS, (r + 1) * EXP_ROWS)
            p_ref[ci, rows, :] = jnp.exp2(((s_ref[ci, rows, :] - m_new) * c).astype(BF16))
        v_ext = jnp.concatenate([v_t, jnp.ones((ONES_ROWS, tk), BF16)], axis=0)
        out.append(alpha * acc + jnp.dot(v_ext, p_ref[ci], preferred_element_type=F32))
    return out


def _run_if(cond, body, carry):
    return lax.fori_loop(0, cond.astype(jnp.int32), lambda _, c: body(c), carry)


def _attend(i, n_near, max_near, n_far, n, tq, dv, score_inputs, value_tiles, s_refs, p_refs, c=LOG2E):
    def scores(j, pend, slot, d):
        chains = [(k, q, adds, m) for (k, q, adds), (m, _) in zip(score_inputs(j, d), pend)]
        return tuple(_scores(chains, s_refs[slot], c))

    def accumulate(j, pend, accs, slot):
        chains = [(v_t, m_new, alpha, acc) for v_t, (m_new, alpha), acc in zip(value_tiles(j), pend, accs)]
        return tuple(_accumulate(chains, s_refs[slot], p_refs[slot], c))

    start = [(jnp.full((1, tq), NEG_INF, F32), None)] * n
    accs = tuple(jnp.zeros((dv + ONES_ROWS, tq), F32) for _ in range(n))
    carry = (scores(i, start, 0, 0), accs)
    for d in range(1, max_near + 1):
        def near(carry, d=d):
            pend, accs = carry
            return scores(i - d, pend, d % 2, d), accumulate(i - d + 1, pend, accs, (d - 1) % 2)
        carry = _run_if(n_near >= d, near, carry)

    a, b = max_near % 2, 1 - max_near % 2
    first = i - max_near - 1

    def pair(t, carry):
        j_pend, pend, accs = carry
        j0 = first - 2 * t
        pend_b = scores(j0, pend, b, None)
        accs = accumulate(j_pend, pend, accs, a)
        pend_a = scores(j0 - 1, pend_b, a, None)
        return j0 - 1, pend_a, accumulate(j0, pend_b, accs, b)

    def single(j, carry):
        j_pend, pend, accs = carry
        accs = accumulate(j_pend, pend, accs, a)
        return j, scores(j, pend, a, None), accs

    carry = (i - n_near,) + carry
    if n_far is not None:
        carry = lax.fori_loop(0, n_far >> 1, pair, carry)
        last = first - n_far + 1
        carry = lax.fori_loop(last, last + (n_far & 1), single, carry)
    j_pend, pend, accs = carry
    for slot in range(2):
        accs = _run_if((n_near & 1) == slot, lambda accs, slot=slot: accumulate(j_pend, pend, accs, slot), accs)
    return [acc[:dv] / acc[dv:dv + 1] for acc in accs]


def _rows(ref, j, t=ATT_TILE):
    return ref[pl.ds(pl.multiple_of(j * t, t), t), :]


def _cols(ref, j, t=ATT_TILE):
    return ref[:, pl.ds(pl.multiple_of(j * t, t), t)]


def _topk_rows(score, n_rows, topk):
    row_id = lax.broadcasted_iota(jnp.int32, score.shape, 0)
    rank = jnp.zeros(score.shape, jnp.int32)
    for c in range(n_rows):
        other = score[c:c + 1, :]
        beats = (other > score) | ((other == score) & (row_id > c))
        rank = rank + beats.astype(jnp.int32)
    return (rank < topk) & (row_id < n_rows)


def _extend_query(q, chosen):
    tq, dh = q.shape
    pen = jnp.where(chosen, 0.0, NEG_INF)
    pen = jnp.concatenate([pen, jnp.zeros((LANES - pen.shape[0], tq), F32)], axis=0)
    return jnp.concatenate([q, pen.T[:, :dh].astype(q.dtype)], axis=1)


def _moba_kernel(q_ref, k_ref, vt_ref, bias_ref, avg_ref, o_ref, kmean_ref, s0_ref, s1_ref, p0_ref, p1_ref):
    i = pl.program_id(1)
    t = ATT_TILE
    n_blk = k_ref.shape[2] // MOBA_BLOCK
    heads = range(MOBA_HEADS)

    @pl.when(i == 0)
    def _():
        for h in heads:
            k_mean = jnp.dot(avg_ref[...], k_ref[0, h], preferred_element_type=F32)
            kmean_ref[h] = k_mean[:, :HEAD_DIM]

    blk = lax.broadcasted_iota(jnp.int32, (kmean_ref.shape[1], t), 0)

    qs = []
    for h in heads:
        q = q_ref[0, h]
        k_mean = kmean_ref[h]
        km_hi = k_mean.astype(BF16)
        km_lo = (k_mean - km_hi.astype(F32)).astype(BF16)
        gate = (lax.dot_general(km_hi, q, NT_DIMS, preferred_element_type=F32)
                + lax.dot_general(km_lo, q, NT_DIMS, preferred_element_type=F32))
        gate = jnp.where(blk < i, gate, NEG_INF)
        sel = _topk_rows(gate, n_blk, MOBA_TOPK) & (blk < i)
        qs.append(_extend_query(q, sel | (blk == i)))

    def score_inputs(j, d):
        adds = lambda h: [] if d is None else [bias_ref[h, d]]
        return [(_rows(k_ref.at[0, h], j), qs[h], adds(h)) for h in heads]

    def value_tiles(j):
        return [_cols(vt_ref.at[0, h * HEAD_DIM:(h + 1) * HEAD_DIM], j) for h in heads]

    outs = _attend(i, jnp.minimum(i, 1), 1, jnp.maximum(i - 1, 0), len(heads), t, HEAD_DIM, score_inputs, value_tiles,
                   (s0_ref, s1_ref), (p0_ref, p1_ref))
    o_ref[0] = jnp.concatenate(outs, axis=0).T.astype(o_ref.dtype)


def _moba(mq, mk, mv_t, bias_near, avg):
    b, h, s, dh = mq.shape
    t = ATT_TILE
    assert MOBA_BLOCK == t and s % t == 0 and s // t <= avg.shape[0]
    return pl.pallas_call(
        _moba_kernel,
        grid=(b, s // t),
        in_specs=[
            pl.BlockSpec((1, h, t, dh), lambda bi, i: (bi, 0, i, 0)),
            pl.BlockSpec((1, h, s, mk.shape[3]), lambda bi, i: (bi, 0, 0, 0)),
            pl.BlockSpec((1, h * dh, s), lambda bi, i: (bi, 0, 0)),
            pl.BlockSpec(bias_near.shape, lambda bi, i: (0, 0, 0, 0)),
            pl.BlockSpec(avg.shape, lambda bi, i: (0, 0)),
        ],
        out_specs=pl.BlockSpec((1, t, h * dh), lambda bi, i: (bi, i, 0)),
        out_shape=jax.ShapeDtypeStruct((b, s, h * dh), BF16),
        scratch_shapes=[pltpu.VMEM((h, avg.shape[0], dh), F32),
                        pltpu.VMEM((h, t, t), F32), pltpu.VMEM((h, t, t), F32),
                        pltpu.VMEM((h, t, t), BF16), pltpu.VMEM((h, t, t), BF16)],
        compiler_params=_params("parallel", "arbitrary"),
        name="moba",
    )(mq, mk, mv_t, bias_near, avg)


def _diff_kernel(q_ref, k_ref, vt_ref, bias_ref, lam_ref, subln_ref, o_ref, s0_ref, s1_ref, p0_ref, p1_ref, *,
                 lambda_init):
    i = pl.program_id(1)
    t = ATT_TILE
    heads = range(DIFF_HEADS)
    lane = lax.broadcasted_iota(jnp.int32, (t, HEAD_DIM), 1)
    c = DIFF_QK_DIM ** -0.5 * LOG2E
    lp = lam_ref[...]
    lam = (jnp.exp(jnp.sum(lp[0:1] * lp[1:2], axis=-1, keepdims=True))
           - jnp.exp(jnp.sum(lp[2:3] * lp[3:4], axis=-1, keepdims=True)) + lambda_init)
    qs = []
    for h in heads:
        q = q_ref[0, h]
        qs.append((jnp.where(lane < DIFF_QK_DIM, q, jnp.zeros_like(q)), jnp.where(lane >= DIFF_QK_DIM, q, jnp.zeros_like(q))))

    def score_inputs(j, d):
        out = []
        for h in heads:
            k = _rows(k_ref.at[0, h], j)
            adds = [] if d is None else [bias_ref[h, d]]
            out += [(k, qs[h][mp], adds) for mp in range(2)]
        return out

    def value_tiles(j):
        tiles = [_cols(vt_ref.at[0, h * HEAD_DIM:(h + 1) * HEAD_DIM], j) for h in heads]
        return [tiles[h] for h in heads for _ in range(2)]

    maps = _attend(i, jnp.minimum(i, 1), 1, jnp.maximum(i - 1, 0), 2 * len(heads), t, HEAD_DIM, score_inputs, value_tiles,
                   (s0_ref, s1_ref), (p0_ref, p1_ref), c=c)
    outs = []
    for h in heads:
        o = maps[2 * h] - lam * maps[2 * h + 1]
        o = o * lax.rsqrt(jnp.mean(o * o, axis=0, keepdims=True) + RMS_EPS) * subln_ref[...]
        outs.append(o * (1.0 - lambda_init))
    o_ref[0] = jnp.concatenate(outs, axis=0).T.astype(o_ref.dtype)


def _diff(dq, dk, dv_t, bias_near, lam_params, subln_g, lambda_init):
    b, h, s, dh = dq.shape
    t = ATT_TILE
    return pl.pallas_call(
        functools.partial(_diff_kernel, lambda_init=lambda_init),
        grid=(b, s // t),
        in_specs=[
            pl.BlockSpec((1, h, t, dh), lambda bi, i: (bi, 0, i, 0)),
            pl.BlockSpec((1, h, s, dh), lambda bi, i: (bi, 0, 0, 0)),
            pl.BlockSpec((1, h * dh, s), lambda bi, i: (bi, 0, 0)),
            pl.BlockSpec(bias_near.shape, lambda bi, i: (0, 0, 0, 0)),
            pl.BlockSpec(lam_params.shape, lambda bi, i: (0, 0)),
            pl.BlockSpec((dh, 1), lambda bi, i: (0, 0)),
        ],
        out_specs=pl.BlockSpec((1, t, h * dh), lambda bi, i: (bi, i, 0)),
        out_shape=jax.ShapeDtypeStruct((b, s, h * dh), BF16),
        scratch_shapes=[pltpu.VMEM((2 * h, t, t), F32), pltpu.VMEM((2 * h, t, t), F32),
                        pltpu.VMEM((2 * h, t, t), BF16), pltpu.VMEM((2 * h, t, t), BF16)],
        compiler_params=_params("parallel", "arbitrary"),
        name="diff",
    )(dq, dk, dv_t, bias_near, lam_params, subln_g.reshape(dh, 1))


def _compress_kernel(kc_ref, vc_ref, pe_ref, w1_ref, w2k_ref, w2vt_ref, ko_ref, vo_ref):
    half = NSA_CMP_STRIDE * HEAD_DIM

    def hidden(t, c_ref):
        c = c_ref[0, 0]
        top = (c + pe_ref[2 * t:2 * t + 1]).astype(BF16)
        bot = (c + pe_ref[2 * t + 1:2 * t + 2]).astype(BF16)
        a = jnp.dot(top, w1_ref[t, :half], preferred_element_type=F32)
        bm = jnp.dot(bot, w1_ref[t, half:], preferred_element_type=F32)
        hid = a + pltpu.roll(bm, bm.shape[0] - 1, 0)
        return jax.nn.gelu(hid).astype(BF16)

    ko_ref[0, 0] = jnp.dot(hidden(0, kc_ref), w2k_ref[...], preferred_element_type=F32)
    vo_ref[0, 0] = lax.dot_general(w2vt_ref[...], hidden(1, vc_ref), NT_DIMS, preferred_element_type=F32)


def _compress(kc, vc, pe4, w1, w2k, w2v_t):
    b, g, s, dh = kc.shape
    n_chunk = s // NSA_CMP_STRIDE
    width = NSA_CMP_STRIDE * dh
    kc = kc.reshape(b, g, n_chunk, width)
    vc = vc.reshape(b, g, n_chunk, width)
    spec_in = pl.BlockSpec((1, 1, n_chunk, width), lambda bi, gi: (bi, gi, 0, 0))
    return pl.pallas_call(
        _compress_kernel,
        grid=(b, g),
        in_specs=[
            spec_in, spec_in,
            pl.BlockSpec(pe4.shape, lambda bi, gi: (0, 0)),
            pl.BlockSpec(w1.shape, lambda bi, gi: (0, 0, 0)),
            pl.BlockSpec(w2k.shape, lambda bi, gi: (0, 0)),
            pl.BlockSpec(w2v_t.shape, lambda bi, gi: (0, 0)),
        ],
        out_specs=[pl.BlockSpec((1, 1, n_chunk, dh), lambda bi, gi: (bi, gi, 0, 0)),
                   pl.BlockSpec((1, 1, dh, n_chunk), lambda bi, gi: (bi, gi, 0, 0))],
        out_shape=[jax.ShapeDtypeStruct((b, g, n_chunk, dh), F32), jax.ShapeDtypeStruct((b, g, dh, n_chunk), F32)],
        compiler_params=_params("parallel", "parallel"),
        name="nsa_compress",
    )(kc, vc, pe4, w1, w2k, w2v_t)


def _nsa_kernel(q_ref, kcmp_ref, vcmpt_ref, ks_ref, vst_ref, kw_ref, vwt_ref, gate_ref, bias_ref, overlap_ref,
                o_ref, s0_ref, s1_ref, p0_ref, p1_ref):
    i = pl.program_id(2)
    t = ATT_TILE
    heads = range(NSA_GROUP)
    n_chunk = kcmp_ref.shape[2]
    n_slc = ks_ref.shape[2] // NSA_SLC_BLOCK
    per_tile = t // NSA_SLC_BLOCK
    qs = [q_ref[0, r] for r in heads]
    ks_g, vst_g = ks_ref.at[0, 0], vst_ref.at[0]
    kw_g, vwt_g = kw_ref.at[0, 0], vwt_ref.at[0]

    cmp_id = lax.broadcasted_iota(jnp.int32, (n_chunk, t), 0)
    q_pos = i * t + lax.broadcasted_iota(jnp.int32, (n_chunk, t), 1)
    cmp_valid = (cmp_id * NSA_CMP_STRIDE + (NSA_CMP_BLOCK - 1) <= q_pos) & (cmp_id < n_chunk - 1)
    k_cmp = kcmp_ref[0, 0].astype(BF16)
    v_cmp_t = vcmpt_ref[0, 0].astype(BF16)
    o_cmp = []
    p_sum = jnp.zeros((n_chunk, t), F32)
    for r in heads:
        s = lax.dot_general(k_cmp, qs[r], NT_DIMS, preferred_element_type=F32)
        m = jnp.max(jnp.where(cmp_valid, s, NEG_INF), axis=0, keepdims=True)
        e = jnp.where(cmp_valid, jnp.exp(s - m), 0.0)
        l = jnp.sum(e, axis=0, keepdims=True)
        p = e / jnp.where(l > 0.0, l, 1.0)
        p_sum = p_sum + p
        o_cmp.append(jnp.dot(v_cmp_t, p.astype(BF16), preferred_element_type=F32))

    ps_hi = p_sum.astype(BF16)
    ps_lo = (p_sum - ps_hi.astype(F32)).astype(BF16)
    imp = (jnp.dot(overlap_ref[...], ps_hi, preferred_element_type=F32)
           + jnp.dot(overlap_ref[...], ps_lo, preferred_element_type=F32))
    blk = lax.broadcasted_iota(jnp.int32, (n_slc, t), 0)
    cur = (i * t + lax.broadcasted_iota(jnp.int32, (n_slc, t), 1)) // NSA_SLC_BLOCK
    forced = (blk == 0) | (blk > cur - NSA_LOCAL_BLOCKS)
    score = jnp.where(blk <= cur, jnp.where(forced, FORCE_SCORE, imp), NEG_INF)
    sel = _topk_rows(score, n_slc, min(NSA_SLC_TOPK, n_slc))
    qs_slc = [_extend_query(qs[r], sel) for r in heads]

    def slc_inputs(j, d):
        return [(_rows(ks_g, j), qs_slc[r], [] if d is None else [bias_ref[0, r, d]]) for r in heads]

    def win_inputs(j, d):
        return [(_rows(kw_g, j), qs[r], [bias_ref[0, r, d]]) for r in heads]

    n = len(heads)
    bufs = ((s0_ref, s1_ref), (p0_ref, p1_ref))
    o_slc = _attend(i, jnp.minimum(i, 1), 1, jnp.maximum(i - 1, 0), n, t, HEAD_DIM, slc_inputs,
                    lambda j: [_cols(vst_g, j)] * n, *bufs)
    n_win = NSA_WINDOW // t
    o_win = _attend(i, jnp.minimum(i, n_win), n_win, None, n, t, HEAD_DIM, win_inputs,
                    lambda j: [_cols(vwt_g, j)] * n, *bufs)

    gates = jax.nn.sigmoid(gate_ref[0])
    outs = []
    for r in heads:
        g_cmp, g_slc, g_win = (gates[3 * r + br:3 * r + br + 1, :] for br in range(3))
        outs.append(g_cmp * o_cmp[r] + g_slc * o_slc[r] + g_win * o_win[r])
    o_ref[0] = jnp.concatenate(outs, axis=0).T.astype(o_ref.dtype)


def _nsa(nq, k_cmp, v_cmp_t, ks, vs_t, kw, vw_t, gate_t, bias_tiles, overlap_t):
    b, _, s, dh = nq.shape
    g, r, t = NSA_KV_HEADS, NSA_GROUP, ATT_TILE
    assert NSA_WINDOW == 2 * t and s % t == 0 and t % NSA_SLC_BLOCK == 0
    k_spec = lambda k: pl.BlockSpec((1, 1, s, k.shape[3]), lambda bi, gi, i: (bi, gi, 0, 0))
    vt_spec = pl.BlockSpec((1, dh, s), lambda bi, gi, i: (bi, gi, 0))
    return pl.pallas_call(
        _nsa_kernel,
        grid=(b, g, s // t),
        in_specs=[
            pl.BlockSpec((1, r, t, dh), lambda bi, gi, i: (bi, gi, i, 0)),
            pl.BlockSpec((1, 1) + k_cmp.shape[2:], lambda bi, gi, i: (bi, gi, 0, 0)),
            pl.BlockSpec((1, 1) + v_cmp_t.shape[2:], lambda bi, gi, i: (bi, gi, 0, 0)),
            k_spec(ks), vt_spec, k_spec(kw), vt_spec,
            pl.BlockSpec((1, NSA_GATE_ROWS, t), lambda bi, gi, i: (bi, gi, i)),
            pl.BlockSpec((1,) + bias_tiles.shape[1:], lambda bi, gi, i: (gi, 0, 0, 0, 0)),
            pl.BlockSpec(overlap_t.shape, lambda bi, gi, i: (0, 0)),
        ],
        out_specs=pl.BlockSpec((1, t, r * dh), lambda bi, gi, i: (bi, i, gi)),
        out_shape=jax.ShapeDtypeStruct((b, s, g * r * dh), BF16),
        scratch_shapes=[pltpu.VMEM((r, t, t), F32), pltpu.VMEM((r, t, t), F32),
                        pltpu.VMEM((r, t, t), BF16), pltpu.VMEM((r, t, t), BF16)],
        compiler_params=_params("parallel", "parallel", "arbitrary"),
        name="nsa",
    )(nq, k_cmp, v_cmp_t, ks, vs_t, kw, vw_t, gate_t, bias_tiles, overlap_t)


def _outproj_kernel(x_ref, om_ref, od_ref, on_ref, w_ref, o_ref):
    wm = om_ref.shape[-1]
    wd = od_ref.shape[-1]
    y = jnp.dot(om_ref[0], w_ref[:wm], preferred_element_type=F32)
    y = y + jnp.dot(od_ref[0], w_ref[wm:wm + wd], preferred_element_type=F32)
    y = y + jnp.dot(on_ref[0], w_ref[wm + wd:], preferred_element_type=F32)
    o_ref[0] = x_ref[0] + y


def _outproj(x, o_m, o_d, o_n, w, *, ts=512):
    b, s, d = x.shape

    def spec(a):
        return pl.BlockSpec((1, ts, a.shape[-1]), lambda bi, i: (bi, i, 0))

    return pl.pallas_call(
        _outproj_kernel,
        grid=(b, s // ts),
        in_specs=[spec(x), spec(o_m), spec(o_d), spec(o_n), pl.BlockSpec(w.shape, lambda bi, i: (0, 0))],
        out_specs=spec(x),
        out_shape=jax.ShapeDtypeStruct((b, s, d), F32),
        compiler_params=_params("parallel", "parallel"),
        name="outproj",
    )(x, o_m, o_d, o_n, w)


def _final_norm_kernel(x_ref, g_ref, o_ref):
    o_ref[...] = _rms(x_ref[...], g_ref[...])


def _final_norm(x2, g, *, tm=1024):
    n, d = x2.shape
    return pl.pallas_call(
        _final_norm_kernel,
        grid=(n // tm,),
        in_specs=[pl.BlockSpec((tm, d), lambda i: (i, 0)), pl.BlockSpec((1, d), lambda i: (0, 0))],
        out_specs=pl.BlockSpec((tm, d), lambda i: (i, 0)),
        out_shape=jax.ShapeDtypeStruct((n, d), F32),
        compiler_params=_params("parallel"),
        name="final_norm",
    )(x2, g.reshape(1, d))


def _rel_bucket(dist):
    n = jnp.maximum(dist, 0)
    max_exact = REL_BUCKETS // 2
    n_f = jnp.maximum(n, max_exact).astype(F32)
    large = max_exact + (jnp.log(n_f / max_exact) / math.log(REL_MAX_DIST / max_exact)
                         * (REL_BUCKETS - max_exact)).astype(jnp.int32)
    return jnp.where(n < max_exact, n, jnp.minimum(large, REL_BUCKETS - 1))


def _bias_tiles(rel_bias):
    t = ATT_TILE
    assert t >= REL_MAX_DIST
    heads = rel_bias.shape[1]
    width = 2 * t + 1
    by_dist = rel_bias[_rel_bucket(jnp.arange(width))] - rel_bias[REL_BUCKETS - 1]
    skew = jnp.broadcast_to(by_dist.T[:, None, :], (heads, t, width)).reshape(heads, t * width)
    skew = skew[:, :t * (width - 1)].reshape(heads, t, width - 1)
    causal = jnp.arange(t)[:, None] <= jnp.arange(t)[None, :]
    own = jnp.where(causal, skew[:, :, :t], NEG_INF)
    return jnp.stack([own, skew[:, :, t:]], axis=1)


def _const_tables(s):
    n_moba = s // MOBA_BLOCK
    avg = np.zeros((2 * SUBLANES, s), np.float32)
    for j in range(n_moba):
        avg[j, j * MOBA_BLOCK:(j + 1) * MOBA_BLOCK] = 1.0 / MOBA_BLOCK
    n_cmp = (s - NSA_CMP_BLOCK) // NSA_CMP_STRIDE + 1
    n_slc = s // NSA_SLC_BLOCK
    cmp_start = np.arange(n_cmp) * NSA_CMP_STRIDE
    slc_start = np.arange(n_slc) * NSA_SLC_BLOCK
    ov = np.clip(np.minimum(cmp_start[:, None] + NSA_CMP_BLOCK, slc_start[None, :] + NSA_SLC_BLOCK)
                 - np.maximum(cmp_start[:, None], slc_start[None, :]), 0, None) / NSA_CMP_BLOCK
    overlap_t = np.zeros((n_slc, s // NSA_CMP_STRIDE), np.float32)
    overlap_t[:, :n_cmp] = ov.T
    t = ATT_TILE
    win_far = np.where(np.arange(t)[:, None] > np.arange(t)[None, :], 0.0, NEG_INF).astype(np.float32)
    member = np.zeros((len(_MEMBER_BLOCKS), s, HEAD_DIM), np.float32)
    for kind, block in enumerate(_MEMBER_BLOCKS):
        assert s // block <= HEAD_DIM
        member[kind, np.arange(s), np.arange(s) // block] = 1.0
    return jnp.asarray(avg, BF16), jnp.asarray(overlap_t, BF16), jnp.asarray(win_far), jnp.asarray(member, BF16)


def kernel(x, rel_bias, norm_ffn1, ffn1_gate, ffn1_up, ffn1_down, norm_mix, w_in, diff_lambda, diff_subln, nsa_cmp_pe, nsa_cmp_w1, nsa_cmp_w2, w_out, norm_ffn2, ffn2_gate, ffn2_up, ffn2_down, final_norm):
    b, s, d = x.shape
    depth = w_in.shape[0]
    assert d == D_MODEL
    h0, h1 = MOBA_HEADS, MOBA_HEADS + DIFF_HEADS
    near = _bias_tiles(rel_bias.astype(F32))
    avg, overlap_t, win_far, member = _const_tables(s)
    moba_bias = near[:h0]
    diff_bias = near[h0:h1] / (DIFF_QK_DIM ** -0.5)
    nsa_bias = jnp.concatenate([near[h1:], jnp.broadcast_to(win_far, near[h1:, :1].shape)], axis=1)
    nsa_bias = nsa_bias.reshape((NSA_KV_HEADS, NSA_GROUP) + nsa_bias.shape[1:])
    bf = lambda a: a.astype(BF16)

    for l in range(depth):
        lambda_init = 0.8 - 0.6 * math.exp(-0.3 * l)
        x = _ffn(x.reshape(b * s, d), norm_ffn1[l], bf(ffn1_gate[l]), bf(ffn1_up[l]), bf(ffn1_down[l])).reshape(b, s, d)

        p = _inproj(x, norm_mix[l], *_inproj_weights(w_in[l]), member)
        o_moba = _moba(p["mq"], p["mk"], p["mvT"], moba_bias, avg)
        o_diff = _diff(p["dq"], p["dk"], p["dvT"], diff_bias, diff_lambda[l].astype(F32), diff_subln[l].astype(F32),
                       lambda_init)
        pe4 = nsa_cmp_pe[l].astype(F32).reshape(4, NSA_CMP_STRIDE * HEAD_DIM)
        k_cmp, v_cmp_t = _compress(p["kc"], p["vc"], pe4, bf(nsa_cmp_w1[l]), bf(nsa_cmp_w2[l, 0]), bf(nsa_cmp_w2[l, 1].T))
        o_nsa = _nsa(p["nq"], k_cmp, v_cmp_t, p["ks"], p["vsT"], p["kw"], p["vwT"], p["gateT"], nsa_bias, overlap_t)
        x = _outproj(x, o_moba, o_diff, o_nsa, bf(w_out[l]))

        x = _ffn(x.reshape(b * s, d), norm_ffn2[l], bf(ffn2_gate[l]), bf(ffn2_up[l]), bf(ffn2_down[l])).reshape(b, s, d)
    return _final_norm(x.reshape(b * s, d), final_norm).reshape(b, s, d)
```

```python
import functools
import math

import numpy as np
import jax
import jax.numpy as jnp
from jax import lax
from jax.experimental import pallas as pl
from jax.experimental.pallas import tpu as pltpu

F32 = jnp.float32
BF16 = jnp.bfloat16

D_MODEL = 1024
HEAD_DIM = 64
MOBA_HEADS = 4
MOBA_BLOCK = 256
MOBA_TOPK = 3
DIFF_HEADS = 4
DIFF_QK_DIM = HEAD_DIM // 2
NSA_HEADS = 8
NSA_KV_HEADS = 2
NSA_GROUP = NSA_HEADS // NSA_KV_HEADS
NSA_CMP_BLOCK = 32
NSA_CMP_STRIDE = 16
NSA_CMP_HIDDEN = 256
NSA_SLC_BLOCK = 64
NSA_SLC_TOPK = 16
NSA_LOCAL_BLOCKS = 2
NSA_WINDOW = 512
REL_BUCKETS = 32
REL_MAX_DIST = 128
D_FF = 2816
RMS_EPS = 1e-6
NEG_INF = -1e30
FORCE_SCORE = 1e4

NSA_GATE_W = 3 * NSA_HEADS
NSA_GATE_ROWS = 16
LANES = 128
SUBLANES = 8
ATT_TILE = 256
VMEM_LIMIT = 48 * 1024 * 1024
LOG2E = math.log2(math.e)

NT_DIMS = (((1,), (1,)), ((), ()))


def _rms(x, g):
    return x * lax.rsqrt(jnp.mean(x * x, axis=-1, keepdims=True) + RMS_EPS) * g


def _params(*sem):
    return pltpu.CompilerParams(dimension_semantics=sem, vmem_limit_bytes=VMEM_LIMIT)


def _ffn_kernel(x_ref, g_ref, wg_ref, wu_ref, wd_ref, o_ref, xn_ref, acc_ref):
    f = pl.program_id(1)

    @pl.when(f == 0)
    def _():
        xn_ref[...] = _rms(x_ref[...], g_ref[...]).astype(BF16)
        acc_ref[...] = jnp.zeros_like(acc_ref)

    xn = xn_ref[...]
    gate = jnp.dot(xn, wg_ref[...], preferred_element_type=F32)
    up = jnp.dot(xn, wu_ref[...], preferred_element_type=F32)
    h = (gate * jax.nn.sigmoid(gate) * up).astype(BF16)
    acc_ref[...] += jnp.dot(h, wd_ref[...], preferred_element_type=F32)

    @pl.when(f == pl.num_programs(1) - 1)
    def _():
        o_ref[...] = x_ref[...] + 0.5 * acc_ref[...]


def _ffn(x2, g, wg, wu, wd, *, tm=1024, tf=256):
    n, d = x2.shape
    ff = wg.shape[1]
    return pl.pallas_call(
        _ffn_kernel,
        grid=(n // tm, ff // tf),
        in_specs=[
            pl.BlockSpec((tm, d), lambda i, f: (i, 0)),
            pl.BlockSpec((1, d), lambda i, f: (0, 0)),
            pl.BlockSpec((d, tf), lambda i, f: (0, f)),
            pl.BlockSpec((d, tf), lambda i, f: (0, f)),
            pl.BlockSpec((tf, d), lambda i, f: (f, 0)),
        ],
        out_specs=pl.BlockSpec((tm, d), lambda i, f: (i, 0)),
        out_shape=jax.ShapeDtypeStruct((n, d), F32),
        scratch_shapes=[pltpu.VMEM((tm, d), BF16), pltpu.VMEM((tm, d), F32)],
        compiler_params=_params("parallel", "arbitrary"),
        name="ffn",
    )(x2, g.reshape(1, d), wg, wu, wd)


_ROW_OUTS = (
    ("mq", 0, 4, BF16, HEAD_DIM ** -0.5, None),
    ("mk", 256, 4, BF16, 1.0, 0),
    ("dq", 768, 4, BF16, 1.0, None),
    ("dk", 1024, 4, BF16, 1.0, None),
    ("nq", 1536, 8, BF16, HEAD_DIM ** -0.5, None),
    ("kc", 2048, 2, F32, 1.0, None),
    ("vc", 2176, 2, F32, 1.0, None),
    ("ks", 2304, 2, BF16, 1.0, 1),
    ("kw", 2560, 2, BF16, 1.0, None),
)
_MEMBER_BLOCKS = (MOBA_BLOCK, NSA_SLC_BLOCK)
_COL_OUTS = (
    ("mvT", 512, 256, BF16),
    ("dvT", 1280, 256, BF16),
    ("vsT", 2432, 128, BF16),
    ("vwT", 2688, 128, BF16),
    ("gateT", None, NSA_KV_HEADS * NSA_GATE_ROWS, F32),
)


def _inproj_kernel(x_ref, g_ref, w_ref, wt_ref, member_ref, *out_refs):
    xn = _rms(x_ref[0], g_ref[...]).astype(BF16)
    col = 0
    for (_, _, heads, dtype, scale, member), o_ref in zip(_ROW_OUTS, out_refs):
        width = heads * HEAD_DIM
        p = jnp.dot(xn, w_ref[:, col:col + width], preferred_element_type=F32)
        if scale != 1.0:
            p = p * scale
        for h in range(heads):
            head = p[:, h * HEAD_DIM:(h + 1) * HEAD_DIM].astype(dtype)
            o_ref[0, h] = head if member is None else jnp.concatenate([head, member_ref[member]], axis=1)
        col += width
    pt = lax.dot_general(wt_ref[...], xn, NT_DIMS, preferred_element_type=F32)
    row = 0
    for (_, _, rows, dtype), o_ref in zip(_COL_OUTS, out_refs[len(_ROW_OUTS):]):
        o_ref[0] = pt[row:row + rows].astype(dtype)
        row += rows


def _inproj_weights(w_l):
    w_rows = jnp.concatenate([w_l[:, c:c + h * HEAD_DIM] for _, c, h, _, _, _ in _ROW_OUTS], axis=1)
    gate_cols = w_l[:, w_l.shape[1] - NSA_GATE_W:]
    per_group = 3 * NSA_GROUP
    gate_t = jnp.zeros((NSA_KV_HEADS * NSA_GATE_ROWS, w_l.shape[0]), w_l.dtype)
    for g in range(NSA_KV_HEADS):
        gate_t = gate_t.at[g * NSA_GATE_ROWS:g * NSA_GATE_ROWS + per_group].set(
            gate_cols[:, g * per_group:(g + 1) * per_group].T)
    w_cols = jnp.concatenate([w_l[:, c:c + r].T for _, c, r, _ in _COL_OUTS[:-1]] + [gate_t], axis=0)
    return w_rows.astype(BF16), w_cols.astype(BF16)


def _inproj(x, g, w_rows, w_cols, member, *, ts=512):
    b, s, d = x.shape
    widths = [HEAD_DIM if member is None else 2 * HEAD_DIM for _, _, _, _, _, member in _ROW_OUTS]
    out_shape = [jax.ShapeDtypeStruct((b, o[2], s, w), o[3]) for o, w in zip(_ROW_OUTS, widths)]
    out_specs = [pl.BlockSpec((1, o[2], ts, w), lambda bi, i: (bi, 0, i, 0)) for o, w in zip(_ROW_OUTS, widths)]
    out_shape += [jax.ShapeDtypeStruct((b, rows, s), dtype) for _, _, rows, dtype in _COL_OUTS]
    out_specs += [pl.BlockSpec((1, rows, ts), lambda bi, i: (bi, 0, i)) for _, _, rows, _ in _COL_OUTS]
    outs = pl.pallas_call(
        _inproj_kernel,
        grid=(b, s // ts),
        in_specs=[
            pl.BlockSpec((1, ts, d), lambda bi, i: (bi, i, 0)),
            pl.BlockSpec((1, d), lambda bi, i: (0, 0)),
            pl.BlockSpec(w_rows.shape, lambda bi, i: (0, 0)),
            pl.BlockSpec(w_cols.shape, lambda bi, i: (0, 0)),
            pl.BlockSpec((member.shape[0], ts, HEAD_DIM), lambda bi, i: (0, i, 0)),
        ],
        out_specs=out_specs,
        out_shape=out_shape,
        compiler_params=_params("parallel", "parallel"),
        name="inproj",
    )(x, g.reshape(1, d), w_rows, w_cols, member)
    names = [o[0] for o in _ROW_OUTS] + [o[0] for o in _COL_OUTS]
    return dict(zip(names, outs))


ONES_ROWS = 16
EXP_ROWS = 32
_ADD_TILE = {"own": 0, "near": 1, "edge": 2, "far": None}


def _tile_schedule(n_qt, window=None):
    tiles = [("own", i, i) for i in range(n_qt)] + [("near", i, i - 1) for i in range(1, n_qt)]
    if window is None:
        tiles += [("far", i, j) for j in range(n_qt - 2) for i in range(j + 2, n_qt)]
    else:
        assert window == 2
        tiles += [("edge", i, i - window) for i in range(window, n_qt)]
    kinds = tuple(kind for kind, _, _ in tiles)
    return kinds, jnp.asarray(np.array([[i for _, i, _ in tiles], [j for _, _, j in tiles]], np.int32))


def _attend_tiles(kinds, tab_ref, score_inputs, value_tiles, m_ref, acc_ref, s_refs, p_refs, c=LOG2E):
    tq = m_ref.shape[-1]

    def scores(t, kind, slot):
        qi, kj = tab_ref[0, t], tab_ref[1, t]
        rows = pl.ds(pl.multiple_of(qi * SUBLANES, SUBLANES), SUBLANES)
        pend = []
        for ci, (k, q, adds) in enumerate(score_inputs(qi, kj, kind)):
            s = lax.dot_general(k, q, NT_DIMS, preferred_element_type=F32)
            for a in adds:
                s = s + a
            s_refs[slot][ci] = s
            m_old = m_ref[ci, rows, :][0:1, :]
            m_new = jnp.maximum(m_old, jnp.max(s, axis=0, keepdims=True))
            m_ref[ci, rows, :] = jnp.broadcast_to(m_new, (SUBLANES, tq))
            pend.append((m_new, jnp.exp2((m_old - m_new) * c)))
        return qi, kj, tuple(pend)

    def accumulate(pending, slot):
        qi, kj, pend = pending
        s_ref, p_ref = s_refs[slot], p_refs[slot]
        tk = s_ref.shape[1]
        for ci, (v_t, (m_new, alpha)) in enumerate(zip(value_tiles(kj), pend)):
            for r in range(tk // EXP_ROWS):
                rows = slice(r * EXP_ROWS, (r + 1) * EXP_ROWS)
                p_ref[ci, rows, :] = jnp.exp2(((s_ref[ci, rows, :] - m_new) * c).astype(BF16))
            v_ext = jnp.concatenate([v_t, jnp.ones((ONES_ROWS, tk), BF16)], axis=0)
            acc_ref[ci, qi] = alpha * acc_ref[ci, qi] + jnp.dot(v_ext, p_ref[ci], preferred_element_type=F32)

    m_ref[...] = jnp.full(m_ref.shape, NEG_INF, F32)
    acc_ref[...] = jnp.zeros(acc_ref.shape, F32)
    n = len(kinds)
    pending = scores(0, kinds[0], 0)
    step = 0
    while step < n - 1:
        kind = kinds[step + 1]
        run = 1
        while step + run < n - 1 and kinds[step + run + 1] == kind:
            run += 1

        def pair(u, pending, step=step, kind=kind):
            t = step + 2 * u
            nxt = scores(t + 1, kind, (step + 1) % 2)
            accumulate(pending, step % 2)
            nxt2 = scores(t + 2, kind, step % 2)
            accumulate(nxt, (step + 1) % 2)
            return nxt2

        if run // 2:
            pending = lax.fori_loop(0, run // 2, pair, pending)
            step += 2 * (run // 2)
        if run % 2:
            nxt = scores(step + 1, kind, (step + 1) % 2)
            accumulate(pending, step % 2)
            pending = nxt
            step += 1
    accumulate(pending, (n - 1) % 2)


def _rows(ref, j, t=ATT_TILE):
    return ref[pl.ds(pl.multiple_of(j * t, t), t), :]


def _cols(ref, j, t=ATT_TILE):
    return ref[:, pl.ds(pl.multiple_of(j * t, t), t)]


def _normalized(acc, dv=HEAD_DIM):
    return acc[:dv] / acc[dv:dv + 1]


def _topk_rows(score, n_rows, topk):
    row_id = lax.broadcasted_iota(jnp.int32, score.shape, 0)
    rank = jnp.zeros(score.shape, F32)
    for c in range(n_rows):
        other = score[c:c + 1, :]
        beats = (other > score) | ((other == score) & (row_id > c))
        rank = rank + jnp.where(beats, 1.0, 0.0)
    return (rank < topk) & (row_id < n_rows)


def _extend_query(q, chosen):
    tq, dh = q.shape
    pen = jnp.where(chosen, 0.0, NEG_INF)
    pen = jnp.concatenate([pen, jnp.zeros((LANES - pen.shape[0], tq), F32)], axis=0)
    return jnp.concatenate([q, pen.T[:, :dh].astype(q.dtype)], axis=1)


def _attention_scratch(n, n_qt, t, dv=HEAD_DIM):
    return [pltpu.VMEM((n, n_qt * SUBLANES, t), F32), pltpu.VMEM((n, n_qt, dv + ONES_ROWS, t), F32),
            pltpu.VMEM((n, t, t), F32), pltpu.VMEM((n, t, t), F32),
            pltpu.VMEM((n, t, t), BF16), pltpu.VMEM((n, t, t), BF16)]


SMEM_SPEC = pl.BlockSpec(memory_space=pltpu.SMEM)


def _moba_kernel(tab_ref, q_ref, k_ref, vt_ref, bias_ref, avg_ref, o_ref, qx_ref, m_ref, acc_ref,
                 s0_ref, s1_ref, p0_ref, p1_ref, *, kinds):
    t = ATT_TILE
    s = q_ref.shape[2]
    n_blk = s // MOBA_BLOCK
    heads = range(MOBA_HEADS)
    blk_rows = -(-n_blk // SUBLANES) * SUBLANES
    blk = lax.broadcasted_iota(jnp.int32, (blk_rows, s), 0)
    own = lax.broadcasted_iota(jnp.int32, (blk_rows, s), 1) // MOBA_BLOCK

    for h in heads:
        k_mean = jnp.dot(avg_ref[...], k_ref[0, h], preferred_element_type=F32)[:, :HEAD_DIM]
        km_hi = k_mean.astype(BF16)
        km_lo = (k_mean - km_hi.astype(F32)).astype(BF16)
        q = q_ref[0, h]
        gate = (lax.dot_general(km_hi, q, NT_DIMS, preferred_element_type=F32)
                + lax.dot_general(km_lo, q, NT_DIMS, preferred_element_type=F32))
        gate = jnp.where(blk < own, gate[:blk_rows], NEG_INF)
        chosen = (_topk_rows(gate, n_blk, MOBA_TOPK) & (blk < own)) | (blk == own)
        for qi in range(s // t):
            cols = slice(qi * t, (qi + 1) * t)
            qx_ref[h, cols, :] = _extend_query(q_ref[0, h, cols, :], chosen[:, cols])

    def score_inputs(qi, kj, kind):
        tile = _ADD_TILE[kind]
        return [(_rows(k_ref.at[0, h], kj), _rows(qx_ref.at[h], qi), [] if tile is None else [bias_ref[h, tile]])
                for h in heads]

    def value_tiles(kj):
        return [_cols(vt_ref.at[0, h * HEAD_DIM:(h + 1) * HEAD_DIM], kj) for h in heads]

    _attend_tiles(kinds, tab_ref, score_inputs, value_tiles, m_ref, acc_ref, (s0_ref, s1_ref), (p0_ref, p1_ref))

    def finish(qi, carry):
        o_t = jnp.concatenate([_normalized(acc_ref[h, qi]) for h in heads], axis=0)
        o_ref[0, pl.ds(pl.multiple_of(qi * t, t), t), :] = o_t.T.astype(o_ref.dtype)
        return carry

    lax.fori_loop(0, s // t, finish, 0)


def _moba(mq, mk, mv_t, bias_near, avg):
    b, h, s, dh = mq.shape
    t = ATT_TILE
    assert MOBA_BLOCK == t and s % t == 0 and s // t <= avg.shape[0]
    kinds, table = _tile_schedule(s // t)
    return pl.pallas_call(
        functools.partial(_moba_kernel, kinds=kinds),
        grid=(b,),
        in_specs=[
            SMEM_SPEC,
            pl.BlockSpec((1, h, s, dh), lambda bi: (bi, 0, 0, 0)),
            pl.BlockSpec((1, h, s, mk.shape[3]), lambda bi: (bi, 0, 0, 0)),
            pl.BlockSpec((1, h * dh, s), lambda bi: (bi, 0, 0)),
            pl.BlockSpec(bias_near.shape, lambda bi: (0, 0, 0, 0)),
            pl.BlockSpec(avg.shape, lambda bi: (0, 0)),
        ],
        out_specs=pl.BlockSpec((1, s, h * dh), lambda bi: (bi, 0, 0)),
        out_shape=jax.ShapeDtypeStruct((b, s, h * dh), BF16),
        scratch_shapes=[pltpu.VMEM((h, s, mk.shape[3]), BF16)] + _attention_scratch(h, s // t, t),
        compiler_params=_params("parallel"),
        name="moba",
    )(table, mq, mk, mv_t, bias_near, avg)


def _diff_kernel(tab_ref, q_ref, k_ref, vt_ref, bias_ref, lam_ref, subln_ref, o_ref, m_ref, acc_ref,
                 s0_ref, s1_ref, p0_ref, p1_ref, *, kinds, lambda_init):
    t = ATT_TILE
    heads = range(DIFF_HEADS)
    lane = lax.broadcasted_iota(jnp.int32, (t, HEAD_DIM), 1)
    c = DIFF_QK_DIM ** -0.5 * LOG2E
    lp = lam_ref[...]
    lam = (jnp.exp(jnp.sum(lp[0:1] * lp[1:2], axis=-1, keepdims=True))
           - jnp.exp(jnp.sum(lp[2:3] * lp[3:4], axis=-1, keepdims=True)) + lambda_init)

    def score_inputs(qi, kj, kind):
        tile = _ADD_TILE[kind]
        out = []
        for h in heads:
            k = _rows(k_ref.at[0, h], kj)
            q = _rows(q_ref.at[0, h], qi)
            adds = [] if tile is None else [bias_ref[h, tile]]
            out.append((k, jnp.where(lane < DIFF_QK_DIM, q, jnp.zeros_like(q)), adds))
            out.append((k, jnp.where(lane >= DIFF_QK_DIM, q, jnp.zeros_like(q)), adds))
        return out

    def value_tiles(kj):
        tiles = [_cols(vt_ref.at[0, h * HEAD_DIM:(h + 1) * HEAD_DIM], kj) for h in heads]
        return [tiles[h] for h in heads for _ in range(2)]

    _attend_tiles(kinds, tab_ref, score_inputs, value_tiles, m_ref, acc_ref, (s0_ref, s1_ref), (p0_ref, p1_ref), c=c)

    def finish(qi, carry):
        outs = []
        for h in heads:
            o = _normalized(acc_ref[2 * h, qi]) - lam * _normalized(acc_ref[2 * h + 1, qi])
            o = o * lax.rsqrt(jnp.mean(o * o, axis=0, keepdims=True) + RMS_EPS) * subln_ref[...]
            outs.append(o * (1.0 - lambda_init))
        o_ref[0, pl.ds(pl.multiple_of(qi * t, t), t), :] = jnp.concatenate(outs, axis=0).T.astype(o_ref.dtype)
        return carry

    lax.fori_loop(0, q_ref.shape[2] // t, finish, 0)


def _diff(dq, dk, dv_t, bias_near, lam_params, subln_g, lambda_init):
    b, h, s, dh = dq.shape
    t = ATT_TILE
    kinds, table = _tile_schedule(s // t)
    return pl.pallas_call(
        functools.partial(_diff_kernel, kinds=kinds, lambda_init=lambda_init),
        grid=(b,),
        in_specs=[
            SMEM_SPEC,
            pl.BlockSpec((1, h, s, dh), lambda bi: (bi, 0, 0, 0)),
            pl.BlockSpec((1, h, s, dh), lambda bi: (bi, 0, 0, 0)),
            pl.BlockSpec((1, h * dh, s), lambda bi: (bi, 0, 0)),
            pl.BlockSpec(bias_near.shape, lambda bi: (0, 0, 0, 0)),
            pl.BlockSpec(lam_params.shape, lambda bi: (0, 0)),
            pl.BlockSpec((dh, 1), lambda bi: (0, 0)),
        ],
        out_specs=pl.BlockSpec((1, s, h * dh), lambda bi: (bi, 0, 0)),
        out_shape=jax.ShapeDtypeStruct((b, s, h * dh), BF16),
        scratch_shapes=_attention_scratch(2 * h, s // t, t),
        compiler_params=_params("parallel"),
        name="diff",
    )(table, dq, dk, dv_t, bias_near, lam_params, subln_g.reshape(dh, 1))


def _compress_kernel(kc_ref, vc_ref, pe_ref, w1_ref, w2k_ref, w2vt_ref, ko_ref, vo_ref):
    half = NSA_CMP_STRIDE * HEAD_DIM

    def hidden(t, c_ref):
        c = c_ref[0, 0]
        top = (c + pe_ref[2 * t:2 * t + 1]).astype(BF16)
        bot = (c + pe_ref[2 * t + 1:2 * t + 2]).astype(BF16)
        a = jnp.dot(top, w1_ref[t, :half], preferred_element_type=F32)
        bm = jnp.dot(bot, w1_ref[t, half:], preferred_element_type=F32)
        hid = a + pltpu.roll(bm, bm.shape[0] - 1, 0)
        return jax.nn.gelu(hid).astype(BF16)

    ko_ref[0, 0] = jnp.dot(hidden(0, kc_ref), w2k_ref[...], preferred_element_type=F32)
    vo_ref[0, 0] = lax.dot_general(w2vt_ref[...], hidden(1, vc_ref), NT_DIMS, preferred_element_type=F32)


def _compress(kc, vc, pe4, w1, w2k, w2v_t):
    b, g, s, dh = kc.shape
    n_chunk = s // NSA_CMP_STRIDE
    width = NSA_CMP_STRIDE * dh
    kc = kc.reshape(b, g, n_chunk, width)
    vc = vc.reshape(b, g, n_chunk, width)
    spec_in = pl.BlockSpec((1, 1, n_chunk, width), lambda bi, gi: (bi, gi, 0, 0))
    return pl.pallas_call(
        _compress_kernel,
        grid=(b, g),
        in_specs=[
            spec_in, spec_in,
            pl.BlockSpec(pe4.shape, lambda bi, gi: (0, 0)),
            pl.BlockSpec(w1.shape, lambda bi, gi: (0, 0, 0)),
            pl.BlockSpec(w2k.shape, lambda bi, gi: (0, 0)),
            pl.BlockSpec(w2v_t.shape, lambda bi, gi: (0, 0)),
        ],
        out_specs=[pl.BlockSpec((1, 1, n_chunk, dh), lambda bi, gi: (bi, gi, 0, 0)),
                   pl.BlockSpec((1, 1, dh, n_chunk), lambda bi, gi: (bi, gi, 0, 0))],
        out_shape=[jax.ShapeDtypeStruct((b, g, n_chunk, dh), F32), jax.ShapeDtypeStruct((b, g, dh, n_chunk), F32)],
        compiler_params=_params("parallel", "parallel"),
        name="nsa_compress",
    )(kc, vc, pe4, w1, w2k, w2v_t)


def _nsa_kernel(slc_tab_ref, win_tab_ref, q_ref, kcmp_ref, vcmpt_ref, ks_ref, vst_ref, kw_ref, vwt_ref, gate_ref,
                bias_ref, overlap_ref, o_ref, qx_ref, ocmp_ref, m_ref, acc_ref, m2_ref, acc2_ref,
                s0_ref, s1_ref, p0_ref, p1_ref, *, slc_kinds, win_kinds):
    t = ATT_TILE
    heads = range(NSA_GROUP)
    s = q_ref.shape[2]
    n_chunk = kcmp_ref.shape[2]
    n_slc = s // NSA_SLC_BLOCK
    ks_g, vst_g = ks_ref.at[0, 0], vst_ref.at[0]
    kw_g, vwt_g = kw_ref.at[0, 0], vwt_ref.at[0]
    k_cmp = kcmp_ref[0, 0].astype(BF16)
    v_cmp_t = vcmpt_ref[0, 0].astype(BF16)
    cmp_id = lax.broadcasted_iota(jnp.int32, (n_chunk, t), 0)
    blk = lax.broadcasted_iota(jnp.int32, (n_slc, t), 0)

    def select(qi, carry):
        qs = [_rows(q_ref.at[0, r], qi) for r in heads]
        q_pos = qi * t + lax.broadcasted_iota(jnp.int32, (n_chunk, t), 1)
        cmp_valid = (cmp_id * NSA_CMP_STRIDE + (NSA_CMP_BLOCK - 1) <= q_pos) & (cmp_id < n_chunk - 1)
        p_sum = jnp.zeros((n_chunk, t), F32)
        for r in heads:
            sc = lax.dot_general(k_cmp, qs[r], NT_DIMS, preferred_element_type=F32)
            m = jnp.max(jnp.where(cmp_valid, sc, NEG_INF), axis=0, keepdims=True)
            e = jnp.where(cmp_valid, jnp.exp(sc - m), 0.0)
            l = jnp.sum(e, axis=0, keepdims=True)
            p = e / jnp.where(l > 0.0, l, 1.0)
            p_sum = p_sum + p
            ocmp_ref[r, qi] = jnp.dot(v_cmp_t, p.astype(BF16), preferred_element_type=F32)
        ps_hi = p_sum.astype(BF16)
        ps_lo = (p_sum - ps_hi.astype(F32)).astype(BF16)
        imp = (jnp.dot(overlap_ref[...], ps_hi, preferred_element_type=F32)
               + jnp.dot(overlap_ref[...], ps_lo, preferred_element_type=F32))
        cur = (qi * t + lax.broadcasted_iota(jnp.int32, (n_slc, t), 1)) // NSA_SLC_BLOCK
        forced = (blk == 0) | (blk > cur - NSA_LOCAL_BLOCKS)
        score = jnp.where(blk <= cur, jnp.where(forced, FORCE_SCORE, imp), NEG_INF)
        sel = _topk_rows(score, n_slc, min(NSA_SLC_TOPK, n_slc))
        rows = pl.ds(pl.multiple_of(qi * t, t), t)
        for r in heads:
            qx_ref[r, rows, :] = _extend_query(qs[r], sel)
        return carry

    n_qt = s // t
    lax.fori_loop(0, n_qt // 2, lambda u, carry: select(2 * u + 1, select(2 * u, carry)), 0)
    if n_qt % 2:
        select(n_qt - 1, 0)

    def adds(r, kind):
        return [] if _ADD_TILE[kind] is None else [bias_ref[0, r, _ADD_TILE[kind]]]

    def slc_inputs(qi, kj, kind):
        return [(_rows(ks_g, kj), _rows(qx_ref.at[r], qi), adds(r, kind)) for r in heads]

    def win_inputs(qi, kj, kind):
        return [(_rows(kw_g, kj), _rows(q_ref.at[0, r], qi), adds(r, kind)) for r in heads]

    bufs = ((s0_ref, s1_ref), (p0_ref, p1_ref))
    _attend_tiles(slc_kinds, slc_tab_ref, slc_inputs, lambda kj: [_cols(vst_g, kj)] * len(heads), m_ref, acc_ref, *bufs)
    _attend_tiles(win_kinds, win_tab_ref, win_inputs, lambda kj: [_cols(vwt_g, kj)] * len(heads), m2_ref, acc2_ref, *bufs)

    def finish(qi, carry):
        cols = pl.ds(pl.multiple_of(qi * t, t), t)
        gates = jax.nn.sigmoid(gate_ref[0, :, cols])
        outs = []
        for r in heads:
            g_cmp, g_slc, g_win = (gates[3 * r + br:3 * r + br + 1, :] for br in range(3))
            outs.append(g_cmp * ocmp_ref[r, qi] + g_slc * _normalized(acc_ref[r, qi]) + g_win * _normalized(acc2_ref[r, qi]))
        o_ref[0, cols, :] = jnp.concatenate(outs, axis=0).T.astype(o_ref.dtype)
        return carry

    lax.fori_loop(0, s // t, finish, 0)


def _nsa(nq, k_cmp, v_cmp_t, ks, vs_t, kw, vw_t, gate_t, bias_tiles, overlap_t):
    b, _, s, dh = nq.shape
    g, r, t = NSA_KV_HEADS, NSA_GROUP, ATT_TILE
    assert NSA_WINDOW == 2 * t and s % t == 0 and t % NSA_SLC_BLOCK == 0
    slc_kinds, slc_table = _tile_schedule(s // t)
    win_kinds, win_table = _tile_schedule(s // t, window=NSA_WINDOW // t)
    k_spec = lambda k: pl.BlockSpec((1, 1, s, k.shape[3]), lambda bi, gi: (bi, gi, 0, 0))
    vt_spec = pl.BlockSpec((1, dh, s), lambda bi, gi: (bi, gi, 0))
    scratch = _attention_scratch(r, s // t, t)
    return pl.pallas_call(
        functools.partial(_nsa_kernel, slc_kinds=slc_kinds, win_kinds=win_kinds),
        grid=(b, g),
        in_specs=[
            SMEM_SPEC, SMEM_SPEC,
            pl.BlockSpec((1, r, s, dh), lambda bi, gi: (bi, gi, 0, 0)),
            pl.BlockSpec((1, 1) + k_cmp.shape[2:], lambda bi, gi: (bi, gi, 0, 0)),
            pl.BlockSpec((1, 1) + v_cmp_t.shape[2:], lambda bi, gi: (bi, gi, 0, 0)),
            k_spec(ks), vt_spec, k_spec(kw), vt_spec,
            pl.BlockSpec((1, NSA_GATE_ROWS, s), lambda bi, gi: (bi, gi, 0)),
            pl.BlockSpec((1,) + bias_tiles.shape[1:], lambda bi, gi: (gi, 0, 0, 0, 0)),
            pl.BlockSpec(overlap_t.shape, lambda bi, gi: (0, 0)),
        ],
        out_specs=pl.BlockSpec((1, s, r * dh), lambda bi, gi: (bi, 0, gi)),
        out_shape=jax.ShapeDtypeStruct((b, s, g * r * dh), BF16),
        scratch_shapes=[pltpu.VMEM((r, s, ks.shape[3]), BF16), pltpu.VMEM((r, s // t, dh, t), F32)]
                       + scratch[:2] + scratch,
        compiler_params=_params("parallel", "parallel"),
        name="nsa",
    )(slc_table, win_table, nq, k_cmp, v_cmp_t, ks, vs_t, kw, vw_t, gate_t, bias_tiles, overlap_t)


def _outproj_kernel(x_ref, om_ref, od_ref, on_ref, w_ref, o_ref):
    wm = om_ref.shape[-1]
    wd = od_ref.shape[-1]
    y = jnp.dot(om_ref[0], w_ref[:wm], preferred_element_type=F32)
    y = y + jnp.dot(od_ref[0], w_ref[wm:wm + wd], preferred_element_type=F32)
    y = y + jnp.dot(on_ref[0], w_ref[wm + wd:], preferred_element_type=F32)
    o_ref[0] = x_ref[0] + y


def _outproj(x, o_m, o_d, o_n, w, *, ts=512):
    b, s, d = x.shape

    def spec(a):
        return pl.BlockSpec((1, ts, a.shape[-1]), lambda bi, i: (bi, i, 0))

    return pl.pallas_call(
        _outproj_kernel,
        grid=(b, s // ts),
        in_specs=[spec(x), spec(o_m), spec(o_d), spec(o_n), pl.BlockSpec(w.shape, lambda bi, i: (0, 0))],
        out_specs=spec(x),
        out_shape=jax.ShapeDtypeStruct((b, s, d), F32),
        compiler_params=_params("parallel", "parallel"),
        name="outproj",
    )(x, o_m, o_d, o_n, w)


def _final_norm_kernel(x_ref, g_ref, o_ref):
    o_ref[...] = _rms(x_ref[...], g_ref[...])


def _final_norm(x2, g, *, tm=1024):
    n, d = x2.shape
    return pl.pallas_call(
        _final_norm_kernel,
        grid=(n // tm,),
        in_specs=[pl.BlockSpec((tm, d), lambda i: (i, 0)), pl.BlockSpec((1, d), lambda i: (0, 0))],
        out_specs=pl.BlockSpec((tm, d), lambda i: (i, 0)),
        out_shape=jax.ShapeDtypeStruct((n, d), F32),
        compiler_params=_params("parallel"),
        name="final_norm",
    )(x2, g.reshape(1, d))


def _rel_bucket(dist):
    n = jnp.maximum(dist, 0)
    max_exact = REL_BUCKETS // 2
    n_f = jnp.maximum(n, max_exact).astype(F32)
    large = max_exact + (jnp.log(n_f / max_exact) / math.log(REL_MAX_DIST / max_exact)
                         * (REL_BUCKETS - max_exact)).astype(jnp.int32)
    return jnp.where(n < max_exact, n, jnp.minimum(large, REL_BUCKETS - 1))


def _bias_tiles(rel_bias):
    t = ATT_TILE
    assert t >= REL_MAX_DIST
    heads = rel_bias.shape[1]
    width = 2 * t + 1
    by_dist = rel_bias[_rel_bucket(jnp.arange(width))] - rel_bias[REL_BUCKETS - 1]
    skew = jnp.broadcast_to(by_dist.T[:, None, :], (heads, t, width)).reshape(heads, t * width)
    skew = skew[:, :t * (width - 1)].reshape(heads, t, width - 1)
    causal = jnp.arange(t)[:, None] <= jnp.arange(t)[None, :]
    own = jnp.where(causal, skew[:, :, :t], NEG_INF)
    return jnp.stack([own, skew[:, :, t:]], axis=1)


def _const_tables(s):
    n_moba = s // MOBA_BLOCK
    avg = np.zeros((2 * SUBLANES, s), np.float32)
    for j in range(n_moba):
        avg[j, j * MOBA_BLOCK:(j + 1) * MOBA_BLOCK] = 1.0 / MOBA_BLOCK
    n_cmp = (s - NSA_CMP_BLOCK) // NSA_CMP_STRIDE + 1
    n_slc = s // NSA_SLC_BLOCK
    cmp_start = np.arange(n_cmp) * NSA_CMP_STRIDE
    slc_start = np.arange(n_slc) * NSA_SLC_BLOCK
    ov = np.clip(np.minimum(cmp_start[:, None] + NSA_CMP_BLOCK, slc_start[None, :] + NSA_SLC_BLOCK)
                 - np.maximum(cmp_start[:, None], slc_start[None, :]), 0, None) / NSA_CMP_BLOCK
    overlap_t = np.zeros((n_slc, s // NSA_CMP_STRIDE), np.float32)
    overlap_t[:, :n_cmp] = ov.T
    t = ATT_TILE
    win_far = np.where(np.arange(t)[:, None] > np.arange(t)[None, :], 0.0, NEG_INF).astype(np.float32)
    member = np.zeros((len(_MEMBER_BLOCKS), s, HEAD_DIM), np.float32)
    for kind, block in enumerate(_MEMBER_BLOCKS):
        assert s // block <= HEAD_DIM
        member[kind, np.arange(s), np.arange(s) // block] = 1.0
    return jnp.asarray(avg, BF16), jnp.asarray(overlap_t, BF16), jnp.asarray(win_far), jnp.asarray(member, BF16)


def kernel(x, rel_bias, norm_ffn1, ffn1_gate, ffn1_up, ffn1_down, norm_mix, w_in, diff_lambda, diff_subln, nsa_cmp_pe, nsa_cmp_w1, nsa_cmp_w2, w_out, norm_ffn2, ffn2_gate, ffn2_up, ffn2_down, final_norm):
    b, s, d = x.shape
    depth = w_in.shape[0]
    assert d == D_MODEL
    h0, h1 = MOBA_HEADS, MOBA_HEADS + DIFF_HEADS
    near = _bias_tiles(rel_bias.astype(F32))
    avg, overlap_t, win_far, member = _const_tables(s)
    moba_bias = near[:h0]
    diff_bias = near[h0:h1] / (DIFF_QK_DIM ** -0.5)
    nsa_bias = jnp.concatenate([near[h1:], jnp.broadcast_to(win_far, near[h1:, :1].shape)], axis=1)
    nsa_bias = nsa_bias.reshape((NSA_KV_HEADS, NSA_GROUP) + nsa_bias.shape[1:])
    bf = lambda a: a.astype(BF16)

    for l in range(depth):
        lambda_init = 0.8 - 0.6 * math.exp(-0.3 * l)
        x = _ffn(x.reshape(b * s, d), norm_ffn1[l], bf(ffn1_gate[l]), bf(ffn1_up[l]), bf(ffn1_down[l])).reshape(b, s, d)

        p = _inproj(x, norm_mix[l], *_inproj_weights(w_in[l]), member)
        o_moba = _moba(p["mq"], p["mk"], p["mvT"], moba_bias, avg)
        o_diff = _diff(p["dq"], p["dk"], p["dvT"], diff_bias, diff_lambda[l].astype(F32), diff_subln[l].astype(F32),
                       lambda_init)
        pe4 = nsa_cmp_pe[l].astype(F32).reshape(4, NSA_CMP_STRIDE * HEAD_DIM)
        k_cmp, v_cmp_t = _compress(p["kc"], p["vc"], pe4, bf(nsa_cmp_w1[l]), bf(nsa_cmp_w2[l, 0]), bf(nsa_cmp_w2[l, 1].T))
        o_nsa = _nsa(p["nq"], k_cmp, v_cmp_t, p["ks"], p["vsT"], p["kw"], p["vwT"], p["gateT"], nsa_bias, overlap_t)
        x = _outproj(x, o_moba, o_diff, o_nsa, bf(w_out[l]))

        x = _ffn(x.reshape(b * s, d), norm_ffn2[l], bf(ffn2_gate[l]), bf(ffn2_up[l]), bf(ffn2_down[l])).reshape(b, s, d)
    return _final_norm(x.reshape(b * s, d), final_norm).reshape(b, s, d)
```

```python
import functools
import math

import numpy as np
import jax
import jax.numpy as jnp
from jax import lax
from jax.experimental import pallas as pl
from jax.experimental.pallas import tpu as pltpu

F32 = jnp.float32
BF16 = jnp.bfloat16

D_MODEL = 1024
HEAD_DIM = 64
MOBA_HEADS = 4
MOBA_BLOCK = 256
MOBA_TOPK = 3
DIFF_HEADS = 4
DIFF_QK_DIM = HEAD_DIM // 2
NSA_HEADS = 8
NSA_KV_HEADS = 2
NSA_GROUP = NSA_HEADS // NSA_KV_HEADS
NSA_CMP_BLOCK = 32
NSA_CMP_STRIDE = 16
NSA_CMP_HIDDEN = 256
NSA_SLC_BLOCK = 64
NSA_SLC_TOPK = 16
NSA_LOCAL_BLOCKS = 2
NSA_WINDOW = 512
REL_BUCKETS = 32
REL_MAX_DIST = 128
D_FF = 2816
RMS_EPS = 1e-6
NEG_INF = -1e30
FORCE_SCORE = 1e4

NSA_GATE_W = 3 * NSA_HEADS
NSA_GATE_ROWS = 16
LANES = 128
SUBLANES = 8
ATT_TILE = 256
VMEM_LIMIT = 48 * 1024 * 1024
LOG2E = math.log2(math.e)

NT_DIMS = (((1,), (1,)), ((), ()))


def _rms(x, g):
    return x * lax.rsqrt(jnp.mean(x * x, axis=-1, keepdims=True) + RMS_EPS) * g


def _params(*sem):
    return pltpu.CompilerParams(dimension_semantics=sem, vmem_limit_bytes=VMEM_LIMIT)


def _ffn_kernel(x_ref, g_ref, wg_ref, wu_ref, wd_ref, o_ref, h_ref, *, tf):
    xn = _rms(x_ref[...], g_ref[...]).astype(BF16)
    for f in range(h_ref.shape[1] // tf):
        cols = slice(f * tf, (f + 1) * tf)
        gate = jnp.dot(xn, wg_ref[:, cols], preferred_element_type=F32)
        up = jnp.dot(xn, wu_ref[:, cols], preferred_element_type=F32)
        h_ref[:, cols] = (gate * jax.nn.sigmoid(gate) * up).astype(BF16)
    o_ref[...] = x_ref[...] + 0.5 * jnp.dot(h_ref[...], wd_ref[...], preferred_element_type=F32)


def _ffn(x2, g, wg, wu, wd, *, tm=1024, tf=256):
    n, d = x2.shape
    ff = wg.shape[1]
    resident = dict(pipeline_mode=pl.Buffered(1))
    return pl.pallas_call(
        functools.partial(_ffn_kernel, tf=tf),
        grid=(n // tm,),
        in_specs=[
            pl.BlockSpec((tm, d), lambda i: (i, 0)),
            pl.BlockSpec((1, d), lambda i: (0, 0)),
            pl.BlockSpec((d, ff), lambda i: (0, 0), **resident),
            pl.BlockSpec((d, ff), lambda i: (0, 0), **resident),
            pl.BlockSpec((ff, d), lambda i: (0, 0), **resident),
        ],
        out_specs=pl.BlockSpec((tm, d), lambda i: (i, 0)),
        out_shape=jax.ShapeDtypeStruct((n, d), F32),
        scratch_shapes=[pltpu.VMEM((tm, ff), BF16)],
        compiler_params=_params("parallel"),
        name="ffn",
    )(x2, g.reshape(1, d), wg, wu, wd)


_ROW_OUTS = (
    ("mq", 0, 4, BF16, HEAD_DIM ** -0.5, None),
    ("mk", 256, 4, BF16, 1.0, 0),
    ("dq", 768, 4, BF16, 1.0, None),
    ("dk", 1024, 4, BF16, 1.0, None),
    ("nq", 1536, 8, BF16, HEAD_DIM ** -0.5, None),
    ("kc", 2048, 2, F32, 1.0, None),
    ("vc", 2176, 2, F32, 1.0, None),
    ("ks", 2304, 2, BF16, 1.0, 1),
    ("kw", 2560, 2, BF16, 1.0, None),
)
_MEMBER_BLOCKS = (MOBA_BLOCK, NSA_SLC_BLOCK)
_COL_OUTS = (
    ("mvT", 512, 256, BF16),
    ("dvT", 1280, 256, BF16),
    ("vsT", 2432, 128, BF16),
    ("vwT", 2688, 128, BF16),
    ("gateT", None, NSA_KV_HEADS * NSA_GATE_ROWS, F32),
)


def _inproj_kernel(x_ref, g_ref, w_ref, wt_ref, member_ref, *out_refs):
    xn = _rms(x_ref[0], g_ref[...]).astype(BF16)
    col = 0
    for (_, _, heads, dtype, scale, member), o_ref in zip(_ROW_OUTS, out_refs):
        width = heads * HEAD_DIM
        p = jnp.dot(xn, w_ref[:, col:col + width], preferred_element_type=F32)
        if scale != 1.0:
            p = p * scale
        for h in range(heads):
            head = p[:, h * HEAD_DIM:(h + 1) * HEAD_DIM].astype(dtype)
            o_ref[0, h] = head if member is None else jnp.concatenate([head, member_ref[member]], axis=1)
        col += width
    pt = lax.dot_general(wt_ref[...], xn, NT_DIMS, preferred_element_type=F32)
    row = 0
    for (_, _, rows, dtype), o_ref in zip(_COL_OUTS, out_refs[len(_ROW_OUTS):]):
        o_ref[0] = pt[row:row + rows].astype(dtype)
        row += rows


def _inproj_weights(w_l):
    w_rows = jnp.concatenate([w_l[:, c:c + h * HEAD_DIM] for _, c, h, _, _, _ in _ROW_OUTS], axis=1)
    gate_cols = w_l[:, w_l.shape[1] - NSA_GATE_W:]
    per_group = 3 * NSA_GROUP
    gate_t = jnp.zeros((NSA_KV_HEADS * NSA_GATE_ROWS, w_l.shape[0]), w_l.dtype)
    for g in range(NSA_KV_HEADS):
        gate_t = gate_t.at[g * NSA_GATE_ROWS:g * NSA_GATE_ROWS + per_group].set(
            gate_cols[:, g * per_group:(g + 1) * per_group].T)
    w_cols = jnp.concatenate([w_l[:, c:c + r].T for _, c, r, _ in _COL_OUTS[:-1]] + [gate_t], axis=0)
    return w_rows.astype(BF16), w_cols.astype(BF16)


def _inproj(x, g, w_rows, w_cols, member, *, ts=512):
    b, s, d = x.shape
    widths = [HEAD_DIM if member is None else 2 * HEAD_DIM for _, _, _, _, _, member in _ROW_OUTS]
    out_shape = [jax.ShapeDtypeStruct((b, o[2], s, w), o[3]) for o, w in zip(_ROW_OUTS, widths)]
    out_specs = [pl.BlockSpec((1, o[2], ts, w), lambda bi, i: (bi, 0, i, 0)) for o, w in zip(_ROW_OUTS, widths)]
    out_shape += [jax.ShapeDtypeStruct((b, rows, s), dtype) for _, _, rows, dtype in _COL_OUTS]
    out_specs += [pl.BlockSpec((1, rows, ts), lambda bi, i: (bi, 0, i)) for _, _, rows, _ in _COL_OUTS]
    outs = pl.pallas_call(
        _inproj_kernel,
        grid=(b, s // ts),
        in_specs=[
            pl.BlockSpec((1, ts, d), lambda bi, i: (bi, i, 0)),
            pl.BlockSpec((1, d), lambda bi, i: (0, 0)),
            pl.BlockSpec(w_rows.shape, lambda bi, i: (0, 0)),
            pl.BlockSpec(w_cols.shape, lambda bi, i: (0, 0)),
            pl.BlockSpec((member.shape[0], ts, HEAD_DIM), lambda bi, i: (0, i, 0)),
        ],
        out_specs=out_specs,
        out_shape=out_shape,
        compiler_params=_params("parallel", "parallel"),
        name="inproj",
    )(x, g.reshape(1, d), w_rows, w_cols, member)
    names = [o[0] for o in _ROW_OUTS] + [o[0] for o in _COL_OUTS]
    return dict(zip(names, outs))


ONES_ROWS = 16
EXP_ROWS = 32
_ADD_TILE = {"own": 0, "near": 1, "edge": 2, "far": None}


def _tile_schedule(n_qt, window=None):
    tiles = [("own", i, i) for i in range(n_qt)] + [("near", i, i - 1) for i in range(1, n_qt)]
    if window is None:
        tiles += [("far", i, j) for j in range(n_qt - 2) for i in range(j + 2, n_qt)]
    else:
        assert window == 2
        tiles += [("edge", i, i - window) for i in range(window, n_qt)]
    kinds = tuple(kind for kind, _, _ in tiles)
    return kinds, jnp.asarray(np.array([[i for _, i, _ in tiles], [j for _, _, j in tiles]], np.int32))


def _attend_tiles(kinds, tab_ref, score_inputs, value_tiles, m_ref, acc_ref, s_refs, p_refs, c=LOG2E):
    tq = m_ref.shape[-1]

    def scores(t, kind, slot):
        qi, kj = tab_ref[0, t], tab_ref[1, t]
        rows = pl.ds(pl.multiple_of(qi * SUBLANES, SUBLANES), SUBLANES)
        pend = []
        for ci, (k, q, adds) in enumerate(score_inputs(qi, kj, kind)):
            s = lax.dot_general(k, q, NT_DIMS, preferred_element_type=F32)
            for a in adds:
                s = s + a
            s_refs[slot][ci] = s
            m_old = m_ref[ci, rows, :][0:1, :]
            m_new = jnp.maximum(m_old, jnp.max(s, axis=0, keepdims=True))
            m_ref[ci, rows, :] = jnp.broadcast_to(m_new, (SUBLANES, tq))
            pend.append((m_new, jnp.exp2((m_old - m_new) * c)))
        return qi, kj, tuple(pend)

    def accumulate(pending, slot):
        qi, kj, pend = pending
        s_ref, p_ref = s_refs[slot], p_refs[slot]
        tk = s_ref.shape[1]
        for ci, (v_t, (m_new, alpha)) in enumerate(zip(value_tiles(kj), pend)):
            for r in range(tk // EXP_ROWS):
                rows = slice(r * EXP_ROWS, (r + 1) * EXP_ROWS)
                p_ref[ci, rows, :] = jnp.exp2(((s_ref[ci, rows, :] - m_new) * c).astype(BF16))
            v_ext = jnp.concatenate([v_t, jnp.ones((ONES_ROWS, tk), BF16)], axis=0)
            acc_ref[ci, qi] = alpha * acc_ref[ci, qi] + jnp.dot(v_ext, p_ref[ci], preferred_element_type=F32)

    m_ref[...] = jnp.full(m_ref.shape, NEG_INF, F32)
    acc_ref[...] = jnp.zeros(acc_ref.shape, F32)
    n = len(kinds)
    pending = scores(0, kinds[0], 0)
    step = 0
    while step < n - 1:
        kind = kinds[step + 1]
        run = 1
        while step + run < n - 1 and kinds[step + run + 1] == kind:
            run += 1

        def pair(u, pending, step=step, kind=kind):
            t = step + 2 * u
            nxt = scores(t + 1, kind, (step + 1) % 2)
            accumulate(pending, step % 2)
            nxt2 = scores(t + 2, kind, step % 2)
            accumulate(nxt, (step + 1) % 2)
            return nxt2

        if run // 2:
            pending = lax.fori_loop(0, run // 2, pair, pending)
            step += 2 * (run // 2)
        if run % 2:
            nxt = scores(step + 1, kind, (step + 1) % 2)
            accumulate(pending, step % 2)
            pending = nxt
            step += 1
    accumulate(pending, (n - 1) % 2)


def _rows(ref, j, t=ATT_TILE):
    return ref[pl.ds(pl.multiple_of(j * t, t), t), :]


def _cols(ref, j, t=ATT_TILE):
    return ref[:, pl.ds(pl.multiple_of(j * t, t), t)]


def _normalized(acc, dv=HEAD_DIM):
    return acc[:dv] / acc[dv:dv + 1]


def _topk_rows(score, n_rows, topk):
    row_id = lax.broadcasted_iota(jnp.int32, score.shape, 0)
    rank = jnp.zeros(score.shape, F32)
    for c in range(n_rows):
        other = score[c:c + 1, :]
        beats = (other > score) | ((other == score) & (row_id > c))
        rank = rank + jnp.where(beats, 1.0, 0.0)
    return (rank < topk) & (row_id < n_rows)


def _extend_query(q, chosen):
    tq, dh = q.shape
    pen = jnp.where(chosen, 0.0, NEG_INF)
    pen = jnp.concatenate([pen, jnp.zeros((LANES - pen.shape[0], tq), F32)], axis=0)
    return jnp.concatenate([q, pen.T[:, :dh].astype(q.dtype)], axis=1)


def _attention_scratch(n, n_qt, t, dv=HEAD_DIM):
    return [pltpu.VMEM((n, n_qt * SUBLANES, t), F32), pltpu.VMEM((n, n_qt, dv + ONES_ROWS, t), F32),
            pltpu.VMEM((n, t, t), F32), pltpu.VMEM((n, t, t), F32),
            pltpu.VMEM((n, t, t), BF16), pltpu.VMEM((n, t, t), BF16)]


SMEM_SPEC = pl.BlockSpec(memory_space=pltpu.SMEM)


def _moba_kernel(tab_ref, q_ref, k_ref, vt_ref, bias_ref, avg_ref, o_ref, qx_ref, m_ref, acc_ref,
                 s0_ref, s1_ref, p0_ref, p1_ref, *, kinds):
    t = ATT_TILE
    s = q_ref.shape[2]
    n_blk = s // MOBA_BLOCK
    heads = range(MOBA_HEADS)
    blk_rows = -(-n_blk // SUBLANES) * SUBLANES
    blk = lax.broadcasted_iota(jnp.int32, (blk_rows, s), 0)
    own = lax.broadcasted_iota(jnp.int32, (blk_rows, s), 1) // MOBA_BLOCK

    for h in heads:
        k_mean = jnp.dot(avg_ref[...], k_ref[0, h], preferred_element_type=F32)[:, :HEAD_DIM]
        km_hi = k_mean.astype(BF16)
        km_lo = (k_mean - km_hi.astype(F32)).astype(BF16)
        q = q_ref[0, h]
        gate = (lax.dot_general(km_hi, q, NT_DIMS, preferred_element_type=F32)
                + lax.dot_general(km_lo, q, NT_DIMS, preferred_element_type=F32))
        gate = jnp.where(blk < own, gate[:blk_rows], NEG_INF)
        chosen = (_topk_rows(gate, n_blk, MOBA_TOPK) & (blk < own)) | (blk == own)
        for qi in range(s // t):
            cols = slice(qi * t, (qi + 1) * t)
            qx_ref[h, cols, :] = _extend_query(q_ref[0, h, cols, :], chosen[:, cols])

    def score_inputs(qi, kj, kind):
        tile = _ADD_TILE[kind]
        return [(_rows(k_ref.at[0, h], kj), _rows(qx_ref.at[h], qi), [] if tile is None else [bias_ref[h, tile]])
                for h in heads]

    def value_tiles(kj):
        return [_cols(vt_ref.at[0, h * HEAD_DIM:(h + 1) * HEAD_DIM], kj) for h in heads]

    _attend_tiles(kinds, tab_ref, score_inputs, value_tiles, m_ref, acc_ref, (s0_ref, s1_ref), (p0_ref, p1_ref))

    def finish(qi, carry):
        o_t = jnp.concatenate([_normalized(acc_ref[h, qi]) for h in heads], axis=0)
        o_ref[0, pl.ds(pl.multiple_of(qi * t, t), t), :] = o_t.T.astype(o_ref.dtype)
        return carry

    lax.fori_loop(0, s // t, finish, 0)


def _moba(mq, mk, mv_t, bias_near, avg):
    b, h, s, dh = mq.shape
    t = ATT_TILE
    assert MOBA_BLOCK == t and s % t == 0 and s // t <= avg.shape[0]
    kinds, table = _tile_schedule(s // t)
    return pl.pallas_call(
        functools.partial(_moba_kernel, kinds=kinds),
        grid=(b,),
        in_specs=[
            SMEM_SPEC,
            pl.BlockSpec((1, h, s, dh), lambda bi: (bi, 0, 0, 0)),
            pl.BlockSpec((1, h, s, mk.shape[3]), lambda bi: (bi, 0, 0, 0)),
            pl.BlockSpec((1, h * dh, s), lambda bi: (bi, 0, 0)),
            pl.BlockSpec(bias_near.shape, lambda bi: (0, 0, 0, 0)),
            pl.BlockSpec(avg.shape, lambda bi: (0, 0)),
        ],
        out_specs=pl.BlockSpec((1, s, h * dh), lambda bi: (bi, 0, 0)),
        out_shape=jax.ShapeDtypeStruct((b, s, h * dh), BF16),
        scratch_shapes=[pltpu.VMEM((h, s, mk.shape[3]), BF16)] + _attention_scratch(h, s // t, t),
        compiler_params=_params("parallel"),
        name="moba",
    )(table, mq, mk, mv_t, bias_near, avg)


def _diff_kernel(tab_ref, q_ref, k_ref, vt_ref, bias_ref, lam_ref, subln_ref, o_ref, m_ref, acc_ref,
                 s0_ref, s1_ref, p0_ref, p1_ref, *, kinds, lambda_init):
    t = ATT_TILE
    heads = range(DIFF_HEADS)
    lane = lax.broadcasted_iota(jnp.int32, (t, HEAD_DIM), 1)
    c = DIFF_QK_DIM ** -0.5 * LOG2E
    lp = lam_ref[...]
    lam = (jnp.exp(jnp.sum(lp[0:1] * lp[1:2], axis=-1, keepdims=True))
           - jnp.exp(jnp.sum(lp[2:3] * lp[3:4], axis=-1, keepdims=True)) + lambda_init)

    def score_inputs(qi, kj, kind):
        tile = _ADD_TILE[kind]
        out = []
        for h in heads:
            k = _rows(k_ref.at[0, h], kj)
            q = _rows(q_ref.at[0, h], qi)
            adds = [] if tile is None else [bias_ref[h, tile]]
            out.append((k, jnp.where(lane < DIFF_QK_DIM, q, jnp.zeros_like(q)), adds))
            out.append((k, jnp.where(lane >= DIFF_QK_DIM, q, jnp.zeros_like(q)), adds))
        return out

    def value_tiles(kj):
        tiles = [_cols(vt_ref.at[0, h * HEAD_DIM:(h + 1) * HEAD_DIM], kj) for h in heads]
        return [tiles[h] for h in heads for _ in range(2)]

    _attend_tiles(kinds, tab_ref, score_inputs, value_tiles, m_ref, acc_ref, (s0_ref, s1_ref), (p0_ref, p1_ref), c=c)

    def finish(qi, carry):
        outs = []
        for h in heads:
            o = _normalized(acc_ref[2 * h, qi]) - lam * _normalized(acc_ref[2 * h + 1, qi])
            o = o * lax.rsqrt(jnp.mean(o * o, axis=0, keepdims=True) + RMS_EPS) * subln_ref[...]
            outs.append(o * (1.0 - lambda_init))
        o_ref[0, pl.ds(pl.multiple_of(qi * t, t), t), :] = jnp.concatenate(outs, axis=0).T.astype(o_ref.dtype)
        return carry

    lax.fori_loop(0, q_ref.shape[2] // t, finish, 0)


def _diff(dq, dk, dv_t, bias_near, lam_params, subln_g, lambda_init):
    b, h, s, dh = dq.shape
    t = ATT_TILE
    kinds, table = _tile_schedule(s // t)
    return pl.pallas_call(
        functools.partial(_diff_kernel, kinds=kinds, lambda_init=lambda_init),
        grid=(b,),
        in_specs=[
            SMEM_SPEC,
            pl.BlockSpec((1, h, s, dh), lambda bi: (bi, 0, 0, 0)),
            pl.BlockSpec((1, h, s, dh), lambda bi: (bi, 0, 0, 0)),
            pl.BlockSpec((1, h * dh, s), lambda bi: (bi, 0, 0)),
            pl.BlockSpec(bias_near.shape, lambda bi: (0, 0, 0, 0)),
            pl.BlockSpec(lam_params.shape, lambda bi: (0, 0)),
            pl.BlockSpec((dh, 1), lambda bi: (0, 0)),
        ],
        out_specs=pl.BlockSpec((1, s, h * dh), lambda bi: (bi, 0, 0)),
        out_shape=jax.ShapeDtypeStruct((b, s, h * dh), BF16),
        scratch_shapes=_attention_scratch(2 * h, s // t, t),
        compiler_params=_params("parallel"),
        name="diff",
    )(table, dq, dk, dv_t, bias_near, lam_params, subln_g.reshape(dh, 1))


def _compress_kernel(kc_ref, vc_ref, pe_ref, w1_ref, w2k_ref, w2vt_ref, ko_ref, vo_ref):
    half = NSA_CMP_STRIDE * HEAD_DIM

    def hidden(t, c_ref):
        c = c_ref[0, 0]
        top = (c + pe_ref[2 * t:2 * t + 1]).astype(BF16)
        bot = (c + pe_ref[2 * t + 1:2 * t + 2]).astype(BF16)
        a = jnp.dot(top, w1_ref[t, :half], preferred_element_type=F32)
        bm = jnp.dot(bot, w1_ref[t, half:], preferred_element_type=F32)
        hid = a + pltpu.roll(bm, bm.shape[0] - 1, 0)
        return jax.nn.gelu(hid).astype(BF16)

    ko_ref[0, 0] = jnp.dot(hidden(0, kc_ref), w2k_ref[...], preferred_element_type=F32)
    vo_ref[0, 0] = lax.dot_general(w2vt_ref[...], hidden(1, vc_ref), NT_DIMS, preferred_element_type=F32)


def _compress(kc, vc, pe4, w1, w2k, w2v_t):
    b, g, s, dh = kc.shape
    n_chunk = s // NSA_CMP_STRIDE
    width = NSA_CMP_STRIDE * dh
    kc = kc.reshape(b, g, n_chunk, width)
    vc = vc.reshape(b, g, n_chunk, width)
    spec_in = pl.BlockSpec((1, 1, n_chunk, width), lambda bi, gi: (bi, gi, 0, 0))
    return pl.pallas_call(
        _compress_kernel,
        grid=(b, g),
        in_specs=[
            spec_in, spec_in,
            pl.BlockSpec(pe4.shape, lambda bi, gi: (0, 0)),
            pl.BlockSpec(w1.shape, lambda bi, gi: (0, 0, 0)),
            pl.BlockSpec(w2k.shape, lambda bi, gi: (0, 0)),
            pl.BlockSpec(w2v_t.shape, lambda bi, gi: (0, 0)),
        ],
        out_specs=[pl.BlockSpec((1, 1, n_chunk, dh), lambda bi, gi: (bi, gi, 0, 0)),
                   pl.BlockSpec((1, 1, dh, n_chunk), lambda bi, gi: (bi, gi, 0, 0))],
        out_shape=[jax.ShapeDtypeStruct((b, g, n_chunk, dh), F32), jax.ShapeDtypeStruct((b, g, dh, n_chunk), F32)],
        compiler_params=_params("parallel", "parallel"),
        name="nsa_compress",
    )(kc, vc, pe4, w1, w2k, w2v_t)


def _nsa_kernel(slc_tab_ref, win_tab_ref, q_ref, kcmp_ref, vcmpt_ref, ks_ref, vst_ref, kw_ref, vwt_ref, gate_ref,
                bias_ref, overlap_ref, o_ref, qx_ref, ocmp_ref, m_ref, acc_ref, m2_ref, acc2_ref,
                s0_ref, s1_ref, p0_ref, p1_ref, *, slc_kinds, win_kinds):
    t = ATT_TILE
    heads = range(NSA_GROUP)
    s = q_ref.shape[2]
    n_chunk = kcmp_ref.shape[2]
    n_slc = s // NSA_SLC_BLOCK
    ks_g, vst_g = ks_ref.at[0, 0], vst_ref.at[0]
    kw_g, vwt_g = kw_ref.at[0, 0], vwt_ref.at[0]
    k_cmp = kcmp_ref[0, 0].astype(BF16)
    v_cmp_t = vcmpt_ref[0, 0].astype(BF16)
    cmp_id = lax.broadcasted_iota(jnp.int32, (n_chunk, t), 0)
    blk = lax.broadcasted_iota(jnp.int32, (n_slc, t), 0)

    def select(qi, carry):
        qs = [_rows(q_ref.at[0, r], qi) for r in heads]
        q_pos = qi * t + lax.broadcasted_iota(jnp.int32, (n_chunk, t), 1)
        cmp_valid = (cmp_id * NSA_CMP_STRIDE + (NSA_CMP_BLOCK - 1) <= q_pos) & (cmp_id < n_chunk - 1)
        p_sum = jnp.zeros((n_chunk, t), F32)
        for r in heads:
            sc = lax.dot_general(k_cmp, qs[r], NT_DIMS, preferred_element_type=F32)
            m = jnp.max(jnp.where(cmp_valid, sc, NEG_INF), axis=0, keepdims=True)
            e = jnp.where(cmp_valid, jnp.exp(sc - m), 0.0)
            l = jnp.sum(e, axis=0, keepdims=True)
            p = e / jnp.where(l > 0.0, l, 1.0)
            p_sum = p_sum + p
            ocmp_ref[r, qi] = jnp.dot(v_cmp_t, p.astype(BF16), preferred_element_type=F32)
        ps_hi = p_sum.astype(BF16)
        ps_lo = (p_sum - ps_hi.astype(F32)).astype(BF16)
        imp = (jnp.dot(overlap_ref[...], ps_hi, preferred_element_type=F32)
               + jnp.dot(overlap_ref[...], ps_lo, preferred_element_type=F32))
        cur = (qi * t + lax.broadcasted_iota(jnp.int32, (n_slc, t), 1)) // NSA_SLC_BLOCK
        forced = (blk == 0) | (blk > cur - NSA_LOCAL_BLOCKS)
        score = jnp.where(blk <= cur, jnp.where(forced, FORCE_SCORE, imp), NEG_INF)
        sel = _topk_rows(score, n_slc, min(NSA_SLC_TOPK, n_slc))
        rows = pl.ds(pl.multiple_of(qi * t, t), t)
        for r in heads:
            qx_ref[r, rows, :] = _extend_query(qs[r], sel)
        return carry

    n_qt = s // t
    lax.fori_loop(0, n_qt // 2, lambda u, carry: select(2 * u + 1, select(2 * u, carry)), 0)
    if n_qt % 2:
        select(n_qt - 1, 0)

    def adds(r, kind):
        return [] if _ADD_TILE[kind] is None else [bias_ref[0, r, _ADD_TILE[kind]]]

    def slc_inputs(qi, kj, kind):
        return [(_rows(ks_g, kj), _rows(qx_ref.at[r], qi), adds(r, kind)) for r in heads]

    def win_inputs(qi, kj, kind):
        return [(_rows(kw_g, kj), _rows(q_ref.at[0, r], qi), adds(r, kind)) for r in heads]

    bufs = ((s0_ref, s1_ref), (p0_ref, p1_ref))
    _attend_tiles(slc_kinds, slc_tab_ref, slc_inputs, lambda kj: [_cols(vst_g, kj)] * len(heads), m_ref, acc_ref, *bufs)
    _attend_tiles(win_kinds, win_tab_ref, win_inputs, lambda kj: [_cols(vwt_g, kj)] * len(heads), m2_ref, acc2_ref, *bufs)

    def finish(qi, carry):
        cols = pl.ds(pl.multiple_of(qi * t, t), t)
        gates = jax.nn.sigmoid(gate_ref[0, :, cols])
        outs = []
        for r in heads:
            g_cmp, g_slc, g_win = (gates[3 * r + br:3 * r + br + 1, :] for br in range(3))
            outs.append(g_cmp * ocmp_ref[r, qi] + g_slc * _normalized(acc_ref[r, qi]) + g_win * _normalized(acc2_ref[r, qi]))
        o_ref[0, cols, :] = jnp.concatenate(outs, axis=0).T.astype(o_ref.dtype)
        return carry

    lax.fori_loop(0, s // t, finish, 0)


def _nsa(nq, k_cmp, v_cmp_t, ks, vs_t, kw, vw_t, gate_t, bias_tiles, overlap_t):
    b, _, s, dh = nq.shape
    g, r, t = NSA_KV_HEADS, NSA_GROUP, ATT_TILE
    assert NSA_WINDOW == 2 * t and s % t == 0 and t % NSA_SLC_BLOCK == 0
    slc_kinds, slc_table = _tile_schedule(s // t)
    win_kinds, win_table = _tile_schedule(s // t, window=NSA_WINDOW // t)
    k_spec = lambda k: pl.BlockSpec((1, 1, s, k.shape[3]), lambda bi, gi: (bi, gi, 0, 0))
    vt_spec = pl.BlockSpec((1, dh, s), lambda bi, gi: (bi, gi, 0))
    scratch = _attention_scratch(r, s // t, t)
    return pl.pallas_call(
        functools.partial(_nsa_kernel, slc_kinds=slc_kinds, win_kinds=win_kinds),
        grid=(b, g),
        in_specs=[
            SMEM_SPEC, SMEM_SPEC,
            pl.BlockSpec((1, r, s, dh), lambda bi, gi: (bi, gi, 0, 0)),
            pl.BlockSpec((1, 1) + k_cmp.shape[2:], lambda bi, gi: (bi, gi, 0, 0)),
            pl.BlockSpec((1, 1) + v_cmp_t.shape[2:], lambda bi, gi: (bi, gi, 0, 0)),
            k_spec(ks), vt_spec, k_spec(kw), vt_spec,
            pl.BlockSpec((1, NSA_GATE_ROWS, s), lambda bi, gi: (bi, gi, 0)),
            pl.BlockSpec((1,) + bias_tiles.shape[1:], lambda bi, gi: (gi, 0, 0, 0, 0)),
            pl.BlockSpec(overlap_t.shape, lambda bi, gi: (0, 0)),
        ],
        out_specs=pl.BlockSpec((1, s, r * dh), lambda bi, gi: (bi, 0, gi)),
        out_shape=jax.ShapeDtypeStruct((b, s, g * r * dh), BF16),
        scratch_shapes=[pltpu.VMEM((r, s, ks.shape[3]), BF16), pltpu.VMEM((r, s // t, dh, t), F32)]
                       + scratch[:2] + scratch,
        compiler_params=_params("parallel", "parallel"),
        name="nsa",
    )(slc_table, win_table, nq, k_cmp, v_cmp_t, ks, vs_t, kw, vw_t, gate_t, bias_tiles, overlap_t)


def _outproj_kernel(x_ref, om_ref, od_ref, on_ref, w_ref, o_ref):
    wm = om_ref.shape[-1]
    wd = od_ref.shape[-1]
    y = jnp.dot(om_ref[0], w_ref[:wm], preferred_element_type=F32)
    y = y + jnp.dot(od_ref[0], w_ref[wm:wm + wd], preferred_element_type=F32)
    y = y + jnp.dot(on_ref[0], w_ref[wm + wd:], preferred_element_type=F32)
    o_ref[0] = x_ref[0] + y


def _outproj(x, o_m, o_d, o_n, w, *, ts=512):
    b, s, d = x.shape

    def spec(a):
        return pl.BlockSpec((1, ts, a.shape[-1]), lambda bi, i: (bi, i, 0))

    return pl.pallas_call(
        _outproj_kernel,
        grid=(b, s // ts),
        in_specs=[spec(x), spec(o_m), spec(o_d), spec(o_n), pl.BlockSpec(w.shape, lambda bi, i: (0, 0))],
        out_specs=spec(x),
        out_shape=jax.ShapeDtypeStruct((b, s, d), F32),
        compiler_params=_params("parallel", "parallel"),
        name="outproj",
    )(x, o_m, o_d, o_n, w)


def _final_norm_kernel(x_ref, g_ref, o_ref):
    o_ref[...] = _rms(x_ref[...], g_ref[...])


def _final_norm(x2, g, *, tm=1024):
    n, d = x2.shape
    return pl.pallas_call(
        _final_norm_kernel,
        grid=(n // tm,),
        in_specs=[pl.BlockSpec((tm, d), lambda i: (i, 0)), pl.BlockSpec((1, d), lambda i: (0, 0))],
        out_specs=pl.BlockSpec((tm, d), lambda i: (i, 0)),
        out_shape=jax.ShapeDtypeStruct((n, d), F32),
        compiler_params=_params("parallel"),
        name="final_norm",
    )(x2, g.reshape(1, d))


def _rel_bucket(dist):
    n = jnp.maximum(dist, 0)
    max_exact = REL_BUCKETS // 2
    n_f = jnp.maximum(n, max_exact).astype(F32)
    large = max_exact + (jnp.log(n_f / max_exact) / math.log(REL_MAX_DIST / max_exact)
                         * (REL_BUCKETS - max_exact)).astype(jnp.int32)
    return jnp.where(n < max_exact, n, jnp.minimum(large, REL_BUCKETS - 1))


def _bias_tiles(rel_bias):
    t = ATT_TILE
    assert t >= REL_MAX_DIST
    heads = rel_bias.shape[1]
    width = 2 * t + 1
    by_dist = rel_bias[_rel_bucket(jnp.arange(width))] - rel_bias[REL_BUCKETS - 1]
    skew = jnp.broadcast_to(by_dist.T[:, None, :], (heads, t, width)).reshape(heads, t * width)
    skew = skew[:, :t * (width - 1)].reshape(heads, t, width - 1)
    causal = jnp.arange(t)[:, None] <= jnp.arange(t)[None, :]
    own = jnp.where(causal, skew[:, :, :t], NEG_INF)
    return jnp.stack([own, skew[:, :, t:]], axis=1)


def _const_tables(s):
    n_moba = s // MOBA_BLOCK
    avg = np.zeros((2 * SUBLANES, s), np.float32)
    for j in range(n_moba):
        avg[j, j * MOBA_BLOCK:(j + 1) * MOBA_BLOCK] = 1.0 / MOBA_BLOCK
    n_cmp = (s - NSA_CMP_BLOCK) // NSA_CMP_STRIDE + 1
    n_slc = s // NSA_SLC_BLOCK
    cmp_start = np.arange(n_cmp) * NSA_CMP_STRIDE
    slc_start = np.arange(n_slc) * NSA_SLC_BLOCK
    ov = np.clip(np.minimum(cmp_start[:, None] + NSA_CMP_BLOCK, slc_start[None, :] + NSA_SLC_BLOCK)
                 - np.maximum(cmp_start[:, None], slc_start[None, :]), 0, None) / NSA_CMP_BLOCK
    overlap_t = np.zeros((n_slc, s // NSA_CMP_STRIDE), np.float32)
    overlap_t[:, :n_cmp] = ov.T
    t = ATT_TILE
    win_far = np.where(np.arange(t)[:, None] > np.arange(t)[None, :], 0.0, NEG_INF).astype(np.float32)
    member = np.zeros((len(_MEMBER_BLOCKS), s, HEAD_DIM), np.float32)
    for kind, block in enumerate(_MEMBER_BLOCKS):
        assert s // block <= HEAD_DIM
        member[kind, np.arange(s), np.arange(s) // block] = 1.0
    return jnp.asarray(avg, BF16), jnp.asarray(overlap_t, BF16), jnp.asarray(win_far), jnp.asarray(member, BF16)


def kernel(x, rel_bias, norm_ffn1, ffn1_gate, ffn1_up, ffn1_down, norm_mix, w_in, diff_lambda, diff_subln, nsa_cmp_pe, nsa_cmp_w1, nsa_cmp_w2, w_out, norm_ffn2, ffn2_gate, ffn2_up, ffn2_down, final_norm):
    b, s, d = x.shape
    depth = w_in.shape[0]
    assert d == D_MODEL
    h0, h1 = MOBA_HEADS, MOBA_HEADS + DIFF_HEADS
    near = _bias_tiles(rel_bias.astype(F32))
    avg, overlap_t, win_far, member = _const_tables(s)
    moba_bias = near[:h0]
    diff_bias = near[h0:h1] / (DIFF_QK_DIM ** -0.5)
    nsa_bias = jnp.concatenate([near[h1:], jnp.broadcast_to(win_far, near[h1:, :1].shape)], axis=1)
    nsa_bias = nsa_bias.reshape((NSA_KV_HEADS, NSA_GROUP) + nsa_bias.shape[1:])
    bf = lambda a: a.astype(BF16)

    for l in range(depth):
        lambda_init = 0.8 - 0.6 * math.exp(-0.3 * l)
        x = _ffn(x.reshape(b * s, d), norm_ffn1[l], bf(ffn1_gate[l]), bf(ffn1_up[l]), bf(ffn1_down[l])).reshape(b, s, d)

        p = _inproj(x, norm_mix[l], *_inproj_weights(w_in[l]), member)
        o_moba = _moba(p["mq"], p["mk"], p["mvT"], moba_bias, avg)
        o_diff = _diff(p["dq"], p["dk"], p["dvT"], diff_bias, diff_lambda[l].astype(F32), diff_subln[l].astype(F32),
                       lambda_init)
        pe4 = nsa_cmp_pe[l].astype(F32).reshape(4, NSA_CMP_STRIDE * HEAD_DIM)
        k_cmp, v_cmp_t = _compress(p["kc"], p["vc"], pe4, bf(nsa_cmp_w1[l]), bf(nsa_cmp_w2[l, 0]), bf(nsa_cmp_w2[l, 1].T))
        o_nsa = _nsa(p["nq"], k_cmp, v_cmp_t, p["ks"], p["vsT"], p["kw"], p["vwT"], p["gateT"], nsa_bias, overlap_t)
        x = _outproj(x, o_moba, o_diff, o_nsa, bf(w_out[l]))

        x = _ffn(x.reshape(b * s, d), norm_ffn2[l], bf(ffn2_gate[l]), bf(ffn2_up[l]), bf(ffn2_down[l])).reshape(b, s, d)
    return _final_norm(x.reshape(b * s, d), final_norm).reshape(b, s, d)
```

```python
import functools
import math

import numpy as np
import jax
import jax.numpy as jnp
from jax import lax
from jax.experimental import pallas as pl
from jax.experimental.pallas import tpu as pltpu

F32 = jnp.float32
BF16 = jnp.bfloat16

D_MODEL = 1024
HEAD_DIM = 64
MOBA_HEADS = 4
MOBA_BLOCK = 256
MOBA_TOPK = 3
DIFF_HEADS = 4
DIFF_QK_DIM = HEAD_DIM // 2
NSA_HEADS = 8
NSA_KV_HEADS = 2
NSA_GROUP = NSA_HEADS // NSA_KV_HEADS
NSA_CMP_BLOCK = 32
NSA_CMP_STRIDE = 16
NSA_CMP_HIDDEN = 256
NSA_SLC_BLOCK = 64
NSA_SLC_TOPK = 16
NSA_LOCAL_BLOCKS = 2
NSA_WINDOW = 512
REL_BUCKETS = 32
REL_MAX_DIST = 128
D_FF = 2816
RMS_EPS = 1e-6
NEG_INF = -1e30
FORCE_SCORE = 1e4

NSA_GATE_W = 3 * NSA_HEADS
NSA_GATE_ROWS = 16
LANES = 128
SUBLANES = 8
ATT_TILE = 256
VMEM_LIMIT = 48 * 1024 * 1024
LOG2E = math.log2(math.e)

NT_DIMS = (((1,), (1,)), ((), ()))


def _rms(x, g):
    return x * lax.rsqrt(jnp.mean(x * x, axis=-1, keepdims=True) + RMS_EPS) * g


def _params(*sem):
    return pltpu.CompilerParams(dimension_semantics=sem, vmem_limit_bytes=VMEM_LIMIT)


def _ffn_kernel(*refs, tf, n_mix, post_norm):
    x_ref, refs = refs[0], refs[1:]
    mix_refs, refs = refs[:n_mix], refs[n_mix:]
    if n_mix:
        wmix_ref, refs = refs[0], refs[1:]
    g_ref, wg_ref, wu_ref, wd_ref, refs = refs[0], refs[1], refs[2], refs[3], refs[4:]
    if post_norm:
        post_ref, refs = refs[0], refs[1:]
    o_ref, h_ref = refs

    y = x_ref[...]
    row = 0
    for m_ref in mix_refs:
        y = y + jnp.dot(m_ref[...], wmix_ref[row:row + m_ref.shape[1]], preferred_element_type=F32)
        row += m_ref.shape[1]
    o_ref[...] = y
    xn = _rms(y, g_ref[...]).astype(BF16)
    for f in range(h_ref.shape[1] // tf):
        cols = slice(f * tf, (f + 1) * tf)
        gate = jnp.dot(xn, wg_ref[:, cols], preferred_element_type=F32)
        up = jnp.dot(xn, wu_ref[:, cols], preferred_element_type=F32)
        h_ref[:, cols] = (gate * jax.nn.sigmoid(gate) * up).astype(BF16)
    o = o_ref[...] + 0.5 * jnp.dot(h_ref[...], wd_ref[...], preferred_element_type=F32)
    o_ref[...] = _rms(o, post_ref[...]) if post_norm else o


def _ffn(x2, g, wg, wu, wd, mix=(), w_mix=None, post_g=None, *, tm=1024, tf=256):
    n, d = x2.shape
    ff = wg.shape[1]
    resident = dict(pipeline_mode=pl.Buffered(1))
    row_spec = lambda a: pl.BlockSpec((tm, a.shape[1]), lambda i: (i, 0))
    whole = lambda a: pl.BlockSpec(a.shape, lambda i: (0, 0), **resident)
    gain = lambda a: a.reshape(1, d)
    args = [x2, *mix] + ([w_mix] if mix else []) + [gain(g), wg, wu, wd] + ([gain(post_g)] if post_g is not None else [])
    specs = ([row_spec(x2)] + [row_spec(m) for m in mix] + ([whole(w_mix)] if mix else [])
             + [pl.BlockSpec((1, d), lambda i: (0, 0)), whole(wg), whole(wu), whole(wd)]
             + ([pl.BlockSpec((1, d), lambda i: (0, 0))] if post_g is not None else []))
    return pl.pallas_call(
        functools.partial(_ffn_kernel, tf=tf, n_mix=len(mix), post_norm=post_g is not None),
        grid=(n // tm,),
        in_specs=specs,
        out_specs=row_spec(x2),
        out_shape=jax.ShapeDtypeStruct((n, d), F32),
        scratch_shapes=[pltpu.VMEM((tm, ff), BF16)],
        compiler_params=_params("parallel"),
        name="ffn",
    )(*args)


_ROW_OUTS = (
    ("mq", 0, 4, BF16, HEAD_DIM ** -0.5, None),
    ("mk", 256, 4, BF16, 1.0, 0),
    ("dq", 768, 4, BF16, 1.0, None),
    ("dk", 1024, 4, BF16, 1.0, None),
    ("nq", 1536, 8, BF16, HEAD_DIM ** -0.5, None),
    ("kc", 2048, 2, F32, 1.0, None),
    ("vc", 2176, 2, F32, 1.0, None),
    ("ks", 2304, 2, BF16, 1.0, 1),
    ("kw", 2560, 2, BF16, 1.0, None),
)
_MEMBER_BLOCKS = (MOBA_BLOCK, NSA_SLC_BLOCK)
_COL_OUTS = (
    ("mvT", 512, 256, BF16),
    ("dvT", 1280, 256, BF16),
    ("vsT", 2432, 128, BF16),
    ("vwT", 2688, 128, BF16),
    ("gateT", None, NSA_KV_HEADS * NSA_GATE_ROWS, F32),
)


def _inproj_kernel(x_ref, g_ref, w_ref, wt_ref, member_ref, *out_refs):
    xn = _rms(x_ref[0], g_ref[...]).astype(BF16)
    col = 0
    for (_, _, heads, dtype, scale, member), o_ref in zip(_ROW_OUTS, out_refs):
        width = heads * HEAD_DIM
        p = jnp.dot(xn, w_ref[:, col:col + width], preferred_element_type=F32)
        if scale != 1.0:
            p = p * scale
        for h in range(heads):
            head = p[:, h * HEAD_DIM:(h + 1) * HEAD_DIM].astype(dtype)
            o_ref[0, h] = head if member is None else jnp.concatenate([head, member_ref[member]], axis=1)
        col += width
    pt = lax.dot_general(wt_ref[...], xn, NT_DIMS, preferred_element_type=F32)
    row = 0
    for (_, _, rows, dtype), o_ref in zip(_COL_OUTS, out_refs[len(_ROW_OUTS):]):
        o_ref[0] = pt[row:row + rows].astype(dtype)
        row += rows


def _inproj_weights(w_l):
    w_rows = jnp.concatenate([w_l[:, c:c + h * HEAD_DIM] for _, c, h, _, _, _ in _ROW_OUTS], axis=1)
    gate_cols = w_l[:, w_l.shape[1] - NSA_GATE_W:]
    per_group = 3 * NSA_GROUP
    gate_t = jnp.zeros((NSA_KV_HEADS * NSA_GATE_ROWS, w_l.shape[0]), w_l.dtype)
    for g in range(NSA_KV_HEADS):
        gate_t = gate_t.at[g * NSA_GATE_ROWS:g * NSA_GATE_ROWS + per_group].set(
            gate_cols[:, g * per_group:(g + 1) * per_group].T)
    w_cols = jnp.concatenate([w_l[:, c:c + r].T for _, c, r, _ in _COL_OUTS[:-1]] + [gate_t], axis=0)
    return w_rows.astype(BF16), w_cols.astype(BF16)


def _inproj(x, g, w_rows, w_cols, member, *, ts=512):
    b, s, d = x.shape
    widths = [HEAD_DIM if member is None else 2 * HEAD_DIM for _, _, _, _, _, member in _ROW_OUTS]
    out_shape = [jax.ShapeDtypeStruct((b, o[2], s, w), o[3]) for o, w in zip(_ROW_OUTS, widths)]
    out_specs = [pl.BlockSpec((1, o[2], ts, w), lambda bi, i: (bi, 0, i, 0)) for o, w in zip(_ROW_OUTS, widths)]
    out_shape += [jax.ShapeDtypeStruct((b, rows, s), dtype) for _, _, rows, dtype in _COL_OUTS]
    out_specs += [pl.BlockSpec((1, rows, ts), lambda bi, i: (bi, 0, i)) for _, _, rows, _ in _COL_OUTS]
    outs = pl.pallas_call(
        _inproj_kernel,
        grid=(b, s // ts),
        in_specs=[
            pl.BlockSpec((1, ts, d), lambda bi, i: (bi, i, 0)),
            pl.BlockSpec((1, d), lambda bi, i: (0, 0)),
            pl.BlockSpec(w_rows.shape, lambda bi, i: (0, 0)),
            pl.BlockSpec(w_cols.shape, lambda bi, i: (0, 0)),
            pl.BlockSpec((member.shape[0], ts, HEAD_DIM), lambda bi, i: (0, i, 0)),
        ],
        out_specs=out_specs,
        out_shape=out_shape,
        compiler_params=_params("parallel", "parallel"),
        name="inproj",
    )(x, g.reshape(1, d), w_rows, w_cols, member)
    names = [o[0] for o in _ROW_OUTS] + [o[0] for o in _COL_OUTS]
    return dict(zip(names, outs))


ONES_ROWS = 16
EXP_ROWS = 32
TILES_PER_TRIP = 2
_ADD_TILE = {"own": 0, "near": 1, "edge": 2, "far": None}


def _tile_schedule(n_qt, window=None):
    tiles = [("own", i, i) for i in range(n_qt)] + [("near", i, i - 1) for i in range(1, n_qt)]
    if window is None:
        tiles += [("far", i, j) for j in range(n_qt - 2) for i in range(j + 2, n_qt)]
    else:
        assert window == 2
        tiles += [("edge", i, i - window) for i in range(window, n_qt)]
    kinds = tuple(kind for kind, _, _ in tiles)
    return kinds, jnp.asarray(np.array([[i for _, i, _ in tiles], [j for _, _, j in tiles]], np.int32))


def _attend_tiles(kinds, tab_ref, score_inputs, value_tiles, m_ref, acc_ref, s_refs, p_refs, c=LOG2E):
    tq = m_ref.shape[-1]

    def scores(t, kind, slot):
        qi, kj = tab_ref[0, t], tab_ref[1, t]
        rows = pl.ds(pl.multiple_of(qi * SUBLANES, SUBLANES), SUBLANES)
        pend = []
        for ci, (k, q, adds) in enumerate(score_inputs(qi, kj, kind)):
            s = lax.dot_general(k, q, NT_DIMS, preferred_element_type=F32)
            for a in adds:
                s = s + a
            s_refs[slot][ci, :s.shape[0], :] = s
            m_old = m_ref[ci, rows, :][0:1, :]
            m_new = jnp.maximum(m_old, jnp.max(s, axis=0, keepdims=True))
            m_ref[ci, rows, :] = jnp.broadcast_to(m_new, (SUBLANES, tq))
            pend.append((m_new, jnp.exp2((m_old - m_new) * c)))
        return qi, kj, tuple(pend)

    def accumulate(pending, slot):
        qi, kj, pend = pending
        s_ref, p_ref = s_refs[slot], p_refs[slot]
        for ci, (v_t, (m_new, alpha)) in enumerate(zip(value_tiles(kj), pend)):
            tk = v_t.shape[1]
            for r in range(tk // EXP_ROWS):
                rows = slice(r * EXP_ROWS, (r + 1) * EXP_ROWS)
                p_ref[ci, rows, :] = jnp.exp2(((s_ref[ci, rows, :] - m_new) * c).astype(BF16))
            v_ext = jnp.concatenate([v_t, jnp.ones((ONES_ROWS, tk), BF16)], axis=0)
            acc_ref[ci, qi] = alpha * acc_ref[ci, qi] + jnp.dot(v_ext, p_ref[ci, :tk, :], preferred_element_type=F32)

    m_ref[...] = jnp.full(m_ref.shape, NEG_INF, F32)
    acc_ref[...] = jnp.zeros(acc_ref.shape, F32)
    n = len(kinds)
    pending = scores(0, kinds[0], 0)
    step = 0
    while step < n - 1:
        kind = kinds[step + 1]
        run = 1
        while step + run < n - 1 and kinds[step + run + 1] == kind:
            run += 1

        def group(u, pending, step=step, kind=kind):
            for d in range(TILES_PER_TRIP):
                nxt = scores(step + TILES_PER_TRIP * u + d + 1, kind, (step + d + 1) % 2)
                accumulate(pending, (step + d) % 2)
                pending = nxt
            return pending

        if run // TILES_PER_TRIP:
            pending = lax.fori_loop(0, run // TILES_PER_TRIP, group, pending)
            step += run - run % TILES_PER_TRIP
        for _ in range(run % TILES_PER_TRIP):
            nxt = scores(step + 1, kind, (step + 1) % 2)
            accumulate(pending, step % 2)
            pending = nxt
            step += 1
    accumulate(pending, (n - 1) % 2)


def _rows(ref, j, t=ATT_TILE):
    return ref[pl.ds(pl.multiple_of(j * t, t), t), :]


def _cols(ref, j, t=ATT_TILE):
    return ref[:, pl.ds(pl.multiple_of(j * t, t), t)]


def _normalized(acc, dv=HEAD_DIM):
    return acc[:dv] / acc[dv:dv + 1]


def _topk_rows(score, n_rows, topk):
    row_id = lax.broadcasted_iota(jnp.int32, score.shape, 0)
    rank = jnp.zeros(score.shape, F32)
    for c in range(n_rows):
        other = score[c:c + 1, :]
        beats = (other > score) | ((other == score) & (row_id > c))
        rank = rank + jnp.where(beats, 1.0, 0.0)
    return (rank < topk) & (row_id < n_rows)


def _extend_query(q, chosen):
    tq, dh = q.shape
    pen = jnp.where(chosen, 0.0, NEG_INF)
    pen = jnp.concatenate([pen, jnp.zeros((LANES - pen.shape[0], tq), F32)], axis=0)
    return jnp.concatenate([q, pen.T[:, :dh].astype(q.dtype)], axis=1)


def _attention_scratch(n, n_qt, t, dv=HEAD_DIM):
    s_rows, p_rows = t + SUBLANES, t + 2 * SUBLANES
    return [pltpu.VMEM((n, n_qt * SUBLANES, t), F32), pltpu.VMEM((n, n_qt, dv + ONES_ROWS, t), F32),
            pltpu.VMEM((n, s_rows, t), F32), pltpu.VMEM((n, s_rows, t), F32),
            pltpu.VMEM((n, p_rows, t), BF16), pltpu.VMEM((n, p_rows, t), BF16)]


SMEM_SPEC = pl.BlockSpec(memory_space=pltpu.SMEM)


def _moba_kernel(tab_ref, q_ref, k_ref, vt_ref, bias_ref, avg_ref, o_ref, qx_ref, m_ref, acc_ref,
                 s0_ref, s1_ref, p0_ref, p1_ref, *, kinds):
    t = ATT_TILE
    s = q_ref.shape[2]
    n_blk = s // MOBA_BLOCK
    heads = range(MOBA_HEADS)
    blk_rows = -(-n_blk // SUBLANES) * SUBLANES
    blk = lax.broadcasted_iota(jnp.int32, (blk_rows, s), 0)
    own = lax.broadcasted_iota(jnp.int32, (blk_rows, s), 1) // MOBA_BLOCK

    for h in heads:
        k_mean = jnp.dot(avg_ref[...], k_ref[0, h], preferred_element_type=F32)[:, :HEAD_DIM]
        km_hi = k_mean.astype(BF16)
        km_lo = (k_mean - km_hi.astype(F32)).astype(BF16)
        q = q_ref[0, h]
        gate = (lax.dot_general(km_hi, q, NT_DIMS, preferred_element_type=F32)
                + lax.dot_general(km_lo, q, NT_DIMS, preferred_element_type=F32))
        gate = jnp.where(blk < own, gate[:blk_rows], NEG_INF)
        chosen = (_topk_rows(gate, n_blk, MOBA_TOPK) & (blk < own)) | (blk == own)
        for qi in range(s // t):
            cols = slice(qi * t, (qi + 1) * t)
            qx_ref[h, cols, :] = _extend_query(q_ref[0, h, cols, :], chosen[:, cols])

    def score_inputs(qi, kj, kind):
        tile = _ADD_TILE[kind]
        return [(_rows(k_ref.at[0, h], kj), _rows(qx_ref.at[h], qi), [] if tile is None else [bias_ref[h, tile]])
                for h in heads]

    def value_tiles(kj):
        return [_cols(vt_ref.at[0, h * HEAD_DIM:(h + 1) * HEAD_DIM], kj) for h in heads]

    _attend_tiles(kinds, tab_ref, score_inputs, value_tiles, m_ref, acc_ref, (s0_ref, s1_ref), (p0_ref, p1_ref))

    def finish(qi, carry):
        o_t = jnp.concatenate([_normalized(acc_ref[h, qi]) for h in heads], axis=0)
        o_ref[0, pl.ds(pl.multiple_of(qi * t, t), t), :] = o_t.T.astype(o_ref.dtype)
        return carry

    lax.fori_loop(0, s // t, finish, 0)


def _moba(mq, mk, mv_t, bias_near, avg):
    b, h, s, dh = mq.shape
    t = ATT_TILE
    assert MOBA_BLOCK == t and s % t == 0 and s // t <= avg.shape[0]
    kinds, table = _tile_schedule(s // t)
    return pl.pallas_call(
        functools.partial(_moba_kernel, kinds=kinds),
        grid=(b,),
        in_specs=[
            SMEM_SPEC,
            pl.BlockSpec((1, h, s, dh), lambda bi: (bi, 0, 0, 0)),
            pl.BlockSpec((1, h, s, mk.shape[3]), lambda bi: (bi, 0, 0, 0)),
            pl.BlockSpec((1, h * dh, s), lambda bi: (bi, 0, 0)),
            pl.BlockSpec(bias_near.shape, lambda bi: (0, 0, 0, 0)),
            pl.BlockSpec(avg.shape, lambda bi: (0, 0)),
        ],
        out_specs=pl.BlockSpec((1, s, h * dh), lambda bi: (bi, 0, 0)),
        out_shape=jax.ShapeDtypeStruct((b, s, h * dh), BF16),
        scratch_shapes=[pltpu.VMEM((h, s, mk.shape[3]), BF16)] + _attention_scratch(h, s // t, t),
        compiler_params=_params("parallel"),
        name="moba",
    )(table, mq, mk, mv_t, bias_near, avg)


def _diff_kernel(tab_ref, q_ref, k_ref, vt_ref, bias_ref, lam_ref, subln_ref, o_ref, m_ref, acc_ref,
                 s0_ref, s1_ref, p0_ref, p1_ref, *, kinds, lambda_init):
    t = ATT_TILE
    heads = range(DIFF_HEADS)
    lane = lax.broadcasted_iota(jnp.int32, (t, HEAD_DIM), 1)
    c = DIFF_QK_DIM ** -0.5 * LOG2E
    lp = lam_ref[...]
    lam = (jnp.exp(jnp.sum(lp[0:1] * lp[1:2], axis=-1, keepdims=True))
           - jnp.exp(jnp.sum(lp[2:3] * lp[3:4], axis=-1, keepdims=True)) + lambda_init)

    def score_inputs(qi, kj, kind):
        tile = _ADD_TILE[kind]
        out = []
        for h in heads:
            k = _rows(k_ref.at[0, h], kj)
            q = _rows(q_ref.at[0, h], qi)
            adds = [] if tile is None else [bias_ref[h, tile]]
            out.append((k, jnp.where(lane < DIFF_QK_DIM, q, jnp.zeros_like(q)), adds))
            out.append((k, jnp.where(lane >= DIFF_QK_DIM, q, jnp.zeros_like(q)), adds))
        return out

    def value_tiles(kj):
        tiles = [_cols(vt_ref.at[0, h * HEAD_DIM:(h + 1) * HEAD_DIM], kj) for h in heads]
        return [tiles[h] for h in heads for _ in range(2)]

    _attend_tiles(kinds, tab_ref, score_inputs, value_tiles, m_ref, acc_ref, (s0_ref, s1_ref), (p0_ref, p1_ref), c=c)

    def finish(qi, carry):
        outs = []
        for h in heads:
            o = _normalized(acc_ref[2 * h, qi]) - lam * _normalized(acc_ref[2 * h + 1, qi])
            o = o * lax.rsqrt(jnp.mean(o * o, axis=0, keepdims=True) + RMS_EPS) * subln_ref[...]
            outs.append(o * (1.0 - lambda_init))
        o_ref[0, pl.ds(pl.multiple_of(qi * t, t), t), :] = jnp.concatenate(outs, axis=0).T.astype(o_ref.dtype)
        return carry

    lax.fori_loop(0, q_ref.shape[2] // t, finish, 0)


def _diff(dq, dk, dv_t, bias_near, lam_params, subln_g, lambda_init):
    b, h, s, dh = dq.shape
    t = ATT_TILE
    kinds, table = _tile_schedule(s // t)
    return pl.pallas_call(
        functools.partial(_diff_kernel, kinds=kinds, lambda_init=lambda_init),
        grid=(b,),
        in_specs=[
            SMEM_SPEC,
            pl.BlockSpec((1, h, s, dh), lambda bi: (bi, 0, 0, 0)),
            pl.BlockSpec((1, h, s, dh), lambda bi: (bi, 0, 0, 0)),
            pl.BlockSpec((1, h * dh, s), lambda bi: (bi, 0, 0)),
            pl.BlockSpec(bias_near.shape, lambda bi: (0, 0, 0, 0)),
            pl.BlockSpec(lam_params.shape, lambda bi: (0, 0)),
            pl.BlockSpec((dh, 1), lambda bi: (0, 0)),
        ],
        out_specs=pl.BlockSpec((1, s, h * dh), lambda bi: (bi, 0, 0)),
        out_shape=jax.ShapeDtypeStruct((b, s, h * dh), BF16),
        scratch_shapes=_attention_scratch(2 * h, s // t, t),
        compiler_params=_params("parallel"),
        name="diff",
    )(table, dq, dk, dv_t, bias_near, lam_params, subln_g.reshape(dh, 1))


def _compress_kernel(kc_ref, vc_ref, pe_ref, w1_ref, w2k_ref, w2vt_ref, ko_ref, vo_ref):
    half = NSA_CMP_STRIDE * HEAD_DIM

    def hidden(t, c_ref):
        c = c_ref[0, 0]
        top = (c + pe_ref[2 * t:2 * t + 1]).astype(BF16)
        bot = (c + pe_ref[2 * t + 1:2 * t + 2]).astype(BF16)
        a = jnp.dot(top, w1_ref[t, :half], preferred_element_type=F32)
        bm = jnp.dot(bot, w1_ref[t, half:], preferred_element_type=F32)
        hid = a + pltpu.roll(bm, bm.shape[0] - 1, 0)
        return jax.nn.gelu(hid).astype(BF16)

    ko_ref[0, 0] = jnp.dot(hidden(0, kc_ref), w2k_ref[...], preferred_element_type=F32)
    vo_ref[0, 0] = lax.dot_general(w2vt_ref[...], hidden(1, vc_ref), NT_DIMS, preferred_element_type=F32)


def _compress(kc, vc, pe4, w1, w2k, w2v_t):
    b, g, s, dh = kc.shape
    n_chunk = s // NSA_CMP_STRIDE
    width = NSA_CMP_STRIDE * dh
    kc = kc.reshape(b, g, n_chunk, width)
    vc = vc.reshape(b, g, n_chunk, width)
    spec_in = pl.BlockSpec((1, 1, n_chunk, width), lambda bi, gi: (bi, gi, 0, 0))
    return pl.pallas_call(
        _compress_kernel,
        grid=(b, g),
        in_specs=[
            spec_in, spec_in,
            pl.BlockSpec(pe4.shape, lambda bi, gi: (0, 0)),
            pl.BlockSpec(w1.shape, lambda bi, gi: (0, 0, 0)),
            pl.BlockSpec(w2k.shape, lambda bi, gi: (0, 0)),
            pl.BlockSpec(w2v_t.shape, lambda bi, gi: (0, 0)),
        ],
        out_specs=[pl.BlockSpec((1, 1, n_chunk, dh), lambda bi, gi: (bi, gi, 0, 0)),
                   pl.BlockSpec((1, 1, dh, n_chunk), lambda bi, gi: (bi, gi, 0, 0))],
        out_shape=[jax.ShapeDtypeStruct((b, g, n_chunk, dh), F32), jax.ShapeDtypeStruct((b, g, dh, n_chunk), F32)],
        compiler_params=_params("parallel", "parallel"),
        name="nsa_compress",
    )(kc, vc, pe4, w1, w2k, w2v_t)


def _nsa_kernel(slc_tab_ref, win_tab_ref, q_ref, kcmp_ref, vcmpt_ref, ks_ref, vst_ref, kw_ref, vwt_ref, gate_ref,
                bias_ref, overlap_ref, o_ref, qx_ref, ocmp_ref, m_ref, acc_ref, m2_ref, acc2_ref,
                s0_ref, s1_ref, p0_ref, p1_ref, *, slc_kinds, win_kinds):
    t = ATT_TILE
    heads = range(NSA_GROUP)
    s = q_ref.shape[2]
    n_chunk = kcmp_ref.shape[2]
    n_slc = s // NSA_SLC_BLOCK
    ks_g, vst_g = ks_ref.at[0, 0], vst_ref.at[0]
    kw_g, vwt_g = kw_ref.at[0, 0], vwt_ref.at[0]
    k_cmp = kcmp_ref[0, 0].astype(BF16)
    v_cmp_t = vcmpt_ref[0, 0].astype(BF16)
    cmp_id = lax.broadcasted_iota(jnp.int32, (n_chunk, t), 0)
    blk = lax.broadcasted_iota(jnp.int32, (n_slc, t), 0)

    def select(qi, carry):
        qs = [_rows(q_ref.at[0, r], qi) for r in heads]
        q_pos = qi * t + lax.broadcasted_iota(jnp.int32, (n_chunk, t), 1)
        cmp_valid = (cmp_id * NSA_CMP_STRIDE + (NSA_CMP_BLOCK - 1) <= q_pos) & (cmp_id < n_chunk - 1)
        p_sum = jnp.zeros((n_chunk, t), F32)
        for r in heads:
            sc = lax.dot_general(k_cmp, qs[r], NT_DIMS, preferred_element_type=F32)
            m = jnp.max(jnp.where(cmp_valid, sc, NEG_INF), axis=0, keepdims=True)
            e = jnp.where(cmp_valid, jnp.exp(sc - m), 0.0)
            l = jnp.sum(e, axis=0, keepdims=True)
            p = e / jnp.where(l > 0.0, l, 1.0)
            p_sum = p_sum + p
            ocmp_ref[r, qi] = jnp.dot(v_cmp_t, p.astype(BF16), preferred_element_type=F32)
        ps_hi = p_sum.astype(BF16)
        ps_lo = (p_sum - ps_hi.astype(F32)).astype(BF16)
        imp = (jnp.dot(overlap_ref[...], ps_hi, preferred_element_type=F32)
               + jnp.dot(overlap_ref[...], ps_lo, preferred_element_type=F32))
        cur = (qi * t + lax.broadcasted_iota(jnp.int32, (n_slc, t), 1)) // NSA_SLC_BLOCK
        forced = (blk == 0) | (blk > cur - NSA_LOCAL_BLOCKS)
        score = jnp.where(blk <= cur, jnp.where(forced, FORCE_SCORE, imp), NEG_INF)
        sel = _topk_rows(score, n_slc, min(NSA_SLC_TOPK, n_slc))
        rows = pl.ds(pl.multiple_of(qi * t, t), t)
        for r in heads:
            qx_ref[r, rows, :] = _extend_query(qs[r], sel)
        return carry

    n_qt = s // t
    lax.fori_loop(0, n_qt // 2, lambda u, carry: select(2 * u + 1, select(2 * u, carry)), 0)
    if n_qt % 2:
        select(n_qt - 1, 0)

    def adds(r, kind):
        return [] if _ADD_TILE[kind] is None else [bias_ref[0, r, _ADD_TILE[kind]]]

    def slc_inputs(qi, kj, kind):
        return [(_rows(ks_g, kj), _rows(qx_ref.at[r], qi), adds(r, kind)) for r in heads]

    def win_inputs(qi, kj, kind):
        return [(_rows(kw_g, kj), _rows(q_ref.at[0, r], qi), adds(r, kind)) for r in heads]

    bufs = ((s0_ref, s1_ref), (p0_ref, p1_ref))
    _attend_tiles(slc_kinds, slc_tab_ref, slc_inputs, lambda kj: [_cols(vst_g, kj)] * len(heads), m_ref, acc_ref, *bufs)
    _attend_tiles(win_kinds, win_tab_ref, win_inputs, lambda kj: [_cols(vwt_g, kj)] * len(heads), m2_ref, acc2_ref, *bufs)

    def finish(qi, carry):
        cols = pl.ds(pl.multiple_of(qi * t, t), t)
        gates = jax.nn.sigmoid(gate_ref[0, :, cols])
        outs = []
        for r in heads:
            g_cmp, g_slc, g_win = (gates[3 * r + br:3 * r + br + 1, :] for br in range(3))
            outs.append(g_cmp * ocmp_ref[r, qi] + g_slc * _normalized(acc_ref[r, qi]) + g_win * _normalized(acc2_ref[r, qi]))
        o_ref[0, cols, :] = jnp.concatenate(outs, axis=0).T.astype(o_ref.dtype)
        return carry

    lax.fori_loop(0, s // t, finish, 0)


def _nsa(nq, k_cmp, v_cmp_t, ks, vs_t, kw, vw_t, gate_t, bias_tiles, overlap_t):
    b, _, s, dh = nq.shape
    g, r, t = NSA_KV_HEADS, NSA_GROUP, ATT_TILE
    assert NSA_WINDOW == 2 * t and s % t == 0 and t % NSA_SLC_BLOCK == 0
    slc_kinds, slc_table = _tile_schedule(s // t)
    win_kinds, win_table = _tile_schedule(s // t, window=NSA_WINDOW // t)
    k_spec = lambda k: pl.BlockSpec((1, 1, s, k.shape[3]), lambda bi, gi: (bi, gi, 0, 0))
    vt_spec = pl.BlockSpec((1, dh, s), lambda bi, gi: (bi, gi, 0))
    scratch = _attention_scratch(r, s // t, t)
    return pl.pallas_call(
        functools.partial(_nsa_kernel, slc_kinds=slc_kinds, win_kinds=win_kinds),
        grid=(b, g),
        in_specs=[
            SMEM_SPEC, SMEM_SPEC,
            pl.BlockSpec((1, r, s, dh), lambda bi, gi: (bi, gi, 0, 0)),
            pl.BlockSpec((1, 1) + k_cmp.shape[2:], lambda bi, gi: (bi, gi, 0, 0)),
            pl.BlockSpec((1, 1) + v_cmp_t.shape[2:], lambda bi, gi: (bi, gi, 0, 0)),
            k_spec(ks), vt_spec, k_spec(kw), vt_spec,
            pl.BlockSpec((1, NSA_GATE_ROWS, s), lambda bi, gi: (bi, gi, 0)),
            pl.BlockSpec((1,) + bias_tiles.shape[1:], lambda bi, gi: (gi, 0, 0, 0, 0)),
            pl.BlockSpec(overlap_t.shape, lambda bi, gi: (0, 0)),
        ],
        out_specs=pl.BlockSpec((1, s, r * dh), lambda bi, gi: (bi, 0, gi)),
        out_shape=jax.ShapeDtypeStruct((b, s, g * r * dh), BF16),
        scratch_shapes=[pltpu.VMEM((r, s, ks.shape[3]), BF16), pltpu.VMEM((r, s // t, dh, t), F32)]
                       + scratch[:2] + scratch,
        compiler_params=_params("parallel", "parallel"),
        name="nsa",
    )(slc_table, win_table, nq, k_cmp, v_cmp_t, ks, vs_t, kw, vw_t, gate_t, bias_tiles, overlap_t)


def _rel_bucket(dist):
    n = jnp.maximum(dist, 0)
    max_exact = REL_BUCKETS // 2
    n_f = jnp.maximum(n, max_exact).astype(F32)
    large = max_exact + (jnp.log(n_f / max_exact) / math.log(REL_MAX_DIST / max_exact)
                         * (REL_BUCKETS - max_exact)).astype(jnp.int32)
    return jnp.where(n < max_exact, n, jnp.minimum(large, REL_BUCKETS - 1))


def _bias_tiles(rel_bias):
    t = ATT_TILE
    assert t >= REL_MAX_DIST
    heads = rel_bias.shape[1]
    width = 2 * t + 1
    by_dist = rel_bias[_rel_bucket(jnp.arange(width))] - rel_bias[REL_BUCKETS - 1]
    skew = jnp.broadcast_to(by_dist.T[:, None, :], (heads, t, width)).reshape(heads, t * width)
    skew = skew[:, :t * (width - 1)].reshape(heads, t, width - 1)
    causal = jnp.arange(t)[:, None] <= jnp.arange(t)[None, :]
    own = jnp.where(causal, skew[:, :, :t], NEG_INF)
    return jnp.stack([own, skew[:, :, t:]], axis=1)


def _const_tables(s):
    n_moba = s // MOBA_BLOCK
    avg = np.zeros((2 * SUBLANES, s), np.float32)
    for j in range(n_moba):
        avg[j, j * MOBA_BLOCK:(j + 1) * MOBA_BLOCK] = 1.0 / MOBA_BLOCK
    n_cmp = (s - NSA_CMP_BLOCK) // NSA_CMP_STRIDE + 1
    n_slc = s // NSA_SLC_BLOCK
    cmp_start = np.arange(n_cmp) * NSA_CMP_STRIDE
    slc_start = np.arange(n_slc) * NSA_SLC_BLOCK
    ov = np.clip(np.minimum(cmp_start[:, None] + NSA_CMP_BLOCK, slc_start[None, :] + NSA_SLC_BLOCK)
                 - np.maximum(cmp_start[:, None], slc_start[None, :]), 0, None) / NSA_CMP_BLOCK
    overlap_t = np.zeros((n_slc, s // NSA_CMP_STRIDE), np.float32)
    overlap_t[:, :n_cmp] = ov.T
    t = ATT_TILE
    win_far = np.where(np.arange(t)[:, None] > np.arange(t)[None, :], 0.0, NEG_INF).astype(np.float32)
    member = np.zeros((len(_MEMBER_BLOCKS), s, HEAD_DIM), np.float32)
    for kind, block in enumerate(_MEMBER_BLOCKS):
        assert s // block <= HEAD_DIM
        member[kind, np.arange(s), np.arange(s) // block] = 1.0
    return jnp.asarray(avg, BF16), jnp.asarray(overlap_t, BF16), jnp.asarray(win_far), jnp.asarray(member, BF16)


def kernel(x, rel_bias, norm_ffn1, ffn1_gate, ffn1_up, ffn1_down, norm_mix, w_in, diff_lambda, diff_subln, nsa_cmp_pe, nsa_cmp_w1, nsa_cmp_w2, w_out, norm_ffn2, ffn2_gate, ffn2_up, ffn2_down, final_norm):
    b, s, d = x.shape
    depth = w_in.shape[0]
    assert d == D_MODEL
    h0, h1 = MOBA_HEADS, MOBA_HEADS + DIFF_HEADS
    near = _bias_tiles(rel_bias.astype(F32))
    avg, overlap_t, win_far, member = _const_tables(s)
    moba_bias = near[:h0]
    diff_bias = near[h0:h1] / (DIFF_QK_DIM ** -0.5)
    nsa_bias = jnp.concatenate([near[h1:], jnp.broadcast_to(win_far, near[h1:, :1].shape)], axis=1)
    nsa_bias = nsa_bias.reshape((NSA_KV_HEADS, NSA_GROUP) + nsa_bias.shape[1:])
    bf = lambda a: a.astype(BF16)

    x = x.reshape(b * s, d)
    flat = lambda a: a.reshape(b * s, a.shape[-1])
    for l in range(depth):
        lambda_init = 0.8 - 0.6 * math.exp(-0.3 * l)
        x = _ffn(x, norm_ffn1[l], bf(ffn1_gate[l]), bf(ffn1_up[l]), bf(ffn1_down[l]))

        p = _inproj(x.reshape(b, s, d), norm_mix[l], *_inproj_weights(w_in[l]), member)
        o_moba = _moba(p["mq"], p["mk"], p["mvT"], moba_bias, avg)
        o_diff = _diff(p["dq"], p["dk"], p["dvT"], diff_bias, diff_lambda[l].astype(F32), diff_subln[l].astype(F32),
                       lambda_init)
        pe4 = nsa_cmp_pe[l].astype(F32).reshape(4, NSA_CMP_STRIDE * HEAD_DIM)
        k_cmp, v_cmp_t = _compress(p["kc"], p["vc"], pe4, bf(nsa_cmp_w1[l]), bf(nsa_cmp_w2[l, 0]), bf(nsa_cmp_w2[l, 1].T))
        o_nsa = _nsa(p["nq"], k_cmp, v_cmp_t, p["ks"], p["vsT"], p["kw"], p["vwT"], p["gateT"], nsa_bias, overlap_t)
        x = _ffn(x, norm_ffn2[l], bf(ffn2_gate[l]), bf(ffn2_up[l]), bf(ffn2_down[l]),
                 mix=(flat(o_moba), flat(o_diff), flat(o_nsa)), w_mix=bf(w_out[l]),
                 post_g=final_norm if l == depth - 1 else None)
    return x.reshape(b, s, d)
```

```python
import functools
import math

import numpy as np
import jax
import jax.numpy as jnp
from jax import lax
from jax.experimental import pallas as pl
from jax.experimental.pallas import tpu as pltpu

F32 = jnp.float32
BF16 = jnp.bfloat16

D_MODEL = 1024
HEAD_DIM = 64
MOBA_HEADS = 4
MOBA_BLOCK = 256
MOBA_TOPK = 3
DIFF_HEADS = 4
DIFF_QK_DIM = HEAD_DIM // 2
NSA_HEADS = 8
NSA_KV_HEADS = 2
NSA_GROUP = NSA_HEADS // NSA_KV_HEADS
NSA_CMP_BLOCK = 32
NSA_CMP_STRIDE = 16
NSA_CMP_HIDDEN = 256
NSA_SLC_BLOCK = 64
NSA_SLC_TOPK = 16
NSA_LOCAL_BLOCKS = 2
NSA_WINDOW = 512
REL_BUCKETS = 32
REL_MAX_DIST = 128
D_FF = 2816
RMS_EPS = 1e-6
NEG_INF = -1e30
FORCE_SCORE = 1e4

NSA_GATE_W = 3 * NSA_HEADS
NSA_GATE_ROWS = 16
LANES = 128
SUBLANES = 8
ATT_TILE = 256
VMEM_LIMIT = 48 * 1024 * 1024
LOG2E = math.log2(math.e)

NT_DIMS = (((1,), (1,)), ((), ()))


def _rms(x, g):
    return x * lax.rsqrt(jnp.mean(x * x, axis=-1, keepdims=True) + RMS_EPS) * g


def _params(*sem):
    return pltpu.CompilerParams(dimension_semantics=sem, vmem_limit_bytes=VMEM_LIMIT)


def _cast_kernel(x_ref, o_ref):
    o_ref[...] = x_ref[...].astype(o_ref.dtype)


def _to_bf16(w, *, row_blocks=4):
    layers, rows, cols = w.shape
    br = rows // row_blocks
    assert rows % row_blocks == 0 and br % (2 * SUBLANES) == 0
    spec = pl.BlockSpec((1, br, cols), lambda l, r: (l, r, 0))
    return pl.pallas_call(
        _cast_kernel,
        grid=(layers, row_blocks),
        in_specs=[spec],
        out_specs=spec,
        out_shape=jax.ShapeDtypeStruct(w.shape, BF16),
        compiler_params=_params("parallel", "parallel"),
        name="to_bf16",
    )(w)


def _ffn_kernel(*refs, tf, n_mix, post_norm):
    x_ref, refs = refs[0], refs[1:]
    mix_refs, refs = refs[:n_mix], refs[n_mix:]
    if n_mix:
        wmix_ref, refs = refs[0], refs[1:]
    g_ref, wg_ref, wu_ref, wd_ref, refs = refs[0], refs[1], refs[2], refs[3], refs[4:]
    if post_norm:
        post_ref, refs = refs[0], refs[1:]
    o_ref, h_ref = refs

    y = x_ref[...]
    row = 0
    for m_ref in mix_refs:
        y = y + jnp.dot(m_ref[...], wmix_ref[row:row + m_ref.shape[1]], preferred_element_type=F32)
        row += m_ref.shape[1]
    o_ref[...] = y
    xn = _rms(y, g_ref[...]).astype(BF16)
    for f in range(h_ref.shape[1] // tf):
        cols = slice(f * tf, (f + 1) * tf)
        gate = jnp.dot(xn, wg_ref[:, cols], preferred_element_type=F32)
        up = jnp.dot(xn, wu_ref[:, cols], preferred_element_type=F32)
        h_ref[:, cols] = (gate * jax.nn.sigmoid(gate) * up).astype(BF16)
    o = o_ref[...] + 0.5 * jnp.dot(h_ref[...], wd_ref[...], preferred_element_type=F32)
    o_ref[...] = _rms(o, post_ref[...]) if post_norm else o


def _ffn(x2, layer, g, wg, wu, wd, mix=(), w_mix=None, post_g=None, *, tm=1024, tf=256):
    n, d = x2.shape
    ff = wg.shape[2]
    resident = dict(pipeline_mode=pl.Buffered(1))
    row_spec = lambda a: pl.BlockSpec((tm, a.shape[1]), lambda i: (i, 0))
    whole = lambda a: pl.BlockSpec((None,) + a.shape[1:], lambda i: (layer, 0, 0), **resident)
    gain = lambda a: a.reshape(1, d)
    args = [x2, *mix] + ([w_mix] if mix else []) + [gain(g), wg, wu, wd] + ([gain(post_g)] if post_g is not None else [])
    specs = ([row_spec(x2)] + [row_spec(m) for m in mix] + ([whole(w_mix)] if mix else [])
             + [pl.BlockSpec((1, d), lambda i: (0, 0)), whole(wg), whole(wu), whole(wd)]
             + ([pl.BlockSpec((1, d), lambda i: (0, 0))] if post_g is not None else []))
    return pl.pallas_call(
        functools.partial(_ffn_kernel, tf=tf, n_mix=len(mix), post_norm=post_g is not None),
        grid=(n // tm,),
        in_specs=specs,
        out_specs=row_spec(x2),
        out_shape=jax.ShapeDtypeStruct((n, d), F32),
        scratch_shapes=[pltpu.VMEM((tm, ff), BF16)],
        compiler_params=_params("parallel"),
        name="ffn",
    )(*args)


_ROW_OUTS = (
    ("mq", 0, 4, BF16, HEAD_DIM ** -0.5, None),
    ("mk", 256, 4, BF16, 1.0, 0),
    ("dq", 768, 4, BF16, 1.0, None),
    ("dk", 1024, 4, BF16, 1.0, None),
    ("nq", 1536, 8, BF16, HEAD_DIM ** -0.5, None),
    ("kc", 2048, 2, F32, 1.0, None),
    ("vc", 2176, 2, F32, 1.0, None),
    ("ks", 2304, 2, BF16, 1.0, 1),
    ("kw", 2560, 2, BF16, 1.0, None),
)
_MEMBER_BLOCKS = (MOBA_BLOCK, NSA_SLC_BLOCK)
_COL_OUTS = (
    ("mvT", 512, 256, BF16),
    ("dvT", 1280, 256, BF16),
    ("vsT", 2432, 128, BF16),
    ("vwT", 2688, 128, BF16),
    ("gateT", None, NSA_KV_HEADS * NSA_GATE_ROWS, F32),
)


def _inproj_kernel(x_ref, g_ref, w_ref, wt_ref, member_ref, *out_refs):
    xn = _rms(x_ref[0], g_ref[...]).astype(BF16)
    col = 0
    for (_, _, heads, dtype, scale, member), o_ref in zip(_ROW_OUTS, out_refs):
        width = heads * HEAD_DIM
        p = jnp.dot(xn, w_ref[:, col:col + width], preferred_element_type=F32)
        if scale != 1.0:
            p = p * scale
        for h in range(heads):
            head = p[:, h * HEAD_DIM:(h + 1) * HEAD_DIM].astype(dtype)
            o_ref[0, h] = head if member is None else jnp.concatenate([head, member_ref[member]], axis=1)
        col += width
    pt = lax.dot_general(wt_ref[...], xn, NT_DIMS, preferred_element_type=F32)
    row = 0
    for (_, _, rows, dtype), o_ref in zip(_COL_OUTS, out_refs[len(_ROW_OUTS):]):
        o_ref[0] = pt[row:row + rows].astype(dtype)
        row += rows


def _inproj_weights(w_l):
    w_rows = jnp.concatenate([w_l[:, c:c + h * HEAD_DIM] for _, c, h, _, _, _ in _ROW_OUTS], axis=1)
    gate_cols = w_l[:, w_l.shape[1] - NSA_GATE_W:]
    per_group = 3 * NSA_GROUP
    gate_t = jnp.zeros((NSA_KV_HEADS * NSA_GATE_ROWS, w_l.shape[0]), w_l.dtype)
    for g in range(NSA_KV_HEADS):
        gate_t = gate_t.at[g * NSA_GATE_ROWS:g * NSA_GATE_ROWS + per_group].set(
            gate_cols[:, g * per_group:(g + 1) * per_group].T)
    w_cols = jnp.concatenate([w_l[:, c:c + r].T for _, c, r, _ in _COL_OUTS[:-1]] + [gate_t], axis=0)
    return w_rows.astype(BF16), w_cols.astype(BF16)


def _inproj(x, g, w_rows, w_cols, member, *, ts=512):
    b, s, d = x.shape
    widths = [HEAD_DIM if member is None else 2 * HEAD_DIM for _, _, _, _, _, member in _ROW_OUTS]
    out_shape = [jax.ShapeDtypeStruct((b, o[2], s, w), o[3]) for o, w in zip(_ROW_OUTS, widths)]
    out_specs = [pl.BlockSpec((1, o[2], ts, w), lambda bi, i: (bi, 0, i, 0)) for o, w in zip(_ROW_OUTS, widths)]
    out_shape += [jax.ShapeDtypeStruct((b, rows, s), dtype) for _, _, rows, dtype in _COL_OUTS]
    out_specs += [pl.BlockSpec((1, rows, ts), lambda bi, i: (bi, 0, i)) for _, _, rows, _ in _COL_OUTS]
    outs = pl.pallas_call(
        _inproj_kernel,
        grid=(b, s // ts),
        in_specs=[
            pl.BlockSpec((1, ts, d), lambda bi, i: (bi, i, 0)),
            pl.BlockSpec((1, d), lambda bi, i: (0, 0)),
            pl.BlockSpec(w_rows.shape, lambda bi, i: (0, 0)),
            pl.BlockSpec(w_cols.shape, lambda bi, i: (0, 0)),
            pl.BlockSpec((member.shape[0], ts, HEAD_DIM), lambda bi, i: (0, i, 0)),
        ],
        out_specs=out_specs,
        out_shape=out_shape,
        compiler_params=_params("parallel", "parallel"),
        name="inproj",
    )(x, g.reshape(1, d), w_rows, w_cols, member)
    names = [o[0] for o in _ROW_OUTS] + [o[0] for o in _COL_OUTS]
    return dict(zip(names, outs))


ONES_ROWS = 16
EXP_ROWS = 32
TILES_PER_TRIP = 2
_ADD_TILE = {"own": 0, "near": 1, "edge": 2, "far": None}


def _tile_schedule(n_qt, window=None):
    tiles = [("own", i, i) for i in range(n_qt)] + [("near", i, i - 1) for i in range(1, n_qt)]
    if window is None:
        tiles += [("far", i, j) for j in range(n_qt - 2) for i in range(j + 2, n_qt)]
    else:
        assert window == 2
        tiles += [("edge", i, i - window) for i in range(window, n_qt)]
    kinds = tuple(kind for kind, _, _ in tiles)
    return kinds, jnp.asarray(np.array([[i for _, i, _ in tiles], [j for _, _, j in tiles]], np.int32))


def _attend_tiles(kinds, tab_ref, score_inputs, value_tiles, m_ref, acc_ref, s_refs, p_refs, c=LOG2E):
    tq = m_ref.shape[-1]

    def scores(t, kind, slot):
        qi, kj = tab_ref[0, t], tab_ref[1, t]
        rows = pl.ds(pl.multiple_of(qi * SUBLANES, SUBLANES), SUBLANES)
        pend = []
        for ci, (k, q, adds) in enumerate(score_inputs(qi, kj, kind)):
            s = lax.dot_general(k, q, NT_DIMS, preferred_element_type=F32)
            for a in adds:
                s = s + a
            s_refs[slot][ci] = s
            m_old = m_ref[ci, rows, :][0:1, :]
            m_new = jnp.maximum(m_old, jnp.max(s, axis=0, keepdims=True))
            m_ref[ci, rows, :] = jnp.broadcast_to(m_new, (SUBLANES, tq))
            pend.append((m_new, jnp.exp2((m_old - m_new) * c)))
        return qi, kj, tuple(pend)

    def accumulate(pending, slot):
        qi, kj, pend = pending
        s_ref, p_ref = s_refs[slot], p_refs[slot]
        for ci, (v_t, (m_new, alpha)) in enumerate(zip(value_tiles(kj), pend)):
            tk = v_t.shape[1]
            for r in range(tk // EXP_ROWS):
                rows = slice(r * EXP_ROWS, (r + 1) * EXP_ROWS)
                p_ref[ci, rows, :] = jnp.exp2(((s_ref[ci, rows, :] - m_new) * c).astype(BF16))
            v_ext = jnp.concatenate([v_t, jnp.ones((ONES_ROWS, tk), BF16)], axis=0)
            acc_ref[ci, qi] = alpha * acc_ref[ci, qi] + jnp.dot(v_ext, p_ref[ci], preferred_element_type=F32)

    m_ref[...] = jnp.full(m_ref.shape, NEG_INF, F32)
    acc_ref[...] = jnp.zeros(acc_ref.shape, F32)
    n = len(kinds)
    pending = scores(0, kinds[0], 0)
    step = 0
    while step < n - 1:
        kind = kinds[step + 1]
        run = 1
        while step + run < n - 1 and kinds[step + run + 1] == kind:
            run += 1

        def group(u, pending, step=step, kind=kind):
            for d in range(TILES_PER_TRIP):
                nxt = scores(step + TILES_PER_TRIP * u + d + 1, kind, (step + d + 1) % 2)
                accumulate(pending, (step + d) % 2)
                pending = nxt
            return pending

        if run // TILES_PER_TRIP:
            pending = lax.fori_loop(0, run // TILES_PER_TRIP, group, pending)
            step += run - run % TILES_PER_TRIP
        for _ in range(run % TILES_PER_TRIP):
            nxt = scores(step + 1, kind, (step + 1) % 2)
            accumulate(pending, step % 2)
            pending = nxt
            step += 1
    accumulate(pending, (n - 1) % 2)


def _rows(ref, j, t=ATT_TILE):
    return ref[pl.ds(pl.multiple_of(j * t, t), t), :]


def _cols(ref, j, t=ATT_TILE):
    return ref[:, pl.ds(pl.multiple_of(j * t, t), t)]


def _normalized(acc, dv=HEAD_DIM):
    return acc[:dv] / acc[dv:dv + 1]


def _topk_rows(score, n_rows, topk):
    row_id = lax.broadcasted_iota(jnp.int32, score.shape, 0)
    rank = jnp.zeros(score.shape, F32)
    for c in range(n_rows):
        other = score[c:c + 1, :]
        beats = (other > score) | ((other == score) & (row_id > c))
        rank = rank + jnp.where(beats, 1.0, 0.0)
    return (rank < topk) & (row_id < n_rows)


def _extend_query(q, chosen):
    tq, dh = q.shape
    pen = jnp.where(chosen, 0.0, NEG_INF)
    pen = jnp.concatenate([pen, jnp.zeros((LANES - pen.shape[0], tq), F32)], axis=0)
    return jnp.concatenate([q, pen.T[:, :dh].astype(q.dtype)], axis=1)


def _attention_scratch(n, n_qt, t, dv=HEAD_DIM):
    return [pltpu.VMEM((n, n_qt * SUBLANES, t), F32), pltpu.VMEM((n, n_qt, dv + ONES_ROWS, t), F32),
            pltpu.VMEM((n, t, t), F32), pltpu.VMEM((n, t, t), F32),
            pltpu.VMEM((n, t, t), BF16), pltpu.VMEM((n, t, t), BF16)]


SMEM_SPEC = pl.BlockSpec(memory_space=pltpu.SMEM)


def _moba_kernel(tab_ref, q_ref, k_ref, vt_ref, bias_ref, avg_ref, o_ref, qx_ref, m_ref, acc_ref,
                 s0_ref, s1_ref, p0_ref, p1_ref, *, kinds):
    t = ATT_TILE
    s = q_ref.shape[2]
    n_blk = s // MOBA_BLOCK
    heads = range(MOBA_HEADS)
    blk_rows = -(-n_blk // SUBLANES) * SUBLANES
    blk = lax.broadcasted_iota(jnp.int32, (blk_rows, s), 0)
    own = lax.broadcasted_iota(jnp.int32, (blk_rows, s), 1) // MOBA_BLOCK

    for h in heads:
        k_mean = jnp.dot(avg_ref[...], k_ref[0, h], preferred_element_type=F32)[:, :HEAD_DIM]
        km_hi = k_mean.astype(BF16)
        km_lo = (k_mean - km_hi.astype(F32)).astype(BF16)
        q = q_ref[0, h]
        gate = (lax.dot_general(km_hi, q, NT_DIMS, preferred_element_type=F32)
                + lax.dot_general(km_lo, q, NT_DIMS, preferred_element_type=F32))
        gate = jnp.where(blk < own, gate[:blk_rows], NEG_INF)
        chosen = (_topk_rows(gate, n_blk, MOBA_TOPK) & (blk < own)) | (blk == own)
        for qi in range(s // t):
            cols = slice(qi * t, (qi + 1) * t)
            qx_ref[h, cols, :] = _extend_query(q_ref[0, h, cols, :], chosen[:, cols])

    def score_inputs(qi, kj, kind):
        tile = _ADD_TILE[kind]
        return [(_rows(k_ref.at[0, h], kj), _rows(qx_ref.at[h], qi), [] if tile is None else [bias_ref[h, tile]])
                for h in heads]

    def value_tiles(kj):
        return [_cols(vt_ref.at[0, h * HEAD_DIM:(h + 1) * HEAD_DIM], kj) for h in heads]

    _attend_tiles(kinds, tab_ref, score_inputs, value_tiles, m_ref, acc_ref, (s0_ref, s1_ref), (p0_ref, p1_ref))

    def finish(qi, carry):
        o_t = jnp.concatenate([_normalized(acc_ref[h, qi]) for h in heads], axis=0)
        o_ref[0, pl.ds(pl.multiple_of(qi * t, t), t), :] = o_t.T.astype(o_ref.dtype)
        return carry

    lax.fori_loop(0, s // t, finish, 0)


def _moba(mq, mk, mv_t, bias_near, avg):
    b, h, s, dh = mq.shape
    t = ATT_TILE
    assert MOBA_BLOCK == t and s % t == 0 and s // t <= avg.shape[0]
    kinds, table = _tile_schedule(s // t)
    return pl.pallas_call(
        functools.partial(_moba_kernel, kinds=kinds),
        grid=(b,),
        in_specs=[
            SMEM_SPEC,
            pl.BlockSpec((1, h, s, dh), lambda bi: (bi, 0, 0, 0)),
            pl.BlockSpec((1, h, s, mk.shape[3]), lambda bi: (bi, 0, 0, 0)),
            pl.BlockSpec((1, h * dh, s), lambda bi: (bi, 0, 0)),
            pl.BlockSpec(bias_near.shape, lambda bi: (0, 0, 0, 0)),
            pl.BlockSpec(avg.shape, lambda bi: (0, 0)),
        ],
        out_specs=pl.BlockSpec((1, s, h * dh), lambda bi: (bi, 0, 0)),
        out_shape=jax.ShapeDtypeStruct((b, s, h * dh), BF16),
        scratch_shapes=[pltpu.VMEM((h, s, mk.shape[3]), BF16)] + _attention_scratch(h, s // t, t),
        compiler_params=_params("parallel"),
        name="moba",
    )(table, mq, mk, mv_t, bias_near, avg)


def _diff_kernel(tab_ref, q_ref, k_ref, vt_ref, bias_ref, lam_ref, subln_ref, o_ref, m_ref, acc_ref,
                 s0_ref, s1_ref, p0_ref, p1_ref, *, kinds, lambda_init):
    t = ATT_TILE
    heads = range(DIFF_HEADS)
    lane = lax.broadcasted_iota(jnp.int32, (t, HEAD_DIM), 1)
    c = DIFF_QK_DIM ** -0.5 * LOG2E
    lp = lam_ref[...]
    lam = (jnp.exp(jnp.sum(lp[0:1] * lp[1:2], axis=-1, keepdims=True))
           - jnp.exp(jnp.sum(lp[2:3] * lp[3:4], axis=-1, keepdims=True)) + lambda_init)

    def score_inputs(qi, kj, kind):
        tile = _ADD_TILE[kind]
        out = []
        for h in heads:
            k = _rows(k_ref.at[0, h], kj)
            q = _rows(q_ref.at[0, h], qi)
            adds = [] if tile is None else [bias_ref[h, tile]]
            out.append((k, jnp.where(lane < DIFF_QK_DIM, q, jnp.zeros_like(q)), adds))
            out.append((k, jnp.where(lane >= DIFF_QK_DIM, q, jnp.zeros_like(q)), adds))
        return out

    def value_tiles(kj):
        tiles = [_cols(vt_ref.at[0, h * HEAD_DIM:(h + 1) * HEAD_DIM], kj) for h in heads]
        return [tiles[h] for h in heads for _ in range(2)]

    _attend_tiles(kinds, tab_ref, score_inputs, value_tiles, m_ref, acc_ref, (s0_ref, s1_ref), (p0_ref, p1_ref), c=c)

    def finish(qi, carry):
        outs = []
        for h in heads:
            o = _normalized(acc_ref[2 * h, qi]) - lam * _normalized(acc_ref[2 * h + 1, qi])
            o = o * lax.rsqrt(jnp.mean(o * o, axis=0, keepdims=True) + RMS_EPS) * subln_ref[...]
            outs.append(o * (1.0 - lambda_init))
        o_ref[0, pl.ds(pl.multiple_of(qi * t, t), t), :] = jnp.concatenate(outs, axis=0).T.astype(o_ref.dtype)
        return carry

    lax.fori_loop(0, q_ref.shape[2] // t, finish, 0)


def _diff(dq, dk, dv_t, bias_near, lam_params, subln_g, lambda_init):
    b, h, s, dh = dq.shape
    t = ATT_TILE
    kinds, table = _tile_schedule(s // t)
    return pl.pallas_call(
        functools.partial(_diff_kernel, kinds=kinds, lambda_init=lambda_init),
        grid=(b,),
        in_specs=[
            SMEM_SPEC,
            pl.BlockSpec((1, h, s, dh), lambda bi: (bi, 0, 0, 0)),
            pl.BlockSpec((1, h, s, dh), lambda bi: (bi, 0, 0, 0)),
            pl.BlockSpec((1, h * dh, s), lambda bi: (bi, 0, 0)),
            pl.BlockSpec(bias_near.shape, lambda bi: (0, 0, 0, 0)),
            pl.BlockSpec(lam_params.shape, lambda bi: (0, 0)),
            pl.BlockSpec((dh, 1), lambda bi: (0, 0)),
        ],
        out_specs=pl.BlockSpec((1, s, h * dh), lambda bi: (bi, 0, 0)),
        out_shape=jax.ShapeDtypeStruct((b, s, h * dh), BF16),
        scratch_shapes=_attention_scratch(2 * h, s // t, t),
        compiler_params=_params("parallel"),
        name="diff",
    )(table, dq, dk, dv_t, bias_near, lam_params, subln_g.reshape(dh, 1))


def _compress_kernel(kc_ref, vc_ref, pe_ref, w1_ref, w2k_ref, w2vt_ref, ko_ref, vo_ref):
    half = NSA_CMP_STRIDE * HEAD_DIM

    def hidden(t, c_ref):
        c = c_ref[0, 0]
        top = (c + pe_ref[2 * t:2 * t + 1]).astype(BF16)
        bot = (c + pe_ref[2 * t + 1:2 * t + 2]).astype(BF16)
        a = jnp.dot(top, w1_ref[t, :half], preferred_element_type=F32)
        bm = jnp.dot(bot, w1_ref[t, half:], preferred_element_type=F32)
        hid = a + pltpu.roll(bm, bm.shape[0] - 1, 0)
        return jax.nn.gelu(hid).astype(BF16)

    ko_ref[0, 0] = jnp.dot(hidden(0, kc_ref), w2k_ref[...], preferred_element_type=F32)
    vo_ref[0, 0] = lax.dot_general(w2vt_ref[...], hidden(1, vc_ref), NT_DIMS, preferred_element_type=F32)


def _compress(kc, vc, pe4, w1, w2k, w2v_t):
    b, g, s, dh = kc.shape
    n_chunk = s // NSA_CMP_STRIDE
    width = NSA_CMP_STRIDE * dh
    kc = kc.reshape(b, g, n_chunk, width)
    vc = vc.reshape(b, g, n_chunk, width)
    spec_in = pl.BlockSpec((1, 1, n_chunk, width), lambda bi, gi: (bi, gi, 0, 0))
    return pl.pallas_call(
        _compress_kernel,
        grid=(b, g),
        in_specs=[
            spec_in, spec_in,
            pl.BlockSpec(pe4.shape, lambda bi, gi: (0, 0)),
            pl.BlockSpec(w1.shape, lambda bi, gi: (0, 0, 0)),
            pl.BlockSpec(w2k.shape, lambda bi, gi: (0, 0)),
            pl.BlockSpec(w2v_t.shape, lambda bi, gi: (0, 0)),
        ],
        out_specs=[pl.BlockSpec((1, 1, n_chunk, dh), lambda bi, gi: (bi, gi, 0, 0)),
                   pl.BlockSpec((1, 1, dh, n_chunk), lambda bi, gi: (bi, gi, 0, 0))],
        out_shape=[jax.ShapeDtypeStruct((b, g, n_chunk, dh), F32), jax.ShapeDtypeStruct((b, g, dh, n_chunk), F32)],
        compiler_params=_params("parallel", "parallel"),
        name="nsa_compress",
    )(kc, vc, pe4, w1, w2k, w2v_t)


def _nsa_kernel(slc_tab_ref, win_tab_ref, q_ref, kcmp_ref, vcmpt_ref, ks_ref, vst_ref, kw_ref, vwt_ref, gate_ref,
                bias_ref, overlap_ref, o_ref, qx_ref, ocmp_ref, m_ref, acc_ref, m2_ref, acc2_ref,
                s0_ref, s1_ref, p0_ref, p1_ref, *, slc_kinds, win_kinds):
    t = ATT_TILE
    heads = range(NSA_GROUP)
    s = q_ref.shape[2]
    n_chunk = kcmp_ref.shape[2]
    n_slc = s // NSA_SLC_BLOCK
    ks_g, vst_g = ks_ref.at[0, 0], vst_ref.at[0]
    kw_g, vwt_g = kw_ref.at[0, 0], vwt_ref.at[0]
    k_cmp = kcmp_ref[0, 0].astype(BF16)
    v_cmp_t = vcmpt_ref[0, 0].astype(BF16)
    topk = min(NSA_SLC_TOPK, n_slc)
    width = min(topk * NSA_SLC_BLOCK, s)
    assert s % width == 0 and width % t == 0

    def select(q0):
        cmp_id = lax.broadcasted_iota(jnp.int32, (n_chunk, width), 0)
        q_pos = q0 + lax.broadcasted_iota(jnp.int32, (n_chunk, width), 1)
        cmp_valid = (cmp_id * NSA_CMP_STRIDE + (NSA_CMP_BLOCK - 1) <= q_pos) & (cmp_id < n_chunk - 1)
        p_sum = jnp.zeros((n_chunk, width), F32)
        for r in heads:
            sc = lax.dot_general(k_cmp, q_ref[0, r, q0:q0 + width, :], NT_DIMS, preferred_element_type=F32)
            m = jnp.max(jnp.where(cmp_valid, sc, NEG_INF), axis=0, keepdims=True)
            e = jnp.where(cmp_valid, jnp.exp(sc - m), 0.0)
            l = jnp.sum(e, axis=0, keepdims=True)
            p = e / jnp.where(l > 0.0, l, 1.0)
            p_sum = p_sum + p
            o = jnp.dot(v_cmp_t, p.astype(BF16), preferred_element_type=F32)
            for u in range(width // t):
                ocmp_ref[r, q0 // t + u] = o[:, u * t:(u + 1) * t]
        blk = lax.broadcasted_iota(jnp.int32, (n_slc, width), 0)
        cur = (q0 + lax.broadcasted_iota(jnp.int32, (n_slc, width), 1)) // NSA_SLC_BLOCK
        if q0 + width <= topk * NSA_SLC_BLOCK:
            sel = blk <= cur
        else:
            ps_hi = p_sum.astype(BF16)
            ps_lo = (p_sum - ps_hi.astype(F32)).astype(BF16)
            imp = (jnp.dot(overlap_ref[...], ps_hi, preferred_element_type=F32)
                   + jnp.dot(overlap_ref[...], ps_lo, preferred_element_type=F32))
            forced = (blk == 0) | (blk > cur - NSA_LOCAL_BLOCKS)
            score = jnp.where(blk <= cur, jnp.where(forced, FORCE_SCORE, imp), NEG_INF)
            sel = _topk_rows(score, n_slc, topk)
        for u in range(width // t):
            rows = slice(q0 + u * t, q0 + (u + 1) * t)
            for r in heads:
                qx_ref[r, rows, :] = _extend_query(q_ref[0, r, rows, :], sel[:, u * t:(u + 1) * t])

    for q0 in range(0, s, width):
        select(q0)

    def adds(r, kind):
        return [] if _ADD_TILE[kind] is None else [bias_ref[0, r, _ADD_TILE[kind]]]

    def slc_inputs(qi, kj, kind):
        return [(_rows(ks_g, kj), _rows(qx_ref.at[r], qi), adds(r, kind)) for r in heads]

    def win_inputs(qi, kj, kind):
        return [(_rows(kw_g, kj), _rows(q_ref.at[0, r], qi), adds(r, kind)) for r in heads]

    bufs = ((s0_ref, s1_ref), (p0_ref, p1_ref))
    _attend_tiles(slc_kinds, slc_tab_ref, slc_inputs, lambda kj: [_cols(vst_g, kj)] * len(heads), m_ref, acc_ref, *bufs)
    _attend_tiles(win_kinds, win_tab_ref, win_inputs, lambda kj: [_cols(vwt_g, kj)] * len(heads), m2_ref, acc2_ref, *bufs)

    def finish(qi, carry):
        cols = pl.ds(pl.multiple_of(qi * t, t), t)
        gates = jax.nn.sigmoid(gate_ref[0, :, cols])
        outs = []
        for r in heads:
            g_cmp, g_slc, g_win = (gates[3 * r + br:3 * r + br + 1, :] for br in range(3))
            outs.append(g_cmp * ocmp_ref[r, qi] + g_slc * _normalized(acc_ref[r, qi]) + g_win * _normalized(acc2_ref[r, qi]))
        o_ref[0, cols, :] = jnp.concatenate(outs, axis=0).T.astype(o_ref.dtype)
        return carry

    lax.fori_loop(0, s // t, finish, 0)


def _nsa(nq, k_cmp, v_cmp_t, ks, vs_t, kw, vw_t, gate_t, bias_tiles, overlap_t):
    b, _, s, dh = nq.shape
    g, r, t = NSA_KV_HEADS, NSA_GROUP, ATT_TILE
    assert NSA_WINDOW == 2 * t and s % t == 0 and t % NSA_SLC_BLOCK == 0
    slc_kinds, slc_table = _tile_schedule(s // t)
    win_kinds, win_table = _tile_schedule(s // t, window=NSA_WINDOW // t)
    k_spec = lambda k: pl.BlockSpec((1, 1, s, k.shape[3]), lambda bi, gi: (bi, gi, 0, 0))
    vt_spec = pl.BlockSpec((1, dh, s), lambda bi, gi: (bi, gi, 0))
    scratch = _attention_scratch(r, s // t, t)
    return pl.pallas_call(
        functools.partial(_nsa_kernel, slc_kinds=slc_kinds, win_kinds=win_kinds),
        grid=(b, g),
        in_specs=[
            SMEM_SPEC, SMEM_SPEC,
            pl.BlockSpec((1, r, s, dh), lambda bi, gi: (bi, gi, 0, 0)),
            pl.BlockSpec((1, 1) + k_cmp.shape[2:], lambda bi, gi: (bi, gi, 0, 0)),
            pl.BlockSpec((1, 1) + v_cmp_t.shape[2:], lambda bi, gi: (bi, gi, 0, 0)),
            k_spec(ks), vt_spec, k_spec(kw), vt_spec,
            pl.BlockSpec((1, NSA_GATE_ROWS, s), lambda bi, gi: (bi, gi, 0)),
            pl.BlockSpec((1,) + bias_tiles.shape[1:], lambda bi, gi: (gi, 0, 0, 0, 0)),
            pl.BlockSpec(overlap_t.shape, lambda bi, gi: (0, 0)),
        ],
        out_specs=pl.BlockSpec((1, s, r * dh), lambda bi, gi: (bi, 0, gi)),
        out_shape=jax.ShapeDtypeStruct((b, s, g * r * dh), BF16),
        scratch_shapes=[pltpu.VMEM((r, s, ks.shape[3]), BF16), pltpu.VMEM((r, s // t, dh, t), F32)]
                       + scratch[:2] + scratch,
        compiler_params=_params("parallel", "parallel"),
        name="nsa",
    )(slc_table, win_table, nq, k_cmp, v_cmp_t, ks, vs_t, kw, vw_t, gate_t, bias_tiles, overlap_t)


def _rel_bucket(dist):
    n = jnp.maximum(dist, 0)
    max_exact = REL_BUCKETS // 2
    n_f = jnp.maximum(n, max_exact).astype(F32)
    large = max_exact + (jnp.log(n_f / max_exact) / math.log(REL_MAX_DIST / max_exact)
                         * (REL_BUCKETS - max_exact)).astype(jnp.int32)
    return jnp.where(n < max_exact, n, jnp.minimum(large, REL_BUCKETS - 1))


def _bias_tiles(rel_bias):
    t = ATT_TILE
    assert t >= REL_MAX_DIST
    heads = rel_bias.shape[1]
    width = 2 * t + 1
    by_dist = rel_bias[_rel_bucket(jnp.arange(width))] - rel_bias[REL_BUCKETS - 1]
    skew = jnp.broadcast_to(by_dist.T[:, None, :], (heads, t, width)).reshape(heads, t * width)
    skew = skew[:, :t * (width - 1)].reshape(heads, t, width - 1)
    causal = jnp.arange(t)[:, None] <= jnp.arange(t)[None, :]
    own = jnp.where(causal, skew[:, :, :t], NEG_INF)
    return jnp.stack([own, skew[:, :, t:]], axis=1)


def _const_tables(s):
    n_moba = s // MOBA_BLOCK
    avg = np.zeros((2 * SUBLANES, s), np.float32)
    for j in range(n_moba):
        avg[j, j * MOBA_BLOCK:(j + 1) * MOBA_BLOCK] = 1.0 / MOBA_BLOCK
    n_cmp = (s - NSA_CMP_BLOCK) // NSA_CMP_STRIDE + 1
    n_slc = s // NSA_SLC_BLOCK
    cmp_start = np.arange(n_cmp) * NSA_CMP_STRIDE
    slc_start = np.arange(n_slc) * NSA_SLC_BLOCK
    ov = np.clip(np.minimum(cmp_start[:, None] + NSA_CMP_BLOCK, slc_start[None, :] + NSA_SLC_BLOCK)
                 - np.maximum(cmp_start[:, None], slc_start[None, :]), 0, None) / NSA_CMP_BLOCK
    overlap_t = np.zeros((n_slc, s // NSA_CMP_STRIDE), np.float32)
    overlap_t[:, :n_cmp] = ov.T
    t = ATT_TILE
    win_far = np.where(np.arange(t)[:, None] > np.arange(t)[None, :], 0.0, NEG_INF).astype(np.float32)
    member = np.zeros((len(_MEMBER_BLOCKS), s, HEAD_DIM), np.float32)
    for kind, block in enumerate(_MEMBER_BLOCKS):
        assert s // block <= HEAD_DIM
        member[kind, np.arange(s), np.arange(s) // block] = 1.0
    return jnp.asarray(avg, BF16), jnp.asarray(overlap_t, BF16), jnp.asarray(win_far), jnp.asarray(member, BF16)


def kernel(x, rel_bias, norm_ffn1, ffn1_gate, ffn1_up, ffn1_down, norm_mix, w_in, diff_lambda, diff_subln, nsa_cmp_pe, nsa_cmp_w1, nsa_cmp_w2, w_out, norm_ffn2, ffn2_gate, ffn2_up, ffn2_down, final_norm):
    b, s, d = x.shape
    depth = w_in.shape[0]
    assert d == D_MODEL
    h0, h1 = MOBA_HEADS, MOBA_HEADS + DIFF_HEADS
    near = _bias_tiles(rel_bias.astype(F32))
    avg, overlap_t, win_far, member = _const_tables(s)
    moba_bias = near[:h0]
    diff_bias = near[h0:h1] / (DIFF_QK_DIM ** -0.5)
    nsa_bias = jnp.concatenate([near[h1:], jnp.broadcast_to(win_far, near[h1:, :1].shape)], axis=1)
    nsa_bias = nsa_bias.reshape((NSA_KV_HEADS, NSA_GROUP) + nsa_bias.shape[1:])
    bf = lambda a: a.astype(BF16)
    ffn1_gate, ffn1_up, ffn1_down, ffn2_gate, ffn2_up, ffn2_down, w_in, w_out = (
        _to_bf16(w) for w in (ffn1_gate, ffn1_up, ffn1_down, ffn2_gate, ffn2_up, ffn2_down, w_in, w_out))
    cmp_w1 = _to_bf16(nsa_cmp_w1.reshape((-1,) + nsa_cmp_w1.shape[2:])).reshape(nsa_cmp_w1.shape)

    x = x.reshape(b * s, d)
    flat = lambda a: a.reshape(b * s, a.shape[-1])
    for l in range(depth):
        lambda_init = 0.8 - 0.6 * math.exp(-0.3 * l)
        x = _ffn(x, l, norm_ffn1[l], ffn1_gate, ffn1_up, ffn1_down)

        p = _inproj(x.reshape(b, s, d), norm_mix[l], *_inproj_weights(w_in[l]), member)
        o_moba = _moba(p["mq"], p["mk"], p["mvT"], moba_bias, avg)
        o_diff = _diff(p["dq"], p["dk"], p["dvT"], diff_bias, diff_lambda[l].astype(F32), diff_subln[l].astype(F32),
                       lambda_init)
        pe4 = nsa_cmp_pe[l].astype(F32).reshape(4, NSA_CMP_STRIDE * HEAD_DIM)
        k_cmp, v_cmp_t = _compress(p["kc"], p["vc"], pe4, cmp_w1[l], bf(nsa_cmp_w2[l, 0]), bf(nsa_cmp_w2[l, 1].T))
        o_nsa = _nsa(p["nq"], k_cmp, v_cmp_t, p["ks"], p["vsT"], p["kw"], p["vwT"], p["gateT"], nsa_bias, overlap_t)
        x = _ffn(x, l, norm_ffn2[l], ffn2_gate, ffn2_up, ffn2_down,
                 mix=(flat(o_moba), flat(o_diff), flat(o_nsa)), w_mix=w_out,
                 post_g=final_norm if l == depth - 1 else None)
    return x.reshape(b, s, d)
```

```python
import functools
import math

import numpy as np
import jax
import jax.numpy as jnp
from jax import lax
from jax.experimental import pallas as pl
from jax.experimental.pallas import tpu as pltpu

F32 = jnp.float32
BF16 = jnp.bfloat16

D_MODEL = 1024
HEAD_DIM = 64
MOBA_HEADS = 4
MOBA_BLOCK = 256
MOBA_TOPK = 3
DIFF_HEADS = 4
DIFF_QK_DIM = HEAD_DIM // 2
NSA_HEADS = 8
NSA_KV_HEADS = 2
NSA_GROUP = NSA_HEADS // NSA_KV_HEADS
NSA_CMP_BLOCK = 32
NSA_CMP_STRIDE = 16
NSA_CMP_HIDDEN = 256
NSA_SLC_BLOCK = 64
NSA_SLC_TOPK = 16
NSA_LOCAL_BLOCKS = 2
NSA_WINDOW = 512
REL_BUCKETS = 32
REL_MAX_DIST = 128
D_FF = 2816
RMS_EPS = 1e-6
NEG_INF = -1e30
FORCE_SCORE = 1e4

NSA_GATE_W = 3 * NSA_HEADS
NSA_GATE_ROWS = 16
LANES = 128
SUBLANES = 8
ATT_TILE = 256
VMEM_LIMIT = 48 * 1024 * 1024
LOG2E = math.log2(math.e)

NT_DIMS = (((1,), (1,)), ((), ()))


def _rms(x, g):
    return x * lax.rsqrt(jnp.mean(x * x, axis=-1, keepdims=True) + RMS_EPS) * g


def _params(*sem):
    return pltpu.CompilerParams(dimension_semantics=sem, vmem_limit_bytes=VMEM_LIMIT)


def _cast_kernel(x_ref, o_ref):
    o_ref[...] = x_ref[...].astype(o_ref.dtype)


def _to_bf16(w, *, row_blocks=4):
    layers, rows, cols = w.shape
    br = rows // row_blocks
    assert rows % row_blocks == 0 and br % (2 * SUBLANES) == 0
    spec = pl.BlockSpec((1, br, cols), lambda l, r: (l, r, 0))
    return pl.pallas_call(
        _cast_kernel,
        grid=(layers, row_blocks),
        in_specs=[spec],
        out_specs=spec,
        out_shape=jax.ShapeDtypeStruct(w.shape, BF16),
        compiler_params=_params("parallel", "parallel"),
        name="to_bf16",
    )(w)


def _ffn_kernel(*refs, tf, n_mix, post_norm):
    x_ref, refs = refs[0], refs[1:]
    mix_refs, refs = refs[:n_mix], refs[n_mix:]
    if n_mix:
        wmix_ref, refs = refs[0], refs[1:]
    g_ref, wg_ref, wu_ref, wd_ref, refs = refs[0], refs[1], refs[2], refs[3], refs[4:]
    if post_norm:
        post_ref, refs = refs[0], refs[1:]
    o_ref, h_ref = refs

    y = x_ref[...]
    row = 0
    for m_ref in mix_refs:
        y = y + jnp.dot(m_ref[...], wmix_ref[row:row + m_ref.shape[1]], preferred_element_type=F32)
        row += m_ref.shape[1]
    o_ref[...] = y
    xn = _rms(y, g_ref[...]).astype(BF16)
    for f in range(h_ref.shape[1] // tf):
        cols = slice(f * tf, (f + 1) * tf)
        gate = jnp.dot(xn, wg_ref[:, cols], preferred_element_type=F32)
        up = jnp.dot(xn, wu_ref[:, cols], preferred_element_type=F32)
        h_ref[:, cols] = (gate * jax.nn.sigmoid(gate) * up).astype(BF16)
    o = o_ref[...] + 0.5 * jnp.dot(h_ref[...], wd_ref[...], preferred_element_type=F32)
    o_ref[...] = _rms(o, post_ref[...]) if post_norm else o


def _ffn(x2, layer, g, wg, wu, wd, mix=(), w_mix=None, post_g=None, *, tm=1024, tf=256):
    n, d = x2.shape
    ff = wg.shape[2]
    resident = dict(pipeline_mode=pl.Buffered(1))
    row_spec = lambda a: pl.BlockSpec((tm, a.shape[1]), lambda i: (i, 0))
    whole = lambda a: pl.BlockSpec((None,) + a.shape[1:], lambda i: (layer, 0, 0), **resident)
    gain = lambda a: a.reshape(1, d)
    args = [x2, *mix] + ([w_mix] if mix else []) + [gain(g), wg, wu, wd] + ([gain(post_g)] if post_g is not None else [])
    specs = ([row_spec(x2)] + [row_spec(m) for m in mix] + ([whole(w_mix)] if mix else [])
             + [pl.BlockSpec((1, d), lambda i: (0, 0)), whole(wg), whole(wu), whole(wd)]
             + ([pl.BlockSpec((1, d), lambda i: (0, 0))] if post_g is not None else []))
    return pl.pallas_call(
        functools.partial(_ffn_kernel, tf=tf, n_mix=len(mix), post_norm=post_g is not None),
        grid=(n // tm,),
        in_specs=specs,
        out_specs=row_spec(x2),
        out_shape=jax.ShapeDtypeStruct((n, d), F32),
        scratch_shapes=[pltpu.VMEM((tm, ff), BF16)],
        compiler_params=_params("parallel"),
        name="ffn",
    )(*args)


_ROW_OUTS = (
    ("mq", 0, 4, BF16, HEAD_DIM ** -0.5 * LOG2E, None),
    ("mk", 256, 4, BF16, 1.0, 0),
    ("dq", 768, 4, BF16, DIFF_QK_DIM ** -0.5 * LOG2E, None),
    ("dk", 1024, 4, BF16, 1.0, None),
    ("nq", 1536, 8, BF16, HEAD_DIM ** -0.5 * LOG2E, None),
    ("kc", 2048, 2, F32, 1.0, None),
    ("vc", 2176, 2, F32, 1.0, None),
    ("ks", 2304, 2, BF16, 1.0, 1),
    ("kw", 2560, 2, BF16, 1.0, None),
)
_MEMBER_BLOCKS = (MOBA_BLOCK, NSA_SLC_BLOCK)
_COL_OUTS = (
    ("mvT", 512, 256, BF16),
    ("dvT", 1280, 256, BF16),
    ("vsT", 2432, 128, BF16),
    ("vwT", 2688, 128, BF16),
    ("gateT", None, NSA_KV_HEADS * NSA_GATE_ROWS, F32),
)


def _inproj_kernel(x_ref, g_ref, w_ref, wt_ref, member_ref, *out_refs):
    xn = _rms(x_ref[0], g_ref[...]).astype(BF16)
    col = 0
    for (_, _, heads, dtype, scale, member), o_ref in zip(_ROW_OUTS, out_refs):
        width = heads * HEAD_DIM
        p = jnp.dot(xn, w_ref[:, col:col + width], preferred_element_type=F32)
        if scale != 1.0:
            p = p * scale
        for h in range(heads):
            head = p[:, h * HEAD_DIM:(h + 1) * HEAD_DIM].astype(dtype)
            o_ref[0, h] = head if member is None else jnp.concatenate([head, member_ref[member]], axis=1)
        col += width
    pt = lax.dot_general(wt_ref[...], xn, NT_DIMS, preferred_element_type=F32)
    row = 0
    for (_, _, rows, dtype), o_ref in zip(_COL_OUTS, out_refs[len(_ROW_OUTS):]):
        o_ref[0] = pt[row:row + rows].astype(dtype)
        row += rows


def _inproj_weights(w_l):
    w_rows = jnp.concatenate([w_l[:, c:c + h * HEAD_DIM] for _, c, h, _, _, _ in _ROW_OUTS], axis=1)
    gate_cols = w_l[:, w_l.shape[1] - NSA_GATE_W:]
    per_group = 3 * NSA_GROUP
    gate_t = jnp.zeros((NSA_KV_HEADS * NSA_GATE_ROWS, w_l.shape[0]), w_l.dtype)
    for g in range(NSA_KV_HEADS):
        gate_t = gate_t.at[g * NSA_GATE_ROWS:g * NSA_GATE_ROWS + per_group].set(
            gate_cols[:, g * per_group:(g + 1) * per_group].T)
    w_cols = jnp.concatenate([w_l[:, c:c + r].T for _, c, r, _ in _COL_OUTS[:-1]] + [gate_t], axis=0)
    return w_rows.astype(BF16), w_cols.astype(BF16)


def _inproj(x, g, w_rows, w_cols, member, *, ts=512):
    b, s, d = x.shape
    widths = [HEAD_DIM if member is None else 2 * HEAD_DIM for _, _, _, _, _, member in _ROW_OUTS]
    out_shape = [jax.ShapeDtypeStruct((b, o[2], s, w), o[3]) for o, w in zip(_ROW_OUTS, widths)]
    out_specs = [pl.BlockSpec((1, o[2], ts, w), lambda bi, i: (bi, 0, i, 0)) for o, w in zip(_ROW_OUTS, widths)]
    out_shape += [jax.ShapeDtypeStruct((b, rows, s), dtype) for _, _, rows, dtype in _COL_OUTS]
    out_specs += [pl.BlockSpec((1, rows, ts), lambda bi, i: (bi, 0, i)) for _, _, rows, _ in _COL_OUTS]
    outs = pl.pallas_call(
        _inproj_kernel,
        grid=(b, s // ts),
        in_specs=[
            pl.BlockSpec((1, ts, d), lambda bi, i: (bi, i, 0)),
            pl.BlockSpec((1, d), lambda bi, i: (0, 0)),
            pl.BlockSpec(w_rows.shape, lambda bi, i: (0, 0)),
            pl.BlockSpec(w_cols.shape, lambda bi, i: (0, 0)),
            pl.BlockSpec((member.shape[0], ts, HEAD_DIM), lambda bi, i: (0, i, 0)),
        ],
        out_specs=out_specs,
        out_shape=out_shape,
        compiler_params=_params("parallel", "parallel"),
        name="inproj",
    )(x, g.reshape(1, d), w_rows, w_cols, member)
    names = [o[0] for o in _ROW_OUTS] + [o[0] for o in _COL_OUTS]
    return dict(zip(names, outs))


ONES_ROWS = 16
EXP_ROWS = 32
TILES_PER_TRIP = 2
_ADD_TILE = {"own": 0, "near": 1, "edge": 2, "far": None}


def _tile_schedule(n_qt, window=None):
    tiles = [("own", i, i) for i in range(n_qt)] + [("near", i, i - 1) for i in range(1, n_qt)]
    if window is None:
        tiles += [("far", i, j) for j in range(n_qt - 2) for i in range(j + 2, n_qt)]
    else:
        assert window == 2
        tiles += [("edge", i, i - window) for i in range(window, n_qt)]
    kinds = tuple(kind for kind, _, _ in tiles)
    return kinds, jnp.asarray(np.array([[i for _, i, _ in tiles], [j for _, _, j in tiles]], np.int32))


def _attend_tiles(kinds, tab_ref, score_inputs, value_tiles, m_ref, acc_ref, s_refs, p_refs):
    tq = m_ref.shape[-1]

    def scores(t, kind, slot):
        qi, kj = tab_ref[0, t], tab_ref[1, t]
        rows = pl.ds(pl.multiple_of(qi * SUBLANES, SUBLANES), SUBLANES)
        pend = []
        for ci, (k, q, adds) in enumerate(score_inputs(qi, kj, kind)):
            s = lax.dot_general(k, q, NT_DIMS, preferred_element_type=F32)
            for a in adds:
                s = s + a
            s_refs[slot][ci] = s
            m_old = m_ref[ci, rows, :][0:1, :]
            m_new = jnp.maximum(m_old, jnp.max(s, axis=0, keepdims=True))
            m_ref[ci, rows, :] = jnp.broadcast_to(m_new, (SUBLANES, tq))
            pend.append((m_new, jnp.exp2(m_old - m_new)))
        return qi, kj, tuple(pend)

    def accumulate(pending, slot):
        qi, kj, pend = pending
        s_ref, p_ref = s_refs[slot], p_refs[slot]
        for ci, (v_t, (m_new, alpha)) in enumerate(zip(value_tiles(kj), pend)):
            tk = v_t.shape[1]
            for r in range(tk // EXP_ROWS):
                rows = slice(r * EXP_ROWS, (r + 1) * EXP_ROWS)
                p_ref[ci, rows, :] = jnp.exp2((s_ref[ci, rows, :] - m_new).astype(BF16))
            v_ext = jnp.concatenate([v_t, jnp.ones((ONES_ROWS, tk), BF16)], axis=0)
            acc_ref[ci, qi] = alpha * acc_ref[ci, qi] + jnp.dot(v_ext, p_ref[ci], preferred_element_type=F32)

    m_ref[...] = jnp.full(m_ref.shape, NEG_INF, F32)
    acc_ref[...] = jnp.zeros(acc_ref.shape, F32)
    n = len(kinds)
    pending = scores(0, kinds[0], 0)
    step = 0
    while step < n - 1:
        kind = kinds[step + 1]
        run = 1
        while step + run < n - 1 and kinds[step + run + 1] == kind:
            run += 1

        def group(u, pending, step=step, kind=kind):
            for d in range(TILES_PER_TRIP):
                nxt = scores(step + TILES_PER_TRIP * u + d + 1, kind, (step + d + 1) % 2)
                accumulate(pending, (step + d) % 2)
                pending = nxt
            return pending

        if run // TILES_PER_TRIP:
            pending = lax.fori_loop(0, run // TILES_PER_TRIP, group, pending)
            step += run - run % TILES_PER_TRIP
        for _ in range(run % TILES_PER_TRIP):
            nxt = scores(step + 1, kind, (step + 1) % 2)
            accumulate(pending, step % 2)
            pending = nxt
            step += 1
    accumulate(pending, (n - 1) % 2)


def _rows(ref, j, t=ATT_TILE):
    return ref[pl.ds(pl.multiple_of(j * t, t), t), :]


def _cols(ref, j, t=ATT_TILE):
    return ref[:, pl.ds(pl.multiple_of(j * t, t), t)]


def _normalized(acc, dv=HEAD_DIM):
    return acc[:dv] / acc[dv:dv + 1]


def _topk_rows(score, n_rows, topk):
    row_id = lax.broadcasted_iota(jnp.int32, score.shape, 0)
    rank = jnp.zeros(score.shape, F32)
    for c in range(n_rows):
        other = score[c:c + 1, :]
        beats = (other > score) | ((other == score) & (row_id > c))
        rank = rank + jnp.where(beats, 1.0, 0.0)
    return (rank < topk) & (row_id < n_rows)


def _extend_query(q, chosen):
    tq, dh = q.shape
    pen = jnp.where(chosen, 0.0, NEG_INF)
    pen = jnp.concatenate([pen, jnp.zeros((LANES - pen.shape[0], tq), F32)], axis=0)
    return jnp.concatenate([q, pen.T[:, :dh].astype(q.dtype)], axis=1)


def _attention_scratch(n, n_qt, t, dv=HEAD_DIM):
    return [pltpu.VMEM((n, n_qt * SUBLANES, t), F32), pltpu.VMEM((n, n_qt, dv + ONES_ROWS, t), F32),
            pltpu.VMEM((n, t, t), F32), pltpu.VMEM((n, t, t), F32),
            pltpu.VMEM((n, t, t), BF16), pltpu.VMEM((n, t, t), BF16)]


SMEM_SPEC = pl.BlockSpec(memory_space=pltpu.SMEM)


def _moba_kernel(tab_ref, q_ref, k_ref, vt_ref, bias_ref, avg_ref, o_ref, qx_ref, m_ref, acc_ref,
                 s0_ref, s1_ref, p0_ref, p1_ref, *, kinds):
    t = ATT_TILE
    s = q_ref.shape[2]
    n_blk = s // MOBA_BLOCK
    heads = range(MOBA_HEADS)
    blk_rows = -(-n_blk // SUBLANES) * SUBLANES
    ranked = min((MOBA_TOPK + 1) * MOBA_BLOCK, s)
    blk = lax.broadcasted_iota(jnp.int32, (blk_rows, s - ranked), 0)
    own = (ranked + lax.broadcasted_iota(jnp.int32, (blk_rows, s - ranked), 1)) // MOBA_BLOCK
    blk_t = lax.broadcasted_iota(jnp.int32, (blk_rows, t), 0)

    for h in heads:
        if ranked < s:
            k_mean = jnp.dot(avg_ref[...], k_ref[0, h], preferred_element_type=F32)[:, :HEAD_DIM]
            km_hi = k_mean.astype(BF16)
            km_lo = (k_mean - km_hi.astype(F32)).astype(BF16)
            q = q_ref[0, h, ranked:, :]
            gate = (lax.dot_general(km_hi, q, NT_DIMS, preferred_element_type=F32)
                    + lax.dot_general(km_lo, q, NT_DIMS, preferred_element_type=F32))
            gate = jnp.where(blk < own, gate[:blk_rows], NEG_INF)
            chosen = (_topk_rows(gate, n_blk, MOBA_TOPK) & (blk < own)) | (blk == own)
        for qi in range(s // t):
            cols = slice(qi * t, (qi + 1) * t)
            picked = blk_t <= qi * t // MOBA_BLOCK if qi * t < ranked else chosen[:, qi * t - ranked:(qi + 1) * t - ranked]
            qx_ref[h, cols, :] = _extend_query(q_ref[0, h, cols, :], picked)

    def score_inputs(qi, kj, kind):
        tile = _ADD_TILE[kind]
        return [(_rows(k_ref.at[0, h], kj), _rows(qx_ref.at[h], qi), [] if tile is None else [bias_ref[h, tile]])
                for h in heads]

    def value_tiles(kj):
        return [_cols(vt_ref.at[0, h * HEAD_DIM:(h + 1) * HEAD_DIM], kj) for h in heads]

    _attend_tiles(kinds, tab_ref, score_inputs, value_tiles, m_ref, acc_ref, (s0_ref, s1_ref), (p0_ref, p1_ref))

    def finish(qi, carry):
        o_t = jnp.concatenate([_normalized(acc_ref[h, qi]) for h in heads], axis=0)
        o_ref[0, pl.ds(pl.multiple_of(qi * t, t), t), :] = o_t.T.astype(o_ref.dtype)
        return carry

    lax.fori_loop(0, s // t, finish, 0)


def _moba(mq, mk, mv_t, bias_near, avg):
    b, h, s, dh = mq.shape
    t = ATT_TILE
    assert MOBA_BLOCK == t and s % t == 0 and s // t <= avg.shape[0]
    kinds, table = _tile_schedule(s // t)
    return pl.pallas_call(
        functools.partial(_moba_kernel, kinds=kinds),
        grid=(b,),
        in_specs=[
            SMEM_SPEC,
            pl.BlockSpec((1, h, s, dh), lambda bi: (bi, 0, 0, 0)),
            pl.BlockSpec((1, h, s, mk.shape[3]), lambda bi: (bi, 0, 0, 0)),
            pl.BlockSpec((1, h * dh, s), lambda bi: (bi, 0, 0)),
            pl.BlockSpec(bias_near.shape, lambda bi: (0, 0, 0, 0)),
            pl.BlockSpec(avg.shape, lambda bi: (0, 0)),
        ],
        out_specs=pl.BlockSpec((1, s, h * dh), lambda bi: (bi, 0, 0)),
        out_shape=jax.ShapeDtypeStruct((b, s, h * dh), BF16),
        scratch_shapes=[pltpu.VMEM((h, s, mk.shape[3]), BF16)] + _attention_scratch(h, s // t, t),
        compiler_params=_params("parallel"),
        name="moba",
    )(table, mq, mk, mv_t, bias_near, avg)


def _diff_kernel(tab_ref, q_ref, k_ref, vt_ref, bias_ref, lam_ref, subln_ref, o_ref, m_ref, acc_ref,
                 s0_ref, s1_ref, p0_ref, p1_ref, *, kinds, lambda_init):
    t = ATT_TILE
    heads = range(DIFF_HEADS)
    lane = lax.broadcasted_iota(jnp.int32, (t, HEAD_DIM), 1)
    lp = lam_ref[...]
    lam = (jnp.exp(jnp.sum(lp[0:1] * lp[1:2], axis=-1, keepdims=True))
           - jnp.exp(jnp.sum(lp[2:3] * lp[3:4], axis=-1, keepdims=True)) + lambda_init)

    def score_inputs(qi, kj, kind):
        tile = _ADD_TILE[kind]
        out = []
        for h in heads:
            k = _rows(k_ref.at[0, h], kj)
            q = _rows(q_ref.at[0, h], qi)
            adds = [] if tile is None else [bias_ref[h, tile]]
            out.append((k, jnp.where(lane < DIFF_QK_DIM, q, jnp.zeros_like(q)), adds))
            out.append((k, jnp.where(lane >= DIFF_QK_DIM, q, jnp.zeros_like(q)), adds))
        return out

    def value_tiles(kj):
        tiles = [_cols(vt_ref.at[0, h * HEAD_DIM:(h + 1) * HEAD_DIM], kj) for h in heads]
        return [tiles[h] for h in heads for _ in range(2)]

    _attend_tiles(kinds, tab_ref, score_inputs, value_tiles, m_ref, acc_ref, (s0_ref, s1_ref), (p0_ref, p1_ref))

    def finish(qi, carry):
        outs = []
        for h in heads:
            o = _normalized(acc_ref[2 * h, qi]) - lam * _normalized(acc_ref[2 * h + 1, qi])
            o = o * lax.rsqrt(jnp.mean(o * o, axis=0, keepdims=True) + RMS_EPS) * subln_ref[...]
            outs.append(o * (1.0 - lambda_init))
        o_ref[0, pl.ds(pl.multiple_of(qi * t, t), t), :] = jnp.concatenate(outs, axis=0).T.astype(o_ref.dtype)
        return carry

    lax.fori_loop(0, q_ref.shape[2] // t, finish, 0)


def _diff(dq, dk, dv_t, bias_near, lam_params, subln_g, lambda_init):
    b, h, s, dh = dq.shape
    t = ATT_TILE
    kinds, table = _tile_schedule(s // t)
    return pl.pallas_call(
        functools.partial(_diff_kernel, kinds=kinds, lambda_init=lambda_init),
        grid=(b,),
        in_specs=[
            SMEM_SPEC,
            pl.BlockSpec((1, h, s, dh), lambda bi: (bi, 0, 0, 0)),
            pl.BlockSpec((1, h, s, dh), lambda bi: (bi, 0, 0, 0)),
            pl.BlockSpec((1, h * dh, s), lambda bi: (bi, 0, 0)),
            pl.BlockSpec(bias_near.shape, lambda bi: (0, 0, 0, 0)),
            pl.BlockSpec(lam_params.shape, lambda bi: (0, 0)),
            pl.BlockSpec((dh, 1), lambda bi: (0, 0)),
        ],
        out_specs=pl.BlockSpec((1, s, h * dh), lambda bi: (bi, 0, 0)),
        out_shape=jax.ShapeDtypeStruct((b, s, h * dh), BF16),
        scratch_shapes=_attention_scratch(2 * h, s // t, t),
        compiler_params=_params("parallel"),
        name="diff",
    )(table, dq, dk, dv_t, bias_near, lam_params, subln_g.reshape(dh, 1))


def _compress_kernel(kc_ref, vc_ref, pe_ref, w1_ref, w2k_ref, w2vt_ref, ko_ref, vo_ref):
    half = NSA_CMP_STRIDE * HEAD_DIM

    n_chunk = kc_ref.shape[2] // NSA_CMP_STRIDE

    def hidden(t, c_ref):
        c = jnp.concatenate([c_ref[0, 0, pl.ds(l, n_chunk, stride=NSA_CMP_STRIDE), :] for l in range(NSA_CMP_STRIDE)], axis=1)
        top = (c + pe_ref[2 * t:2 * t + 1]).astype(BF16)
        bot = (c + pe_ref[2 * t + 1:2 * t + 2]).astype(BF16)
        a = jnp.dot(top, w1_ref[t, :half], preferred_element_type=F32)
        bm = jnp.dot(bot, w1_ref[t, half:], preferred_element_type=F32)
        hid = a + pltpu.roll(bm, bm.shape[0] - 1, 0)
        return jax.nn.gelu(hid).astype(BF16)

    ko_ref[0, 0] = jnp.dot(hidden(0, kc_ref), w2k_ref[...], preferred_element_type=F32)
    vo_ref[0, 0] = lax.dot_general(w2vt_ref[...], hidden(1, vc_ref), NT_DIMS, preferred_element_type=F32)


def _compress(kc, vc, pe4, w1, w2k, w2v_t):
    b, g, s, dh = kc.shape
    n_chunk = s // NSA_CMP_STRIDE
    spec_in = pl.BlockSpec((1, 1, s, dh), lambda bi, gi: (bi, gi, 0, 0))
    return pl.pallas_call(
        _compress_kernel,
        grid=(b, g),
        in_specs=[
            spec_in, spec_in,
            pl.BlockSpec(pe4.shape, lambda bi, gi: (0, 0)),
            pl.BlockSpec(w1.shape, lambda bi, gi: (0, 0, 0)),
            pl.BlockSpec(w2k.shape, lambda bi, gi: (0, 0)),
            pl.BlockSpec(w2v_t.shape, lambda bi, gi: (0, 0)),
        ],
        out_specs=[pl.BlockSpec((1, 1, n_chunk, dh), lambda bi, gi: (bi, gi, 0, 0)),
                   pl.BlockSpec((1, 1, dh, n_chunk), lambda bi, gi: (bi, gi, 0, 0))],
        out_shape=[jax.ShapeDtypeStruct((b, g, n_chunk, dh), F32), jax.ShapeDtypeStruct((b, g, dh, n_chunk), F32)],
        compiler_params=_params("parallel", "parallel"),
        name="nsa_compress",
    )(kc, vc, pe4, w1, w2k, w2v_t)


def _nsa_kernel(slc_tab_ref, win_tab_ref, q_ref, kcmp_ref, vcmpt_ref, ks_ref, vst_ref, kw_ref, vwt_ref, gate_ref,
                bias_ref, overlap_ref, o_ref, qx_ref, ocmp_ref, m_ref, acc_ref, m2_ref, acc2_ref,
                s0_ref, s1_ref, p0_ref, p1_ref, *, slc_kinds, win_kinds):
    t = ATT_TILE
    heads = range(NSA_GROUP)
    s = q_ref.shape[2]
    n_chunk = kcmp_ref.shape[2]
    n_slc = s // NSA_SLC_BLOCK
    ks_g, vst_g = ks_ref.at[0, 0], vst_ref.at[0]
    kw_g, vwt_g = kw_ref.at[0, 0], vwt_ref.at[0]
    k_cmp = kcmp_ref[0, 0].astype(BF16)
    v_cmp_t = vcmpt_ref[0, 0].astype(BF16)
    topk = min(NSA_SLC_TOPK, n_slc)
    width = min(topk * NSA_SLC_BLOCK, s)
    assert s % width == 0 and width % t == 0

    def select(q0):
        cmp_id = lax.broadcasted_iota(jnp.int32, (n_chunk, width), 0)
        q_pos = q0 + lax.broadcasted_iota(jnp.int32, (n_chunk, width), 1)
        cmp_valid = (cmp_id * NSA_CMP_STRIDE + (NSA_CMP_BLOCK - 1) <= q_pos) & (cmp_id < n_chunk - 1)
        p_sum = jnp.zeros((n_chunk, width), F32)
        for r in heads:
            sc = lax.dot_general(k_cmp, q_ref[0, r, q0:q0 + width, :], NT_DIMS, preferred_element_type=F32)
            m = jnp.max(jnp.where(cmp_valid, sc, NEG_INF), axis=0, keepdims=True)
            e = jnp.where(cmp_valid, jnp.exp2(sc - m), 0.0)
            l = jnp.sum(e, axis=0, keepdims=True)
            p = e / jnp.where(l > 0.0, l, 1.0)
            p_sum = p_sum + p
            o = jnp.dot(v_cmp_t, p.astype(BF16), preferred_element_type=F32)
            for u in range(width // t):
                ocmp_ref[r, q0 // t + u] = o[:, u * t:(u + 1) * t]
        blk = lax.broadcasted_iota(jnp.int32, (n_slc, width), 0)
        cur = (q0 + lax.broadcasted_iota(jnp.int32, (n_slc, width), 1)) // NSA_SLC_BLOCK
        if q0 + width <= topk * NSA_SLC_BLOCK:
            sel = blk <= cur
        else:
            ps_hi = p_sum.astype(BF16)
            ps_lo = (p_sum - ps_hi.astype(F32)).astype(BF16)
            imp = (jnp.dot(overlap_ref[...], ps_hi, preferred_element_type=F32)
                   + jnp.dot(overlap_ref[...], ps_lo, preferred_element_type=F32))
            forced = (blk == 0) | (blk > cur - NSA_LOCAL_BLOCKS)
            score = jnp.where(blk <= cur, jnp.where(forced, FORCE_SCORE, imp), NEG_INF)
            sel = _topk_rows(score, n_slc, topk)
        for u in range(width // t):
            rows = slice(q0 + u * t, q0 + (u + 1) * t)
            for r in heads:
                qx_ref[r, rows, :] = _extend_query(q_ref[0, r, rows, :], sel[:, u * t:(u + 1) * t])

    for q0 in range(0, s, width):
        select(q0)

    def adds(r, kind):
        return [] if _ADD_TILE[kind] is None else [bias_ref[0, r, _ADD_TILE[kind]]]

    def slc_inputs(qi, kj, kind):
        return [(_rows(ks_g, kj), _rows(qx_ref.at[r], qi), adds(r, kind)) for r in heads]

    def win_inputs(qi, kj, kind):
        return [(_rows(kw_g, kj), _rows(q_ref.at[0, r], qi), adds(r, kind)) for r in heads]

    bufs = ((s0_ref, s1_ref), (p0_ref, p1_ref))
    _attend_tiles(slc_kinds, slc_tab_ref, slc_inputs, lambda kj: [_cols(vst_g, kj)] * len(heads), m_ref, acc_ref, *bufs)
    _attend_tiles(win_kinds, win_tab_ref, win_inputs, lambda kj: [_cols(vwt_g, kj)] * len(heads), m2_ref, acc2_ref, *bufs)

    def finish(qi, carry):
        cols = pl.ds(pl.multiple_of(qi * t, t), t)
        gates = jax.nn.sigmoid(gate_ref[0, :, cols])
        outs = []
        for r in heads:
            g_cmp, g_slc, g_win = (gates[3 * r + br:3 * r + br + 1, :] for br in range(3))
            outs.append(g_cmp * ocmp_ref[r, qi] + g_slc * _normalized(acc_ref[r, qi]) + g_win * _normalized(acc2_ref[r, qi]))
        o_ref[0, cols, :] = jnp.concatenate(outs, axis=0).T.astype(o_ref.dtype)
        return carry

    lax.fori_loop(0, s // t, finish, 0)


def _nsa(nq, k_cmp, v_cmp_t, ks, vs_t, kw, vw_t, gate_t, bias_tiles, overlap_t):
    b, _, s, dh = nq.shape
    g, r, t = NSA_KV_HEADS, NSA_GROUP, ATT_TILE
    assert NSA_WINDOW == 2 * t and s % t == 0 and t % NSA_SLC_BLOCK == 0
    slc_kinds, slc_table = _tile_schedule(s // t)
    win_kinds, win_table = _tile_schedule(s // t, window=NSA_WINDOW // t)
    k_spec = lambda k: pl.BlockSpec((1, 1, s, k.shape[3]), lambda bi, gi: (bi, gi, 0, 0))
    vt_spec = pl.BlockSpec((1, dh, s), lambda bi, gi: (bi, gi, 0))
    scratch = _attention_scratch(r, s // t, t)
    return pl.pallas_call(
        functools.partial(_nsa_kernel, slc_kinds=slc_kinds, win_kinds=win_kinds),
        grid=(b, g),
        in_specs=[
            SMEM_SPEC, SMEM_SPEC,
            pl.BlockSpec((1, r, s, dh), lambda bi, gi: (bi, gi, 0, 0)),
            pl.BlockSpec((1, 1) + k_cmp.shape[2:], lambda bi, gi: (bi, gi, 0, 0)),
            pl.BlockSpec((1, 1) + v_cmp_t.shape[2:], lambda bi, gi: (bi, gi, 0, 0)),
            k_spec(ks), vt_spec, k_spec(kw), vt_spec,
            pl.BlockSpec((1, NSA_GATE_ROWS, s), lambda bi, gi: (bi, gi, 0)),
            pl.BlockSpec((1,) + bias_tiles.shape[1:], lambda bi, gi: (gi, 0, 0, 0, 0)),
            pl.BlockSpec(overlap_t.shape, lambda bi, gi: (0, 0)),
        ],
        out_specs=pl.BlockSpec((1, s, r * dh), lambda bi, gi: (bi, 0, gi)),
        out_shape=jax.ShapeDtypeStruct((b, s, g * r * dh), BF16),
        scratch_shapes=[pltpu.VMEM((r, s, ks.shape[3]), BF16), pltpu.VMEM((r, s // t, dh, t), F32)]
                       + scratch[:2] + scratch,
        compiler_params=_params("parallel", "parallel"),
        name="nsa",
    )(slc_table, win_table, nq, k_cmp, v_cmp_t, ks, vs_t, kw, vw_t, gate_t, bias_tiles, overlap_t)


def _rel_bucket(dist):
    n = jnp.maximum(dist, 0)
    max_exact = REL_BUCKETS // 2
    n_f = jnp.maximum(n, max_exact).astype(F32)
    large = max_exact + (jnp.log(n_f / max_exact) / math.log(REL_MAX_DIST / max_exact)
                         * (REL_BUCKETS - max_exact)).astype(jnp.int32)
    return jnp.where(n < max_exact, n, jnp.minimum(large, REL_BUCKETS - 1))


def _bias_tiles(rel_bias):
    t = ATT_TILE
    assert t >= REL_MAX_DIST
    heads = rel_bias.shape[1]
    width = 2 * t + 1
    by_dist = rel_bias[_rel_bucket(jnp.arange(width))] - rel_bias[REL_BUCKETS - 1]
    skew = jnp.broadcast_to(by_dist.T[:, None, :], (heads, t, width)).reshape(heads, t * width)
    skew = skew[:, :t * (width - 1)].reshape(heads, t, width - 1)
    causal = jnp.arange(t)[:, None] <= jnp.arange(t)[None, :]
    own = jnp.where(causal, skew[:, :, :t], NEG_INF)
    return jnp.stack([own, skew[:, :, t:]], axis=1)


def _const_tables(s):
    n_moba = s // MOBA_BLOCK
    avg = np.zeros((2 * SUBLANES, s), np.float32)
    for j in range(n_moba):
        avg[j, j * MOBA_BLOCK:(j + 1) * MOBA_BLOCK] = 1.0 / MOBA_BLOCK
    n_cmp = (s - NSA_CMP_BLOCK) // NSA_CMP_STRIDE + 1
    n_slc = s // NSA_SLC_BLOCK
    cmp_start = np.arange(n_cmp) * NSA_CMP_STRIDE
    slc_start = np.arange(n_slc) * NSA_SLC_BLOCK
    ov = np.clip(np.minimum(cmp_start[:, None] + NSA_CMP_BLOCK, slc_start[None, :] + NSA_SLC_BLOCK)
                 - np.maximum(cmp_start[:, None], slc_start[None, :]), 0, None) / NSA_CMP_BLOCK
    overlap_t = np.zeros((n_slc, s // NSA_CMP_STRIDE), np.float32)
    overlap_t[:, :n_cmp] = ov.T
    t = ATT_TILE
    win_far = np.where(np.arange(t)[:, None] > np.arange(t)[None, :], 0.0, NEG_INF).astype(np.float32)
    member = np.zeros((len(_MEMBER_BLOCKS), s, HEAD_DIM), np.float32)
    for kind, block in enumerate(_MEMBER_BLOCKS):
        assert s // block <= HEAD_DIM
        member[kind, np.arange(s), np.arange(s) // block] = 1.0
    return jnp.asarray(avg, BF16), jnp.asarray(overlap_t, BF16), jnp.asarray(win_far), jnp.asarray(member, BF16)


def kernel(x, rel_bias, norm_ffn1, ffn1_gate, ffn1_up, ffn1_down, norm_mix, w_in, diff_lambda, diff_subln, nsa_cmp_pe, nsa_cmp_w1, nsa_cmp_w2, w_out, norm_ffn2, ffn2_gate, ffn2_up, ffn2_down, final_norm):
    b, s, d = x.shape
    depth = w_in.shape[0]
    assert d == D_MODEL
    h0, h1 = MOBA_HEADS, MOBA_HEADS + DIFF_HEADS
    near = _bias_tiles(rel_bias.astype(F32))
    avg, overlap_t, win_far, member = _const_tables(s)
    near = near * LOG2E
    moba_bias = near[:h0]
    diff_bias = near[h0:h1]
    nsa_bias = jnp.concatenate([near[h1:], jnp.broadcast_to(win_far, near[h1:, :1].shape)], axis=1)
    nsa_bias = nsa_bias.reshape((NSA_KV_HEADS, NSA_GROUP) + nsa_bias.shape[1:])
    bf = lambda a: a.astype(BF16)
    ffn1_gate, ffn1_up, ffn1_down, ffn2_gate, ffn2_up, ffn2_down, w_in, w_out = (
        _to_bf16(w) for w in (ffn1_gate, ffn1_up, ffn1_down, ffn2_gate, ffn2_up, ffn2_down, w_in, w_out))
    cmp_w1 = _to_bf16(nsa_cmp_w1.reshape((-1,) + nsa_cmp_w1.shape[2:])).reshape(nsa_cmp_w1.shape)

    x = x.reshape(b * s, d)
    flat = lambda a: a.reshape(b * s, a.shape[-1])
    for l in range(depth):
        lambda_init = 0.8 - 0.6 * math.exp(-0.3 * l)
        x = _ffn(x, l, norm_ffn1[l], ffn1_gate, ffn1_up, ffn1_down)

        p = _inproj(x.reshape(b, s, d), norm_mix[l], *_inproj_weights(w_in[l]), member)
        o_moba = _moba(p["mq"], p["mk"], p["mvT"], moba_bias, avg)
        o_diff = _diff(p["dq"], p["dk"], p["dvT"], diff_bias, diff_lambda[l].astype(F32), diff_subln[l].astype(F32),
                       lambda_init)
        pe4 = nsa_cmp_pe[l].astype(F32).reshape(4, NSA_CMP_STRIDE * HEAD_DIM)
        k_cmp, v_cmp_t = _compress(p["kc"], p["vc"], pe4, cmp_w1[l], bf(nsa_cmp_w2[l, 0]), bf(nsa_cmp_w2[l, 1].T))
        o_nsa = _nsa(p["nq"], k_cmp, v_cmp_t, p["ks"], p["vsT"], p["kw"], p["vwT"], p["gateT"], nsa_bias, overlap_t)
        x = _ffn(x, l, norm_ffn2[l], ffn2_gate, ffn2_up, ffn2_down,
                 mix=(flat(o_moba), flat(o_diff), flat(o_nsa)), w_mix=w_out,
                 post_g=final_norm if l == depth - 1 else None)
    return x.reshape(b, s, d)
```

```python
import functools
import math

import numpy as np
import jax
import jax.numpy as jnp
from jax import lax
from jax.experimental import pallas as pl
from jax.experimental.pallas import tpu as pltpu

F32 = jnp.float32
BF16 = jnp.bfloat16

D_MODEL = 1024
HEAD_DIM = 64
MOBA_HEADS = 4
MOBA_BLOCK = 256
MOBA_TOPK = 3
DIFF_HEADS = 4
DIFF_QK_DIM = HEAD_DIM // 2
NSA_HEADS = 8
NSA_KV_HEADS = 2
NSA_GROUP = NSA_HEADS // NSA_KV_HEADS
NSA_CMP_BLOCK = 32
NSA_CMP_STRIDE = 16
NSA_CMP_HIDDEN = 256
NSA_SLC_BLOCK = 64
NSA_SLC_TOPK = 16
NSA_LOCAL_BLOCKS = 2
NSA_WINDOW = 512
REL_BUCKETS = 32
REL_MAX_DIST = 128
D_FF = 2816
RMS_EPS = 1e-6
NEG_INF = -1e30
FORCE_SCORE = 1e4

NSA_GATE_W = 3 * NSA_HEADS
NSA_GATE_ROWS = 16
LANES = 128
MXU_WIDTH = 256
SUBLANES = 8
ATT_TILE = 256
VMEM_LIMIT = 48 * 1024 * 1024
LOG2E = math.log2(math.e)

NT_DIMS = (((1,), (1,)), ((), ()))


def _rms(x, g):
    return x * lax.rsqrt(jnp.mean(x * x, axis=-1, keepdims=True) + RMS_EPS) * g


def _params(*sem):
    return pltpu.CompilerParams(dimension_semantics=sem, vmem_limit_bytes=VMEM_LIMIT)


def _cast_kernel(x_ref, o_ref):
    o_ref[...] = x_ref[...].astype(o_ref.dtype)


def _to_bf16(w, *, row_blocks=4):
    layers, rows, cols = w.shape
    br = rows // row_blocks
    assert rows % row_blocks == 0 and br % (2 * SUBLANES) == 0
    spec = pl.BlockSpec((1, br, cols), lambda l, r: (l, r, 0))
    return pl.pallas_call(
        _cast_kernel,
        grid=(layers, row_blocks),
        in_specs=[spec],
        out_specs=spec,
        out_shape=jax.ShapeDtypeStruct(w.shape, BF16),
        compiler_params=_params("parallel", "parallel"),
        name="to_bf16",
    )(w)


def _ffn_kernel(*refs, tf, n_mix, post_norm):
    x_ref, refs = refs[0], refs[1:]
    mix_refs, refs = refs[:n_mix], refs[n_mix:]
    if n_mix:
        wmix_ref, refs = refs[0], refs[1:]
    g_ref, wg_ref, wu_ref, wd_ref, refs = refs[0], refs[1], refs[2], refs[3], refs[4:]
    if post_norm:
        post_ref, refs = refs[0], refs[1:]
    o_ref, h_ref = refs

    y = x_ref[...]
    row = 0
    for m_ref in mix_refs:
        y = y + jnp.dot(m_ref[...], wmix_ref[row:row + m_ref.shape[1]], preferred_element_type=F32)
        row += m_ref.shape[1]
    o_ref[...] = y
    xn = _rms(y, g_ref[...]).astype(BF16)
    for f in range(h_ref.shape[1] // tf):
        cols = slice(f * tf, (f + 1) * tf)
        gate = jnp.dot(xn, wg_ref[:, cols], preferred_element_type=F32)
        up = jnp.dot(xn, wu_ref[:, cols], preferred_element_type=F32)
        h_ref[:, cols] = (gate * jax.nn.sigmoid(gate) * up).astype(BF16)
    o = o_ref[...] + 0.5 * jnp.dot(h_ref[...], wd_ref[...], preferred_element_type=F32)
    o_ref[...] = _rms(o, post_ref[...]) if post_norm else o


def _ffn(x2, layer, g, wg, wu, wd, mix=(), w_mix=None, post_g=None, *, tm=1024, tf=256):
    n, d = x2.shape
    ff = wg.shape[2]
    resident = dict(pipeline_mode=pl.Buffered(1))
    row_spec = lambda a: pl.BlockSpec((tm, a.shape[1]), lambda i: (i, 0))
    whole = lambda a: pl.BlockSpec((None,) + a.shape[1:], lambda i: (layer, 0, 0), **resident)
    gain = lambda a: a.reshape(1, d)
    args = [x2, *mix] + ([w_mix] if mix else []) + [gain(g), wg, wu, wd] + ([gain(post_g)] if post_g is not None else [])
    specs = ([row_spec(x2)] + [row_spec(m) for m in mix] + ([whole(w_mix)] if mix else [])
             + [pl.BlockSpec((1, d), lambda i: (0, 0)), whole(wg), whole(wu), whole(wd)]
             + ([pl.BlockSpec((1, d), lambda i: (0, 0))] if post_g is not None else []))
    return pl.pallas_call(
        functools.partial(_ffn_kernel, tf=tf, n_mix=len(mix), post_norm=post_g is not None),
        grid=(n // tm,),
        in_specs=specs,
        out_specs=row_spec(x2),
        out_shape=jax.ShapeDtypeStruct((n, d), F32),
        scratch_shapes=[pltpu.VMEM((tm, ff), BF16)],
        compiler_params=_params("parallel"),
        name="ffn",
    )(*args)


_ROW_OUTS = (
    ("mq", 0, 4, BF16, HEAD_DIM ** -0.5 * LOG2E, None),
    ("mk", 256, 4, BF16, 1.0, 0),
    ("dq", 768, 4, BF16, DIFF_QK_DIM ** -0.5 * LOG2E, None),
    ("dk", 1024, 4, BF16, 1.0, None),
    ("nq", 1536, 8, BF16, HEAD_DIM ** -0.5 * LOG2E, None),
    ("kc", 2048, 2, F32, 1.0, None),
    ("vc", 2176, 2, F32, 1.0, None),
    ("ks", 2304, 2, BF16, 1.0, 1),
    ("kw", 2560, 2, BF16, 1.0, None),
)
_MEMBER_BLOCKS = (MOBA_BLOCK, NSA_SLC_BLOCK)
_COL_OUTS = (
    ("mvT", 512, 256, BF16),
    ("dvT", 1280, 256, BF16),
    ("vsT", 2432, 128, BF16),
    ("vwT", 2688, 128, BF16),
    ("gateT", None, NSA_KV_HEADS * NSA_GATE_ROWS, F32),
)


def _inproj_kernel(x_ref, g_ref, w_ref, wt_ref, member_ref, *out_refs):
    xn = _rms(x_ref[0], g_ref[...]).astype(BF16)
    col, pending = 0, []
    for out, o_ref in zip(_ROW_OUTS, out_refs):
        pending.append((out, o_ref))
        width = sum(o[2] for o, _ in pending) * HEAD_DIM
        if width % MXU_WIDTH and out is not _ROW_OUTS[-1]:
            continue
        p = jnp.dot(xn, w_ref[:, col:col + width], preferred_element_type=F32)
        col += width
        first = 0
        for (_, _, heads, dtype, scale, member), ref in pending:
            for h in range(first, first + heads):
                head = p[:, h * HEAD_DIM:(h + 1) * HEAD_DIM]
                head = (head if scale == 1.0 else head * scale).astype(dtype)
                ref[0, h - first] = head if member is None else jnp.concatenate([head, member_ref[member]], axis=1)
            first += heads
        pending = []
    pt = lax.dot_general(wt_ref[...], xn, NT_DIMS, preferred_element_type=F32)
    row = 0
    for (_, _, rows, dtype), o_ref in zip(_COL_OUTS, out_refs[len(_ROW_OUTS):]):
        o_ref[0] = pt[row:row + rows].astype(dtype)
        row += rows


def _inproj_weights(w_l):
    w_rows = jnp.concatenate([w_l[:, c:c + h * HEAD_DIM] for _, c, h, _, _, _ in _ROW_OUTS], axis=1)
    gate_cols = w_l[:, w_l.shape[1] - NSA_GATE_W:]
    per_group = 3 * NSA_GROUP
    gate_t = jnp.zeros((NSA_KV_HEADS * NSA_GATE_ROWS, w_l.shape[0]), w_l.dtype)
    for g in range(NSA_KV_HEADS):
        gate_t = gate_t.at[g * NSA_GATE_ROWS:g * NSA_GATE_ROWS + per_group].set(
            gate_cols[:, g * per_group:(g + 1) * per_group].T)
    w_cols = jnp.concatenate([w_l[:, c:c + r].T for _, c, r, _ in _COL_OUTS[:-1]] + [gate_t], axis=0)
    return w_rows.astype(BF16), w_cols.astype(BF16)


def _inproj(x, g, w_rows, w_cols, member, *, ts=512):
    b, s, d = x.shape
    widths = [HEAD_DIM if member is None else 2 * HEAD_DIM for _, _, _, _, _, member in _ROW_OUTS]
    out_shape = [jax.ShapeDtypeStruct((b, o[2], s, w), o[3]) for o, w in zip(_ROW_OUTS, widths)]
    out_specs = [pl.BlockSpec((1, o[2], ts, w), lambda bi, i: (bi, 0, i, 0)) for o, w in zip(_ROW_OUTS, widths)]
    out_shape += [jax.ShapeDtypeStruct((b, rows, s), dtype) for _, _, rows, dtype in _COL_OUTS]
    out_specs += [pl.BlockSpec((1, rows, ts), lambda bi, i: (bi, 0, i)) for _, _, rows, _ in _COL_OUTS]
    outs = pl.pallas_call(
        _inproj_kernel,
        grid=(b, s // ts),
        in_specs=[
            pl.BlockSpec((1, ts, d), lambda bi, i: (bi, i, 0)),
            pl.BlockSpec((1, d), lambda bi, i: (0, 0)),
            pl.BlockSpec(w_rows.shape, lambda bi, i: (0, 0)),
            pl.BlockSpec(w_cols.shape, lambda bi, i: (0, 0)),
            pl.BlockSpec((member.shape[0], ts, HEAD_DIM), lambda bi, i: (0, i, 0)),
        ],
        out_specs=out_specs,
        out_shape=out_shape,
        compiler_params=_params("parallel", "parallel"),
        name="inproj",
    )(x, g.reshape(1, d), w_rows, w_cols, member)
    names = [o[0] for o in _ROW_OUTS] + [o[0] for o in _COL_OUTS]
    return dict(zip(names, outs))


ONES_ROWS = 16
EXP_ROWS = 32
TILES_PER_TRIP = 2
_ADD_TILE = {"own": 0, "near": 1, "edge": 2, "far": None}


def _tile_schedule(n_qt, window=None):
    tiles = [("own", i, i) for i in range(n_qt)] + [("near", i, i - 1) for i in range(1, n_qt)]
    if window is None:
        tiles += [("far", i, j) for j in range(n_qt - 2) for i in range(j + 2, n_qt)]
    else:
        assert window == 2
        tiles += [("edge", i, i - window) for i in range(window, n_qt)]
    kinds = tuple(kind for kind, _, _ in tiles)
    return kinds, jnp.asarray(np.array([[i for _, i, _ in tiles], [j for _, _, j in tiles]], np.int32))


def _attend_tiles(kinds, tab_ref, score_inputs, value_tiles, m_ref, acc_ref, s_refs, p_refs):
    tq = m_ref.shape[-1]

    def scores(t, kind, slot):
        qi, kj = tab_ref[0, t], tab_ref[1, t]
        rows = pl.ds(pl.multiple_of(qi * SUBLANES, SUBLANES), SUBLANES)
        pend = []
        for ci, (k, q, adds) in enumerate(score_inputs(qi, kj, kind)):
            s = lax.dot_general(k, q, NT_DIMS, preferred_element_type=F32)
            for a in adds:
                s = s + a
            s_refs[slot][ci] = s
            m_old = m_ref[ci, rows, :][0:1, :]
            m_new = jnp.maximum(m_old, jnp.max(s, axis=0, keepdims=True))
            m_ref[ci, rows, :] = jnp.broadcast_to(m_new, (SUBLANES, tq))
            pend.append((m_new, jnp.exp2(m_old - m_new)))
        return qi, kj, tuple(pend)

    def accumulate(pending, slot):
        qi, kj, pend = pending
        s_ref, p_ref = s_refs[slot], p_refs[slot]
        for ci, (v_t, (m_new, alpha)) in enumerate(zip(value_tiles(kj), pend)):
            tk = v_t.shape[1]
            for r in range(tk // EXP_ROWS):
                rows = slice(r * EXP_ROWS, (r + 1) * EXP_ROWS)
                p_ref[ci, rows, :] = jnp.exp2((s_ref[ci, rows, :] - m_new).astype(BF16))
            v_ext = jnp.concatenate([v_t, jnp.ones((ONES_ROWS, tk), BF16)], axis=0)
            acc_ref[ci, qi] = alpha * acc_ref[ci, qi] + jnp.dot(v_ext, p_ref[ci], preferred_element_type=F32)

    m_ref[...] = jnp.full(m_ref.shape, NEG_INF, F32)
    acc_ref[...] = jnp.zeros(acc_ref.shape, F32)
    n = len(kinds)
    pending = scores(0, kinds[0], 0)
    step = 0
    while step < n - 1:
        kind = kinds[step + 1]
        run = 1
        while step + run < n - 1 and kinds[step + run + 1] == kind:
            run += 1

        def group(u, pending, step=step, kind=kind):
            for d in range(TILES_PER_TRIP):
                nxt = scores(step + TILES_PER_TRIP * u + d + 1, kind, (step + d + 1) % 2)
                accumulate(pending, (step + d) % 2)
                pending = nxt
            return pending

        if run // TILES_PER_TRIP:
            pending = lax.fori_loop(0, run // TILES_PER_TRIP, group, pending)
            step += run - run % TILES_PER_TRIP
        for _ in range(run % TILES_PER_TRIP):
            nxt = scores(step + 1, kind, (step + 1) % 2)
            accumulate(pending, step % 2)
            pending = nxt
            step += 1
    accumulate(pending, (n - 1) % 2)


def _rows(ref, j, t=ATT_TILE):
    return ref[pl.ds(pl.multiple_of(j * t, t), t), :]


def _cols(ref, j, t=ATT_TILE):
    return ref[:, pl.ds(pl.multiple_of(j * t, t), t)]


def _normalized(acc, dv=HEAD_DIM):
    return acc[:dv] / acc[dv:dv + 1]


def _topk_rows(score, n_rows, topk):
    row_id = lax.broadcasted_iota(jnp.int32, score.shape, 0)
    rank = jnp.zeros(score.shape, F32)
    for c in range(n_rows):
        other = score[c:c + 1, :]
        beats = (other > score) | ((other == score) & (row_id > c))
        rank = rank + jnp.where(beats, 1.0, 0.0)
    return (rank < topk) & (row_id < n_rows)


def _extend_query(q, chosen):
    tq, dh = q.shape
    pen = jnp.where(chosen, 0.0, NEG_INF)
    pen = jnp.concatenate([pen, jnp.zeros((LANES - pen.shape[0], tq), F32)], axis=0)
    return jnp.concatenate([q, pen.T[:, :dh].astype(q.dtype)], axis=1)


def _attention_scratch(n, n_qt, t, dv=HEAD_DIM):
    return [pltpu.VMEM((n, n_qt * SUBLANES, t), F32), pltpu.VMEM((n, n_qt, dv + ONES_ROWS, t), F32),
            pltpu.VMEM((n, t, t), F32), pltpu.VMEM((n, t, t), F32),
            pltpu.VMEM((n, t, t), BF16), pltpu.VMEM((n, t, t), BF16)]


SMEM_SPEC = pl.BlockSpec(memory_space=pltpu.SMEM)


def _moba_kernel(tab_ref, q_ref, k_ref, vt_ref, bias_ref, avg_ref, o_ref, qx_ref, m_ref, acc_ref,
                 s0_ref, s1_ref, p0_ref, p1_ref, *, kinds):
    t = ATT_TILE
    s = q_ref.shape[2]
    n_blk = s // MOBA_BLOCK
    heads = range(MOBA_HEADS)
    blk_rows = -(-n_blk // SUBLANES) * SUBLANES
    ranked = min((MOBA_TOPK + 1) * MOBA_BLOCK, s)
    blk = lax.broadcasted_iota(jnp.int32, (blk_rows, s - ranked), 0)
    own = (ranked + lax.broadcasted_iota(jnp.int32, (blk_rows, s - ranked), 1)) // MOBA_BLOCK
    blk_t = lax.broadcasted_iota(jnp.int32, (blk_rows, t), 0)

    for h in heads:
        if ranked < s:
            k_mean = jnp.dot(avg_ref[...], k_ref[0, h], preferred_element_type=F32)[:, :HEAD_DIM]
            km_hi = k_mean.astype(BF16)
            km_lo = (k_mean - km_hi.astype(F32)).astype(BF16)
            q = q_ref[0, h, ranked:, :]
            gate = (lax.dot_general(km_hi, q, NT_DIMS, preferred_element_type=F32)
                    + lax.dot_general(km_lo, q, NT_DIMS, preferred_element_type=F32))
            gate = jnp.where(blk < own, gate[:blk_rows], NEG_INF)
            chosen = (_topk_rows(gate, n_blk, MOBA_TOPK) & (blk < own)) | (blk == own)
        for qi in range(s // t):
            cols = slice(qi * t, (qi + 1) * t)
            picked = blk_t <= qi * t // MOBA_BLOCK if qi * t < ranked else chosen[:, qi * t - ranked:(qi + 1) * t - ranked]
            qx_ref[h, cols, :] = _extend_query(q_ref[0, h, cols, :], picked)

    def score_inputs(qi, kj, kind):
        tile = _ADD_TILE[kind]
        return [(_rows(k_ref.at[0, h], kj), _rows(qx_ref.at[h], qi), [] if tile is None else [bias_ref[h, tile]])
                for h in heads]

    def value_tiles(kj):
        return [_cols(vt_ref.at[0, h * HEAD_DIM:(h + 1) * HEAD_DIM], kj) for h in heads]

    _attend_tiles(kinds, tab_ref, score_inputs, value_tiles, m_ref, acc_ref, (s0_ref, s1_ref), (p0_ref, p1_ref))

    def finish(qi, carry):
        o_t = jnp.concatenate([_normalized(acc_ref[h, qi]) for h in heads], axis=0)
        o_ref[0, pl.ds(pl.multiple_of(qi * t, t), t), :] = o_t.T.astype(o_ref.dtype)
        return carry

    lax.fori_loop(0, s // t, finish, 0)


def _moba(mq, mk, mv_t, bias_near, avg):
    b, h, s, dh = mq.shape
    t = ATT_TILE
    assert MOBA_BLOCK == t and s % t == 0 and s // t <= avg.shape[0]
    kinds, table = _tile_schedule(s // t)
    return pl.pallas_call(
        functools.partial(_moba_kernel, kinds=kinds),
        grid=(b,),
        in_specs=[
            SMEM_SPEC,
            pl.BlockSpec((1, h, s, dh), lambda bi: (bi, 0, 0, 0)),
            pl.BlockSpec((1, h, s, mk.shape[3]), lambda bi: (bi, 0, 0, 0)),
            pl.BlockSpec((1, h * dh, s), lambda bi: (bi, 0, 0)),
            pl.BlockSpec(bias_near.shape, lambda bi: (0, 0, 0, 0)),
            pl.BlockSpec(avg.shape, lambda bi: (0, 0)),
        ],
        out_specs=pl.BlockSpec((1, s, h * dh), lambda bi: (bi, 0, 0)),
        out_shape=jax.ShapeDtypeStruct((b, s, h * dh), BF16),
        scratch_shapes=[pltpu.VMEM((h, s, mk.shape[3]), BF16)] + _attention_scratch(h, s // t, t),
        compiler_params=_params("parallel"),
        name="moba",
    )(table, mq, mk, mv_t, bias_near, avg)


def _diff_kernel(tab_ref, q_ref, k_ref, vt_ref, bias_ref, lam_ref, subln_ref, o_ref, m_ref, acc_ref,
                 s0_ref, s1_ref, p0_ref, p1_ref, *, kinds, lambda_init):
    t = ATT_TILE
    heads = range(DIFF_HEADS)
    lane = lax.broadcasted_iota(jnp.int32, (t, HEAD_DIM), 1)
    lp = lam_ref[...]
    lam = (jnp.exp(jnp.sum(lp[0:1] * lp[1:2], axis=-1, keepdims=True))
           - jnp.exp(jnp.sum(lp[2:3] * lp[3:4], axis=-1, keepdims=True)) + lambda_init)

    def score_inputs(qi, kj, kind):
        tile = _ADD_TILE[kind]
        out = []
        for h in heads:
            k = _rows(k_ref.at[0, h], kj)
            q = _rows(q_ref.at[0, h], qi)
            adds = [] if tile is None else [bias_ref[h, tile]]
            out.append((k, jnp.where(lane < DIFF_QK_DIM, q, jnp.zeros_like(q)), adds))
            out.append((k, jnp.where(lane >= DIFF_QK_DIM, q, jnp.zeros_like(q)), adds))
        return out

    def value_tiles(kj):
        tiles = [_cols(vt_ref.at[0, h * HEAD_DIM:(h + 1) * HEAD_DIM], kj) for h in heads]
        return [tiles[h] for h in heads for _ in range(2)]

    _attend_tiles(kinds, tab_ref, score_inputs, value_tiles, m_ref, acc_ref, (s0_ref, s1_ref), (p0_ref, p1_ref))

    def finish(qi, carry):
        outs = []
        for h in heads:
            o = _normalized(acc_ref[2 * h, qi]) - lam * _normalized(acc_ref[2 * h + 1, qi])
            o = o * lax.rsqrt(jnp.mean(o * o, axis=0, keepdims=True) + RMS_EPS) * subln_ref[...]
            outs.append(o * (1.0 - lambda_init))
        o_ref[0, pl.ds(pl.multiple_of(qi * t, t), t), :] = jnp.concatenate(outs, axis=0).T.astype(o_ref.dtype)
        return carry

    lax.fori_loop(0, q_ref.shape[2] // t, finish, 0)


def _diff(dq, dk, dv_t, bias_near, lam_params, subln_g, lambda_init):
    b, h, s, dh = dq.shape
    t = ATT_TILE
    kinds, table = _tile_schedule(s // t)
    return pl.pallas_call(
        functools.partial(_diff_kernel, kinds=kinds, lambda_init=lambda_init),
        grid=(b,),
        in_specs=[
            SMEM_SPEC,
            pl.BlockSpec((1, h, s, dh), lambda bi: (bi, 0, 0, 0)),
            pl.BlockSpec((1, h, s, dh), lambda bi: (bi, 0, 0, 0)),
            pl.BlockSpec((1, h * dh, s), lambda bi: (bi, 0, 0)),
            pl.BlockSpec(bias_near.shape, lambda bi: (0, 0, 0, 0)),
            pl.BlockSpec(lam_params.shape, lambda bi: (0, 0)),
            pl.BlockSpec((dh, 1), lambda bi: (0, 0)),
        ],
        out_specs=pl.BlockSpec((1, s, h * dh), lambda bi: (bi, 0, 0)),
        out_shape=jax.ShapeDtypeStruct((b, s, h * dh), BF16),
        scratch_shapes=_attention_scratch(2 * h, s // t, t),
        compiler_params=_params("parallel"),
        name="diff",
    )(table, dq, dk, dv_t, bias_near, lam_params, subln_g.reshape(dh, 1))


def _compress_kernel(kc_ref, vc_ref, pe_ref, w1_ref, w2k_ref, w2vt_ref, ko_ref, vo_ref):
    half = NSA_CMP_STRIDE * HEAD_DIM

    n_chunk = kc_ref.shape[2] // NSA_CMP_STRIDE

    def hidden(t, c_ref):
        c = jnp.concatenate([c_ref[0, 0, pl.ds(l, n_chunk, stride=NSA_CMP_STRIDE), :] for l in range(NSA_CMP_STRIDE)], axis=1)
        top = (c + pe_ref[2 * t:2 * t + 1]).astype(BF16)
        bot = (c + pe_ref[2 * t + 1:2 * t + 2]).astype(BF16)
        a = jnp.dot(top, w1_ref[t, :half], preferred_element_type=F32)
        bm = jnp.dot(bot, w1_ref[t, half:], preferred_element_type=F32)
        hid = a + pltpu.roll(bm, bm.shape[0] - 1, 0)
        return jax.nn.gelu(hid).astype(BF16)

    ko_ref[0, 0] = jnp.dot(hidden(0, kc_ref), w2k_ref[...], preferred_element_type=F32)
    vo_ref[0, 0] = lax.dot_general(w2vt_ref[...], hidden(1, vc_ref), NT_DIMS, preferred_element_type=F32)


def _compress(kc, vc, pe4, w1, w2k, w2v_t):
    b, g, s, dh = kc.shape
    n_chunk = s // NSA_CMP_STRIDE
    spec_in = pl.BlockSpec((1, 1, s, dh), lambda bi, gi: (bi, gi, 0, 0))
    return pl.pallas_call(
        _compress_kernel,
        grid=(b, g),
        in_specs=[
            spec_in, spec_in,
            pl.BlockSpec(pe4.shape, lambda bi, gi: (0, 0)),
            pl.BlockSpec(w1.shape, lambda bi, gi: (0, 0, 0)),
            pl.BlockSpec(w2k.shape, lambda bi, gi: (0, 0)),
            pl.BlockSpec(w2v_t.shape, lambda bi, gi: (0, 0)),
        ],
        out_specs=[pl.BlockSpec((1, 1, n_chunk, dh), lambda bi, gi: (bi, gi, 0, 0)),
                   pl.BlockSpec((1, 1, dh, n_chunk), lambda bi, gi: (bi, gi, 0, 0))],
        out_shape=[jax.ShapeDtypeStruct((b, g, n_chunk, dh), F32), jax.ShapeDtypeStruct((b, g, dh, n_chunk), F32)],
        compiler_params=_params("parallel", "parallel"),
        name="nsa_compress",
    )(kc, vc, pe4, w1, w2k, w2v_t)


def _nsa_kernel(slc_tab_ref, win_tab_ref, q_ref, kcmp_ref, vcmpt_ref, ks_ref, vst_ref, kw_ref, vwt_ref, gate_ref,
                bias_ref, overlap_ref, o_ref, qx_ref, ocmp_ref, m_ref, acc_ref, m2_ref, acc2_ref,
                s0_ref, s1_ref, p0_ref, p1_ref, *, slc_kinds, win_kinds):
    t = ATT_TILE
    heads = range(NSA_GROUP)
    s = q_ref.shape[2]
    n_chunk = kcmp_ref.shape[2]
    n_slc = s // NSA_SLC_BLOCK
    ks_g, vst_g = ks_ref.at[0, 0], vst_ref.at[0]
    kw_g, vwt_g = kw_ref.at[0, 0], vwt_ref.at[0]
    k_cmp = kcmp_ref[0, 0].astype(BF16)
    v_cmp_t = vcmpt_ref[0, 0].astype(BF16)
    topk = min(NSA_SLC_TOPK, n_slc)
    width = min(topk * NSA_SLC_BLOCK, s)
    assert s % width == 0 and width % t == 0

    def select(q0):
        cmp_id = lax.broadcasted_iota(jnp.int32, (n_chunk, width), 0)
        q_pos = q0 + lax.broadcasted_iota(jnp.int32, (n_chunk, width), 1)
        cmp_valid = (cmp_id * NSA_CMP_STRIDE + (NSA_CMP_BLOCK - 1) <= q_pos) & (cmp_id < n_chunk - 1)
        p_sum = jnp.zeros((n_chunk, width), F32)
        for r in heads:
            sc = lax.dot_general(k_cmp, q_ref[0, r, q0:q0 + width, :], NT_DIMS, preferred_element_type=F32)
            m = jnp.max(jnp.where(cmp_valid, sc, NEG_INF), axis=0, keepdims=True)
            e = jnp.where(cmp_valid, jnp.exp2(sc - m), 0.0)
            l = jnp.sum(e, axis=0, keepdims=True)
            p = e / jnp.where(l > 0.0, l, 1.0)
            p_sum = p_sum + p
            o = jnp.dot(v_cmp_t, p.astype(BF16), preferred_element_type=F32)
            for u in range(width // t):
                ocmp_ref[r, q0 // t + u] = o[:, u * t:(u + 1) * t]
        blk = lax.broadcasted_iota(jnp.int32, (n_slc, width), 0)
        cur = (q0 + lax.broadcasted_iota(jnp.int32, (n_slc, width), 1)) // NSA_SLC_BLOCK
        if q0 + width <= topk * NSA_SLC_BLOCK:
            sel = blk <= cur
        else:
            ps_hi = p_sum.astype(BF16)
            ps_lo = (p_sum - ps_hi.astype(F32)).astype(BF16)
            imp = (jnp.dot(overlap_ref[...], ps_hi, preferred_element_type=F32)
                   + jnp.dot(overlap_ref[...], ps_lo, preferred_element_type=F32))
            forced = (blk == 0) | (blk > cur - NSA_LOCAL_BLOCKS)
            score = jnp.where(blk <= cur, jnp.where(forced, FORCE_SCORE, imp), NEG_INF)
            sel = _topk_rows(score, n_slc, topk)
        for u in range(width // t):
            rows = slice(q0 + u * t, q0 + (u + 1) * t)
            for r in heads:
                qx_ref[r, rows, :] = _extend_query(q_ref[0, r, rows, :], sel[:, u * t:(u + 1) * t])

    for q0 in range(0, s, width):
        select(q0)

    def adds(r, kind):
        return [] if _ADD_TILE[kind] is None else [bias_ref[0, r, _ADD_TILE[kind]]]

    def slc_inputs(qi, kj, kind):
        return [(_rows(ks_g, kj), _rows(qx_ref.at[r], qi), adds(r, kind)) for r in heads]

    def win_inputs(qi, kj, kind):
        return [(_rows(kw_g, kj), _rows(q_ref.at[0, r], qi), adds(r, kind)) for r in heads]

    bufs = ((s0_ref, s1_ref), (p0_ref, p1_ref))
    _attend_tiles(slc_kinds, slc_tab_ref, slc_inputs, lambda kj: [_cols(vst_g, kj)] * len(heads), m_ref, acc_ref, *bufs)
    _attend_tiles(win_kinds, win_tab_ref, win_inputs, lambda kj: [_cols(vwt_g, kj)] * len(heads), m2_ref, acc2_ref, *bufs)

    def finish(qi, carry):
        cols = pl.ds(pl.multiple_of(qi * t, t), t)
        gates = jax.nn.sigmoid(gate_ref[0, :, cols])
        outs = []
        for r in heads:
            g_cmp, g_slc, g_win = (gates[3 * r + br:3 * r + br + 1, :] for br in range(3))
            outs.append(g_cmp * ocmp_ref[r, qi] + g_slc * _normalized(acc_ref[r, qi]) + g_win * _normalized(acc2_ref[r, qi]))
        o_ref[0, cols, :] = jnp.concatenate(outs, axis=0).T.astype(o_ref.dtype)
        return carry

    lax.fori_loop(0, s // t, finish, 0)


def _nsa(nq, k_cmp, v_cmp_t, ks, vs_t, kw, vw_t, gate_t, bias_tiles, overlap_t):
    b, _, s, dh = nq.shape
    g, r, t = NSA_KV_HEADS, NSA_GROUP, ATT_TILE
    assert NSA_WINDOW == 2 * t and s % t == 0 and t % NSA_SLC_BLOCK == 0
    slc_kinds, slc_table = _tile_schedule(s // t)
    win_kinds, win_table = _tile_schedule(s // t, window=NSA_WINDOW // t)
    k_spec = lambda k: pl.BlockSpec((1, 1, s, k.shape[3]), lambda bi, gi: (bi, gi, 0, 0))
    vt_spec = pl.BlockSpec((1, dh, s), lambda bi, gi: (bi, gi, 0))
    scratch = _attention_scratch(r, s // t, t)
    return pl.pallas_call(
        functools.partial(_nsa_kernel, slc_kinds=slc_kinds, win_kinds=win_kinds),
        grid=(b, g),
        in_specs=[
            SMEM_SPEC, SMEM_SPEC,
            pl.BlockSpec((1, r, s, dh), lambda bi, gi: (bi, gi, 0, 0)),
            pl.BlockSpec((1, 1) + k_cmp.shape[2:], lambda bi, gi: (bi, gi, 0, 0)),
            pl.BlockSpec((1, 1) + v_cmp_t.shape[2:], lambda bi, gi: (bi, gi, 0, 0)),
            k_spec(ks), vt_spec, k_spec(kw), vt_spec,
            pl.BlockSpec((1, NSA_GATE_ROWS, s), lambda bi, gi: (bi, gi, 0)),
            pl.BlockSpec((1,) + bias_tiles.shape[1:], lambda bi, gi: (gi, 0, 0, 0, 0)),
            pl.BlockSpec(overlap_t.shape, lambda bi, gi: (0, 0)),
        ],
        out_specs=pl.BlockSpec((1, s, r * dh), lambda bi, gi: (bi, 0, gi)),
        out_shape=jax.ShapeDtypeStruct((b, s, g * r * dh), BF16),
        scratch_shapes=[pltpu.VMEM((r, s, ks.shape[3]), BF16), pltpu.VMEM((r, s // t, dh, t), F32)]
                       + scratch[:2] + scratch,
        compiler_params=_params("parallel", "parallel"),
        name="nsa",
    )(slc_table, win_table, nq, k_cmp, v_cmp_t, ks, vs_t, kw, vw_t, gate_t, bias_tiles, overlap_t)


def _rel_bucket(dist):
    n = jnp.maximum(dist, 0)
    max_exact = REL_BUCKETS // 2
    n_f = jnp.maximum(n, max_exact).astype(F32)
    large = max_exact + (jnp.log(n_f / max_exact) / math.log(REL_MAX_DIST / max_exact)
                         * (REL_BUCKETS - max_exact)).astype(jnp.int32)
    return jnp.where(n < max_exact, n, jnp.minimum(large, REL_BUCKETS - 1))


def _bias_tiles(rel_bias):
    t = ATT_TILE
    assert t >= REL_MAX_DIST
    heads = rel_bias.shape[1]
    width = 2 * t + 1
    by_dist = rel_bias[_rel_bucket(jnp.arange(width))] - rel_bias[REL_BUCKETS - 1]
    skew = jnp.broadcast_to(by_dist.T[:, None, :], (heads, t, width)).reshape(heads, t * width)
    skew = skew[:, :t * (width - 1)].reshape(heads, t, width - 1)
    causal = jnp.arange(t)[:, None] <= jnp.arange(t)[None, :]
    own = jnp.where(causal, skew[:, :, :t], NEG_INF)
    return jnp.stack([own, skew[:, :, t:]], axis=1)


def _const_tables(s):
    n_moba = s // MOBA_BLOCK
    avg = np.zeros((2 * SUBLANES, s), np.float32)
    for j in range(n_moba):
        avg[j, j * MOBA_BLOCK:(j + 1) * MOBA_BLOCK] = 1.0 / MOBA_BLOCK
    n_cmp = (s - NSA_CMP_BLOCK) // NSA_CMP_STRIDE + 1
    n_slc = s // NSA_SLC_BLOCK
    cmp_start = np.arange(n_cmp) * NSA_CMP_STRIDE
    slc_start = np.arange(n_slc) * NSA_SLC_BLOCK
    ov = np.clip(np.minimum(cmp_start[:, None] + NSA_CMP_BLOCK, slc_start[None, :] + NSA_SLC_BLOCK)
                 - np.maximum(cmp_start[:, None], slc_start[None, :]), 0, None) / NSA_CMP_BLOCK
    overlap_t = np.zeros((n_slc, s // NSA_CMP_STRIDE), np.float32)
    overlap_t[:, :n_cmp] = ov.T
    t = ATT_TILE
    win_far = np.where(np.arange(t)[:, None] > np.arange(t)[None, :], 0.0, NEG_INF).astype(np.float32)
    member = np.zeros((len(_MEMBER_BLOCKS), s, HEAD_DIM), np.float32)
    for kind, block in enumerate(_MEMBER_BLOCKS):
        assert s // block <= HEAD_DIM
        member[kind, np.arange(s), np.arange(s) // block] = 1.0
    return jnp.asarray(avg, BF16), jnp.asarray(overlap_t, BF16), jnp.asarray(win_far), jnp.asarray(member, BF16)


def kernel(x, rel_bias, norm_ffn1, ffn1_gate, ffn1_up, ffn1_down, norm_mix, w_in, diff_lambda, diff_subln, nsa_cmp_pe, nsa_cmp_w1, nsa_cmp_w2, w_out, norm_ffn2, ffn2_gate, ffn2_up, ffn2_down, final_norm):
    b, s, d = x.shape
    depth = w_in.shape[0]
    assert d == D_MODEL
    h0, h1 = MOBA_HEADS, MOBA_HEADS + DIFF_HEADS
    near = _bias_tiles(rel_bias.astype(F32))
    avg, overlap_t, win_far, member = _const_tables(s)
    near = near * LOG2E
    moba_bias = near[:h0]
    diff_bias = near[h0:h1]
    nsa_bias = jnp.concatenate([near[h1:], jnp.broadcast_to(win_far, near[h1:, :1].shape)], axis=1)
    nsa_bias = nsa_bias.reshape((NSA_KV_HEADS, NSA_GROUP) + nsa_bias.shape[1:])
    bf = lambda a: a.astype(BF16)
    ffn1_gate, ffn1_up, ffn1_down, ffn2_gate, ffn2_up, ffn2_down, w_in, w_out = (
        _to_bf16(w) for w in (ffn1_gate, ffn1_up, ffn1_down, ffn2_gate, ffn2_up, ffn2_down, w_in, w_out))
    cmp_w1 = _to_bf16(nsa_cmp_w1.reshape((-1,) + nsa_cmp_w1.shape[2:])).reshape(nsa_cmp_w1.shape)

    x = x.reshape(b * s, d)
    flat = lambda a: a.reshape(b * s, a.shape[-1])
    for l in range(depth):
        lambda_init = 0.8 - 0.6 * math.exp(-0.3 * l)
        x = _ffn(x, l, norm_ffn1[l], ffn1_gate, ffn1_up, ffn1_down)

        p = _inproj(x.reshape(b, s, d), norm_mix[l], *_inproj_weights(w_in[l]), member)
        o_moba = _moba(p["mq"], p["mk"], p["mvT"], moba_bias, avg)
        o_diff = _diff(p["dq"], p["dk"], p["dvT"], diff_bias, diff_lambda[l].astype(F32), diff_subln[l].astype(F32),
                       lambda_init)
        pe4 = nsa_cmp_pe[l].astype(F32).reshape(4, NSA_CMP_STRIDE * HEAD_DIM)
        k_cmp, v_cmp_t = _compress(p["kc"], p["vc"], pe4, cmp_w1[l], bf(nsa_cmp_w2[l, 0]), bf(nsa_cmp_w2[l, 1].T))
        o_nsa = _nsa(p["nq"], k_cmp, v_cmp_t, p["ks"], p["vsT"], p["kw"], p["vwT"], p["gateT"], nsa_bias, overlap_t)
        x = _ffn(x, l, norm_ffn2[l], ffn2_gate, ffn2_up, ffn2_down,
                 mix=(flat(o_moba), flat(o_diff), flat(o_nsa)), w_mix=w_out,
                 post_g=final_norm if l == depth - 1 else None)
    return x.reshape(b, s, d)
```

```python
import functools
import math

import numpy as np
import jax
import jax.numpy as jnp
from jax import lax
from jax.experimental import pallas as pl
from jax.experimental.pallas import tpu as pltpu

F32 = jnp.float32
BF16 = jnp.bfloat16

D_MODEL = 1024
HEAD_DIM = 64
MOBA_HEADS = 4
MOBA_BLOCK = 256
MOBA_TOPK = 3
DIFF_HEADS = 4
DIFF_QK_DIM = HEAD_DIM // 2
NSA_HEADS = 8
NSA_KV_HEADS = 2
NSA_GROUP = NSA_HEADS // NSA_KV_HEADS
NSA_CMP_BLOCK = 32
NSA_CMP_STRIDE = 16
NSA_CMP_HIDDEN = 256
NSA_SLC_BLOCK = 64
NSA_SLC_TOPK = 16
NSA_LOCAL_BLOCKS = 2
NSA_WINDOW = 512
REL_BUCKETS = 32
REL_MAX_DIST = 128
D_FF = 2816
RMS_EPS = 1e-6
NEG_INF = -1e30
FORCE_SCORE = 1e4

NSA_GATE_W = 3 * NSA_HEADS
NSA_GATE_ROWS = 16
LANES = 128
MXU_WIDTH = 256
SUBLANES = 8
ATT_TILE = 256
VMEM_LIMIT = 48 * 1024 * 1024
LOG2E = math.log2(math.e)

NT_DIMS = (((1,), (1,)), ((), ()))


def _rms(x, g):
    return x * lax.rsqrt(jnp.mean(x * x, axis=-1, keepdims=True) + RMS_EPS) * g


def _params(*sem):
    return pltpu.CompilerParams(dimension_semantics=sem, vmem_limit_bytes=VMEM_LIMIT)


def _cast_kernel(x_ref, o_ref):
    o_ref[...] = x_ref[...].astype(o_ref.dtype)


def _to_bf16(w, *, row_blocks=4):
    layers, rows, cols = w.shape
    br = rows // row_blocks
    assert rows % row_blocks == 0 and br % (2 * SUBLANES) == 0
    spec = pl.BlockSpec((1, br, cols), lambda l, r: (l, r, 0))
    return pl.pallas_call(
        _cast_kernel,
        grid=(layers, row_blocks),
        in_specs=[spec],
        out_specs=spec,
        out_shape=jax.ShapeDtypeStruct(w.shape, BF16),
        compiler_params=_params("parallel", "parallel"),
        name="to_bf16",
    )(w)


def _ffn_kernel(*refs, tf, n_mix, post_norm):
    x_ref, refs = refs[0], refs[1:]
    mix_refs, refs = refs[:n_mix], refs[n_mix:]
    if n_mix:
        wmix_ref, refs = refs[0], refs[1:]
    g_ref, wg_ref, wu_ref, wd_ref, refs = refs[0], refs[1], refs[2], refs[3], refs[4:]
    if post_norm:
        post_ref, refs = refs[0], refs[1:]
    o_ref, h_ref = refs

    y = x_ref[...]
    row = 0
    for m_ref in mix_refs:
        y = y + jnp.dot(m_ref[...], wmix_ref[row:row + m_ref.shape[1]], preferred_element_type=F32)
        row += m_ref.shape[1]
    o_ref[...] = y
    xn = _rms(y, g_ref[...]).astype(BF16)
    for f in range(h_ref.shape[1] // tf):
        cols = slice(f * tf, (f + 1) * tf)
        gate = jnp.dot(xn, wg_ref[:, cols], preferred_element_type=F32)
        up = jnp.dot(xn, wu_ref[:, cols], preferred_element_type=F32)
        h_ref[:, cols] = (gate * jax.nn.sigmoid(gate) * up).astype(BF16)
    o = o_ref[...] + 0.5 * jnp.dot(h_ref[...], wd_ref[...], preferred_element_type=F32)
    o_ref[...] = _rms(o, post_ref[...]) if post_norm else o


def _ffn(x2, layer, g, wg, wu, wd, mix=(), w_mix=None, post_g=None, *, tm=1024, tf=256):
    n, d = x2.shape
    ff = wg.shape[2]
    resident = dict(pipeline_mode=pl.Buffered(1))
    row_spec = lambda a: pl.BlockSpec((tm, a.shape[1]), lambda i: (i, 0))
    whole = lambda a: pl.BlockSpec((None,) + a.shape[1:], lambda i: (layer, 0, 0), **resident)
    gain = lambda a: a.reshape(1, d)
    args = [x2, *mix] + ([w_mix] if mix else []) + [gain(g), wg, wu, wd] + ([gain(post_g)] if post_g is not None else [])
    specs = ([row_spec(x2)] + [row_spec(m) for m in mix] + ([whole(w_mix)] if mix else [])
             + [pl.BlockSpec((1, d), lambda i: (0, 0)), whole(wg), whole(wu), whole(wd)]
             + ([pl.BlockSpec((1, d), lambda i: (0, 0))] if post_g is not None else []))
    return pl.pallas_call(
        functools.partial(_ffn_kernel, tf=tf, n_mix=len(mix), post_norm=post_g is not None),
        grid=(n // tm,),
        in_specs=specs,
        out_specs=row_spec(x2),
        out_shape=jax.ShapeDtypeStruct((n, d), F32),
        scratch_shapes=[pltpu.VMEM((tm, ff), BF16)],
        compiler_params=_params("parallel"),
        name="ffn",
    )(*args)


_ROW_OUTS = (
    ("mq", 0, 4, BF16, HEAD_DIM ** -0.5 * LOG2E, None),
    ("mk", 256, 4, BF16, 1.0, 0),
    ("dq", 768, 4, BF16, DIFF_QK_DIM ** -0.5 * LOG2E, None),
    ("dk", 1024, 4, BF16, 1.0, None),
    ("nq", 1536, 8, BF16, HEAD_DIM ** -0.5 * LOG2E, None),
    ("kc", 2048, 2, F32, 1.0, None),
    ("vc", 2176, 2, F32, 1.0, None),
    ("ks", 2304, 2, BF16, 1.0, 1),
    ("kw", 2560, 2, BF16, 1.0, None),
)
_MEMBER_BLOCKS = (MOBA_BLOCK, NSA_SLC_BLOCK)
_COL_OUTS = (
    ("mvT", 512, 256, BF16),
    ("dvT", 1280, 256, BF16),
    ("vsT", 2432, 128, BF16),
    ("vwT", 2688, 128, BF16),
    ("gateT", None, NSA_KV_HEADS * NSA_GATE_ROWS, F32),
)


def _inproj_kernel(x_ref, g_ref, w_ref, wt_ref, member_ref, *out_refs):
    xn = _rms(x_ref[0], g_ref[...]).astype(BF16)
    col, pending = 0, []
    for out, o_ref in zip(_ROW_OUTS, out_refs):
        pending.append((out, o_ref))
        width = sum(o[2] for o, _ in pending) * HEAD_DIM
        if width % MXU_WIDTH and out is not _ROW_OUTS[-1]:
            continue
        p = jnp.dot(xn, w_ref[:, col:col + width], preferred_element_type=F32)
        col += width
        first = 0
        for (_, _, heads, dtype, scale, member), ref in pending:
            for h in range(first, first + heads):
                head = p[:, h * HEAD_DIM:(h + 1) * HEAD_DIM]
                head = (head if scale == 1.0 else head * scale).astype(dtype)
                ref[0, h - first] = head if member is None else jnp.concatenate([head, member_ref[member]], axis=1)
            first += heads
        pending = []
    pt = lax.dot_general(wt_ref[...], xn, NT_DIMS, preferred_element_type=F32)
    row = 0
    for (_, _, rows, dtype), o_ref in zip(_COL_OUTS, out_refs[len(_ROW_OUTS):]):
        o_ref[0] = pt[row:row + rows].astype(dtype)
        row += rows


def _inproj_weights(w_l):
    w_rows = jnp.concatenate([w_l[:, c:c + h * HEAD_DIM] for _, c, h, _, _, _ in _ROW_OUTS], axis=1)
    gate_cols = w_l[:, w_l.shape[1] - NSA_GATE_W:]
    per_group = 3 * NSA_GROUP
    gate_t = jnp.zeros((NSA_KV_HEADS * NSA_GATE_ROWS, w_l.shape[0]), w_l.dtype)
    for g in range(NSA_KV_HEADS):
        gate_t = gate_t.at[g * NSA_GATE_ROWS:g * NSA_GATE_ROWS + per_group].set(
            gate_cols[:, g * per_group:(g + 1) * per_group].T)
    w_cols = jnp.concatenate([w_l[:, c:c + r].T for _, c, r, _ in _COL_OUTS[:-1]] + [gate_t], axis=0)
    return w_rows.astype(BF16), w_cols.astype(BF16)


def _inproj(x, g, w_rows, w_cols, member, *, ts=512):
    b, s, d = x.shape
    widths = [HEAD_DIM if member is None else 2 * HEAD_DIM for _, _, _, _, _, member in _ROW_OUTS]
    out_shape = [jax.ShapeDtypeStruct((b, o[2], s, w), o[3]) for o, w in zip(_ROW_OUTS, widths)]
    out_specs = [pl.BlockSpec((1, o[2], ts, w), lambda bi, i: (bi, 0, i, 0)) for o, w in zip(_ROW_OUTS, widths)]
    out_shape += [jax.ShapeDtypeStruct((b, rows, s), dtype) for _, _, rows, dtype in _COL_OUTS]
    out_specs += [pl.BlockSpec((1, rows, ts), lambda bi, i: (bi, 0, i)) for _, _, rows, _ in _COL_OUTS]
    outs = pl.pallas_call(
        _inproj_kernel,
        grid=(b, s // ts),
        in_specs=[
            pl.BlockSpec((1, ts, d), lambda bi, i: (bi, i, 0)),
            pl.BlockSpec((1, d), lambda bi, i: (0, 0)),
            pl.BlockSpec(w_rows.shape, lambda bi, i: (0, 0)),
            pl.BlockSpec(w_cols.shape, lambda bi, i: (0, 0)),
            pl.BlockSpec((member.shape[0], ts, HEAD_DIM), lambda bi, i: (0, i, 0)),
        ],
        out_specs=out_specs,
        out_shape=out_shape,
        compiler_params=_params("parallel", "parallel"),
        name="inproj",
    )(x, g.reshape(1, d), w_rows, w_cols, member)
    names = [o[0] for o in _ROW_OUTS] + [o[0] for o in _COL_OUTS]
    return dict(zip(names, outs))


ONES_ROWS = 16
EXP_ROWS = 32
TILES_PER_TRIP = 2
_ADD_TILE = {"own": 0, "near": 1, "edge": 2, "far": None, "far2": None}
_KEY_TILES = {"far2": 2}


def _tile_schedule(n_qt, window=None):
    tiles = [("own", i, i) for i in range(n_qt)] + [("near", i, i - 1) for i in range(1, n_qt)]
    if window is None:
        tiles += [("far", i, i - 2) for i in range(2, n_qt) if (i - 1) % 2]
        tiles += [("far2", i, j) for j in range(0, n_qt - 3, 2) for i in range(j + 3, n_qt)]
    else:
        assert window == 2
        tiles += [("edge", i, i - window) for i in range(window, n_qt)]
    kinds = tuple(kind for kind, _, _ in tiles)
    return kinds, jnp.asarray(np.array([[i for _, i, _ in tiles], [j for _, _, j in tiles]], np.int32))


def _attend_tiles(kinds, tab_ref, score_inputs, value_tiles, m_ref, acc_ref, s_refs, p_refs):
    tq = m_ref.shape[-1]

    def scores(t, kind, slot):
        qi, kj = tab_ref[0, t], tab_ref[1, t]
        rows = pl.ds(pl.multiple_of(qi * SUBLANES, SUBLANES), SUBLANES)
        pend = []
        for ci, (k, q, adds) in enumerate(score_inputs(qi, kj, kind)):
            s = lax.dot_general(k, q, NT_DIMS, preferred_element_type=F32)
            for a in adds:
                s = s + a
            s_refs[slot][ci, :s.shape[0], :] = s
            m_old = m_ref[ci, rows, :][0:1, :]
            m_new = jnp.maximum(m_old, jnp.max(s, axis=0, keepdims=True))
            m_ref[ci, rows, :] = jnp.broadcast_to(m_new, (SUBLANES, tq))
            pend.append((m_new, jnp.exp2(m_old - m_new)))
        return qi, kj, tuple(pend)

    def accumulate(pending, slot, kind):
        qi, kj, pend = pending
        s_ref, p_ref = s_refs[slot], p_refs[slot]
        for ci, (v_t, (m_new, alpha)) in enumerate(zip(value_tiles(kj, kind), pend)):
            tk = v_t.shape[1]
            for r in range(tk // EXP_ROWS):
                rows = slice(r * EXP_ROWS, (r + 1) * EXP_ROWS)
                p_ref[ci, rows, :] = jnp.exp2((s_ref[ci, rows, :] - m_new).astype(BF16))
            v_ext = jnp.concatenate([v_t, jnp.ones((ONES_ROWS, tk), BF16)], axis=0)
            acc_ref[ci, qi] = alpha * acc_ref[ci, qi] + jnp.dot(v_ext, p_ref[ci, :tk, :], preferred_element_type=F32)

    m_ref[...] = jnp.full(m_ref.shape, NEG_INF, F32)
    acc_ref[...] = jnp.zeros(acc_ref.shape, F32)
    n = len(kinds)
    pending = scores(0, kinds[0], 0)
    step = 0
    while step < n - 1:
        kind = kinds[step + 1]
        run = 1
        while step + run < n - 1 and kinds[step + run + 1] == kind:
            run += 1

        def one(pending, step, pending_kind, kind=kind):
            nxt = scores(step + 1, kind, (step + 1) % 2)
            accumulate(pending, step % 2, pending_kind)
            return nxt

        def group(u, pending, step=step + 1, kind=kind):
            for d in range(TILES_PER_TRIP):
                nxt = scores(step + TILES_PER_TRIP * u + d + 1, kind, (step + d + 1) % 2)
                accumulate(pending, (step + d) % 2, kind)
                pending = nxt
            return pending

        pending = one(pending, step, kinds[step])
        step, run = step + 1, run - 1
        if run // TILES_PER_TRIP:
            pending = lax.fori_loop(0, run // TILES_PER_TRIP, group, pending)
            step += run - run % TILES_PER_TRIP
        for _ in range(run % TILES_PER_TRIP):
            pending = one(pending, step, kind)
            step += 1
    accumulate(pending, (n - 1) % 2, kinds[-1])


def _rows(ref, j, kind=None, t=ATT_TILE):
    return ref[pl.ds(pl.multiple_of(j * t, t), _KEY_TILES.get(kind, 1) * t), :]


def _cols(ref, j, kind=None, t=ATT_TILE):
    return ref[:, pl.ds(pl.multiple_of(j * t, t), _KEY_TILES.get(kind, 1) * t)]


def _normalized(acc, dv=HEAD_DIM):
    return acc[:dv] / acc[dv:dv + 1]


def _topk_rows(score, n_rows, topk):
    row_id = lax.broadcasted_iota(jnp.int32, score.shape, 0)
    rank = jnp.zeros(score.shape, F32)
    for c in range(n_rows):
        other = score[c:c + 1, :]
        beats = (other > score) | ((other == score) & (row_id > c))
        rank = rank + jnp.where(beats, 1.0, 0.0)
    return (rank < topk) & (row_id < n_rows)


def _extend_query(q, chosen):
    tq, dh = q.shape
    pen = jnp.where(chosen, 0.0, NEG_INF)
    pen = jnp.concatenate([pen, jnp.zeros((LANES - pen.shape[0], tq), F32)], axis=0)
    return jnp.concatenate([q, pen.T[:, :dh].astype(q.dtype)], axis=1)


def _attention_scratch(n, n_qt, t, dv=HEAD_DIM):
    tk = max(_KEY_TILES.values()) * t
    return [pltpu.VMEM((n, n_qt * SUBLANES, t), F32), pltpu.VMEM((n, n_qt, dv + ONES_ROWS, t), F32),
            pltpu.VMEM((n, tk, t), F32), pltpu.VMEM((n, tk, t), F32),
            pltpu.VMEM((n, tk, t), BF16), pltpu.VMEM((n, tk, t), BF16)]


SMEM_SPEC = pl.BlockSpec(memory_space=pltpu.SMEM)


def _moba_kernel(tab_ref, q_ref, k_ref, vt_ref, bias_ref, avg_ref, o_ref, qx_ref, m_ref, acc_ref,
                 s0_ref, s1_ref, p0_ref, p1_ref, *, kinds):
    t = ATT_TILE
    s = q_ref.shape[2]
    n_blk = s // MOBA_BLOCK
    heads = range(MOBA_HEADS)
    blk_rows = -(-n_blk // SUBLANES) * SUBLANES
    ranked = min((MOBA_TOPK + 1) * MOBA_BLOCK, s)
    blk = lax.broadcasted_iota(jnp.int32, (blk_rows, s - ranked), 0)
    own = (ranked + lax.broadcasted_iota(jnp.int32, (blk_rows, s - ranked), 1)) // MOBA_BLOCK
    blk_t = lax.broadcasted_iota(jnp.int32, (blk_rows, t), 0)

    for h in heads:
        if ranked < s:
            k_mean = jnp.dot(avg_ref[...], k_ref[0, h], preferred_element_type=F32)[:, :HEAD_DIM]
            km_hi = k_mean.astype(BF16)
            km_lo = (k_mean - km_hi.astype(F32)).astype(BF16)
            q = q_ref[0, h, ranked:, :]
            gate = (lax.dot_general(km_hi, q, NT_DIMS, preferred_element_type=F32)
                    + lax.dot_general(km_lo, q, NT_DIMS, preferred_element_type=F32))
            gate = jnp.where(blk < own, gate[:blk_rows], NEG_INF)
            chosen = (_topk_rows(gate, n_blk, MOBA_TOPK) & (blk < own)) | (blk == own)
        for qi in range(s // t):
            cols = slice(qi * t, (qi + 1) * t)
            picked = blk_t <= qi * t // MOBA_BLOCK if qi * t < ranked else chosen[:, qi * t - ranked:(qi + 1) * t - ranked]
            qx_ref[h, cols, :] = _extend_query(q_ref[0, h, cols, :], picked)

    def score_inputs(qi, kj, kind):
        tile = _ADD_TILE[kind]
        return [(_rows(k_ref.at[0, h], kj, kind), _rows(qx_ref.at[h], qi), [] if tile is None else [bias_ref[h, tile]])
                for h in heads]

    def value_tiles(kj, kind):
        return [_cols(vt_ref.at[0, h * HEAD_DIM:(h + 1) * HEAD_DIM], kj, kind) for h in heads]

    _attend_tiles(kinds, tab_ref, score_inputs, value_tiles, m_ref, acc_ref, (s0_ref, s1_ref), (p0_ref, p1_ref))

    def finish(qi, carry):
        o_t = jnp.concatenate([_normalized(acc_ref[h, qi]) for h in heads], axis=0)
        o_ref[0, pl.ds(pl.multiple_of(qi * t, t), t), :] = o_t.T.astype(o_ref.dtype)
        return carry

    lax.fori_loop(0, s // t, finish, 0)


def _moba(mq, mk, mv_t, bias_near, avg):
    b, h, s, dh = mq.shape
    t = ATT_TILE
    assert MOBA_BLOCK == t and s % t == 0 and s // t <= avg.shape[0]
    kinds, table = _tile_schedule(s // t)
    return pl.pallas_call(
        functools.partial(_moba_kernel, kinds=kinds),
        grid=(b,),
        in_specs=[
            SMEM_SPEC,
            pl.BlockSpec((1, h, s, dh), lambda bi: (bi, 0, 0, 0)),
            pl.BlockSpec((1, h, s, mk.shape[3]), lambda bi: (bi, 0, 0, 0)),
            pl.BlockSpec((1, h * dh, s), lambda bi: (bi, 0, 0)),
            pl.BlockSpec(bias_near.shape, lambda bi: (0, 0, 0, 0)),
            pl.BlockSpec(avg.shape, lambda bi: (0, 0)),
        ],
        out_specs=pl.BlockSpec((1, s, h * dh), lambda bi: (bi, 0, 0)),
        out_shape=jax.ShapeDtypeStruct((b, s, h * dh), BF16),
        scratch_shapes=[pltpu.VMEM((h, s, mk.shape[3]), BF16)] + _attention_scratch(h, s // t, t),
        compiler_params=_params("parallel"),
        name="moba",
    )(table, mq, mk, mv_t, bias_near, avg)


def _diff_kernel(tab_ref, q_ref, k_ref, vt_ref, bias_ref, lam_ref, subln_ref, o_ref, m_ref, acc_ref,
                 s0_ref, s1_ref, p0_ref, p1_ref, *, kinds, lambda_init):
    t = ATT_TILE
    heads = range(DIFF_HEADS)
    lane = lax.broadcasted_iota(jnp.int32, (t, HEAD_DIM), 1)
    lp = lam_ref[...]
    lam = (jnp.exp(jnp.sum(lp[0:1] * lp[1:2], axis=-1, keepdims=True))
           - jnp.exp(jnp.sum(lp[2:3] * lp[3:4], axis=-1, keepdims=True)) + lambda_init)

    def score_inputs(qi, kj, kind):
        tile = _ADD_TILE[kind]
        out = []
        for h in heads:
            k = _rows(k_ref.at[0, h], kj, kind)
            q = _rows(q_ref.at[0, h], qi)
            adds = [] if tile is None else [bias_ref[h, tile]]
            out.append((k, jnp.where(lane < DIFF_QK_DIM, q, jnp.zeros_like(q)), adds))
            out.append((k, jnp.where(lane >= DIFF_QK_DIM, q, jnp.zeros_like(q)), adds))
        return out

    def value_tiles(kj, kind):
        tiles = [_cols(vt_ref.at[0, h * HEAD_DIM:(h + 1) * HEAD_DIM], kj, kind) for h in heads]
        return [tiles[h] for h in heads for _ in range(2)]

    _attend_tiles(kinds, tab_ref, score_inputs, value_tiles, m_ref, acc_ref, (s0_ref, s1_ref), (p0_ref, p1_ref))

    def finish(qi, carry):
        outs = []
        for h in heads:
            o = _normalized(acc_ref[2 * h, qi]) - lam * _normalized(acc_ref[2 * h + 1, qi])
            o = o * lax.rsqrt(jnp.mean(o * o, axis=0, keepdims=True) + RMS_EPS) * subln_ref[...]
            outs.append(o * (1.0 - lambda_init))
        o_ref[0, pl.ds(pl.multiple_of(qi * t, t), t), :] = jnp.concatenate(outs, axis=0).T.astype(o_ref.dtype)
        return carry

    lax.fori_loop(0, q_ref.shape[2] // t, finish, 0)


def _diff(dq, dk, dv_t, bias_near, lam_params, subln_g, lambda_init):
    b, h, s, dh = dq.shape
    t = ATT_TILE
    kinds, table = _tile_schedule(s // t)
    return pl.pallas_call(
        functools.partial(_diff_kernel, kinds=kinds, lambda_init=lambda_init),
        grid=(b,),
        in_specs=[
            SMEM_SPEC,
            pl.BlockSpec((1, h, s, dh), lambda bi: (bi, 0, 0, 0)),
            pl.BlockSpec((1, h, s, dh), lambda bi: (bi, 0, 0, 0)),
            pl.BlockSpec((1, h * dh, s), lambda bi: (bi, 0, 0)),
            pl.BlockSpec(bias_near.shape, lambda bi: (0, 0, 0, 0)),
            pl.BlockSpec(lam_params.shape, lambda bi: (0, 0)),
            pl.BlockSpec((dh, 1), lambda bi: (0, 0)),
        ],
        out_specs=pl.BlockSpec((1, s, h * dh), lambda bi: (bi, 0, 0)),
        out_shape=jax.ShapeDtypeStruct((b, s, h * dh), BF16),
        scratch_shapes=_attention_scratch(2 * h, s // t, t),
        compiler_params=_params("parallel"),
        name="diff",
    )(table, dq, dk, dv_t, bias_near, lam_params, subln_g.reshape(dh, 1))


def _compress_kernel(kc_ref, vc_ref, pe_ref, w1_ref, w2k_ref, w2vt_ref, ko_ref, vo_ref):
    half = NSA_CMP_STRIDE * HEAD_DIM

    n_chunk = kc_ref.shape[2] // NSA_CMP_STRIDE

    def hidden(t, c_ref):
        c = jnp.concatenate([c_ref[0, 0, pl.ds(l, n_chunk, stride=NSA_CMP_STRIDE), :] for l in range(NSA_CMP_STRIDE)], axis=1)
        top = (c + pe_ref[2 * t:2 * t + 1]).astype(BF16)
        bot = (c + pe_ref[2 * t + 1:2 * t + 2]).astype(BF16)
        a = jnp.dot(top, w1_ref[t, :half], preferred_element_type=F32)
        bm = jnp.dot(bot, w1_ref[t, half:], preferred_element_type=F32)
        hid = a + pltpu.roll(bm, bm.shape[0] - 1, 0)
        return jax.nn.gelu(hid).astype(BF16)

    ko_ref[0, 0] = jnp.dot(hidden(0, kc_ref), w2k_ref[...], preferred_element_type=F32)
    vo_ref[0, 0] = lax.dot_general(w2vt_ref[...], hidden(1, vc_ref), NT_DIMS, preferred_element_type=F32)


def _compress(kc, vc, pe4, w1, w2k, w2v_t):
    b, g, s, dh = kc.shape
    n_chunk = s // NSA_CMP_STRIDE
    spec_in = pl.BlockSpec((1, 1, s, dh), lambda bi, gi: (bi, gi, 0, 0))
    return pl.pallas_call(
        _compress_kernel,
        grid=(b, g),
        in_specs=[
            spec_in, spec_in,
            pl.BlockSpec(pe4.shape, lambda bi, gi: (0, 0)),
            pl.BlockSpec(w1.shape, lambda bi, gi: (0, 0, 0)),
            pl.BlockSpec(w2k.shape, lambda bi, gi: (0, 0)),
            pl.BlockSpec(w2v_t.shape, lambda bi, gi: (0, 0)),
        ],
        out_specs=[pl.BlockSpec((1, 1, n_chunk, dh), lambda bi, gi: (bi, gi, 0, 0)),
                   pl.BlockSpec((1, 1, dh, n_chunk), lambda bi, gi: (bi, gi, 0, 0))],
        out_shape=[jax.ShapeDtypeStruct((b, g, n_chunk, dh), F32), jax.ShapeDtypeStruct((b, g, dh, n_chunk), F32)],
        compiler_params=_params("parallel", "parallel"),
        name="nsa_compress",
    )(kc, vc, pe4, w1, w2k, w2v_t)


def _nsa_kernel(slc_tab_ref, win_tab_ref, q_ref, kcmp_ref, vcmpt_ref, ks_ref, vst_ref, kw_ref, vwt_ref, gate_ref,
                bias_ref, overlap_ref, o_ref, qx_ref, ocmp_ref, m_ref, acc_ref, m2_ref, acc2_ref,
                s0_ref, s1_ref, p0_ref, p1_ref, *, slc_kinds, win_kinds):
    t = ATT_TILE
    heads = range(NSA_GROUP)
    s = q_ref.shape[2]
    n_chunk = kcmp_ref.shape[2]
    n_slc = s // NSA_SLC_BLOCK
    ks_g, vst_g = ks_ref.at[0, 0], vst_ref.at[0]
    kw_g, vwt_g = kw_ref.at[0, 0], vwt_ref.at[0]
    k_cmp = kcmp_ref[0, 0].astype(BF16)
    v_cmp_t = vcmpt_ref[0, 0].astype(BF16)
    topk = min(NSA_SLC_TOPK, n_slc)
    width = min(topk * NSA_SLC_BLOCK, s)
    assert s % width == 0 and width % t == 0

    def select(q0):
        cmp_id = lax.broadcasted_iota(jnp.int32, (n_chunk, width), 0)
        q_pos = q0 + lax.broadcasted_iota(jnp.int32, (n_chunk, width), 1)
        cmp_valid = (cmp_id * NSA_CMP_STRIDE + (NSA_CMP_BLOCK - 1) <= q_pos) & (cmp_id < n_chunk - 1)
        p_sum = jnp.zeros((n_chunk, width), F32)
        for r in heads:
            sc = lax.dot_general(k_cmp, q_ref[0, r, q0:q0 + width, :], NT_DIMS, preferred_element_type=F32)
            m = jnp.max(jnp.where(cmp_valid, sc, NEG_INF), axis=0, keepdims=True)
            e = jnp.where(cmp_valid, jnp.exp2(sc - m), 0.0)
            l = jnp.sum(e, axis=0, keepdims=True)
            p = e / jnp.where(l > 0.0, l, 1.0)
            p_sum = p_sum + p
            o = jnp.dot(v_cmp_t, p.astype(BF16), preferred_element_type=F32)
            for u in range(width // t):
                ocmp_ref[r, q0 // t + u] = o[:, u * t:(u + 1) * t]
        blk = lax.broadcasted_iota(jnp.int32, (n_slc, width), 0)
        cur = (q0 + lax.broadcasted_iota(jnp.int32, (n_slc, width), 1)) // NSA_SLC_BLOCK
        if q0 + width <= topk * NSA_SLC_BLOCK:
            sel = blk <= cur
        else:
            ps_hi = p_sum.astype(BF16)
            ps_lo = (p_sum - ps_hi.astype(F32)).astype(BF16)
            imp = (jnp.dot(overlap_ref[...], ps_hi, preferred_element_type=F32)
                   + jnp.dot(overlap_ref[...], ps_lo, preferred_element_type=F32))
            forced = (blk == 0) | (blk > cur - NSA_LOCAL_BLOCKS)
            score = jnp.where(blk <= cur, jnp.where(forced, FORCE_SCORE, imp), NEG_INF)
            sel = _topk_rows(score, n_slc, topk)
        for u in range(width // t):
            rows = slice(q0 + u * t, q0 + (u + 1) * t)
            for r in heads:
                qx_ref[r, rows, :] = _extend_query(q_ref[0, r, rows, :], sel[:, u * t:(u + 1) * t])

    for q0 in range(0, s, width):
        select(q0)

    def adds(r, kind):
        return [] if _ADD_TILE[kind] is None else [bias_ref[0, r, _ADD_TILE[kind]]]

    def slc_inputs(qi, kj, kind):
        return [(_rows(ks_g, kj, kind), _rows(qx_ref.at[r], qi), adds(r, kind)) for r in heads]

    def win_inputs(qi, kj, kind):
        return [(_rows(kw_g, kj, kind), _rows(q_ref.at[0, r], qi), adds(r, kind)) for r in heads]

    bufs = ((s0_ref, s1_ref), (p0_ref, p1_ref))
    _attend_tiles(slc_kinds, slc_tab_ref, slc_inputs, lambda kj, kind: [_cols(vst_g, kj, kind)] * len(heads), m_ref, acc_ref, *bufs)
    _attend_tiles(win_kinds, win_tab_ref, win_inputs, lambda kj, kind: [_cols(vwt_g, kj, kind)] * len(heads), m2_ref, acc2_ref, *bufs)

    def finish(qi, carry):
        cols = pl.ds(pl.multiple_of(qi * t, t), t)
        gates = jax.nn.sigmoid(gate_ref[0, :, cols])
        outs = []
        for r in heads:
            g_cmp, g_slc, g_win = (gates[3 * r + br:3 * r + br + 1, :] for br in range(3))
            outs.append(g_cmp * ocmp_ref[r, qi] + g_slc * _normalized(acc_ref[r, qi]) + g_win * _normalized(acc2_ref[r, qi]))
        o_ref[0, cols, :] = jnp.concatenate(outs, axis=0).T.astype(o_ref.dtype)
        return carry

    lax.fori_loop(0, s // t, finish, 0)


def _nsa(nq, k_cmp, v_cmp_t, ks, vs_t, kw, vw_t, gate_t, bias_tiles, overlap_t):
    b, _, s, dh = nq.shape
    g, r, t = NSA_KV_HEADS, NSA_GROUP, ATT_TILE
    assert NSA_WINDOW == 2 * t and s % t == 0 and t % NSA_SLC_BLOCK == 0
    slc_kinds, slc_table = _tile_schedule(s // t)
    win_kinds, win_table = _tile_schedule(s // t, window=NSA_WINDOW // t)
    k_spec = lambda k: pl.BlockSpec((1, 1, s, k.shape[3]), lambda bi, gi: (bi, gi, 0, 0))
    vt_spec = pl.BlockSpec((1, dh, s), lambda bi, gi: (bi, gi, 0))
    scratch = _attention_scratch(r, s // t, t)
    return pl.pallas_call(
        functools.partial(_nsa_kernel, slc_kinds=slc_kinds, win_kinds=win_kinds),
        grid=(b, g),
        in_specs=[
            SMEM_SPEC, SMEM_SPEC,
            pl.BlockSpec((1, r, s, dh), lambda bi, gi: (bi, gi, 0, 0)),
            pl.BlockSpec((1, 1) + k_cmp.shape[2:], lambda bi, gi: (bi, gi, 0, 0)),
            pl.BlockSpec((1, 1) + v_cmp_t.shape[2:], lambda bi, gi: (bi, gi, 0, 0)),
            k_spec(ks), vt_spec, k_spec(kw), vt_spec,
            pl.BlockSpec((1, NSA_GATE_ROWS, s), lambda bi, gi: (bi, gi, 0)),
            pl.BlockSpec((1,) + bias_tiles.shape[1:], lambda bi, gi: (gi, 0, 0, 0, 0)),
            pl.BlockSpec(overlap_t.shape, lambda bi, gi: (0, 0)),
        ],
        out_specs=pl.BlockSpec((1, s, r * dh), lambda bi, gi: (bi, 0, gi)),
        out_shape=jax.ShapeDtypeStruct((b, s, g * r * dh), BF16),
        scratch_shapes=[pltpu.VMEM((r, s, ks.shape[3]), BF16), pltpu.VMEM((r, s // t, dh, t), F32)]
                       + scratch[:2] + scratch,
        compiler_params=_params("parallel", "parallel"),
        name="nsa",
    )(slc_table, win_table, nq, k_cmp, v_cmp_t, ks, vs_t, kw, vw_t, gate_t, bias_tiles, overlap_t)


def _rel_bucket(dist):
    n = jnp.maximum(dist, 0)
    max_exact = REL_BUCKETS // 2
    n_f = jnp.maximum(n, max_exact).astype(F32)
    large = max_exact + (jnp.log(n_f / max_exact) / math.log(REL_MAX_DIST / max_exact)
                         * (REL_BUCKETS - max_exact)).astype(jnp.int32)
    return jnp.where(n < max_exact, n, jnp.minimum(large, REL_BUCKETS - 1))


def _bias_tiles(rel_bias):
    t = ATT_TILE
    assert t >= REL_MAX_DIST
    heads = rel_bias.shape[1]
    width = 2 * t + 1
    by_dist = rel_bias[_rel_bucket(jnp.arange(width))] - rel_bias[REL_BUCKETS - 1]
    skew = jnp.broadcast_to(by_dist.T[:, None, :], (heads, t, width)).reshape(heads, t * width)
    skew = skew[:, :t * (width - 1)].reshape(heads, t, width - 1)
    causal = jnp.arange(t)[:, None] <= jnp.arange(t)[None, :]
    own = jnp.where(causal, skew[:, :, :t], NEG_INF)
    return jnp.stack([own, skew[:, :, t:]], axis=1)


def _const_tables(s):
    n_moba = s // MOBA_BLOCK
    avg = np.zeros((2 * SUBLANES, s), np.float32)
    for j in range(n_moba):
        avg[j, j * MOBA_BLOCK:(j + 1) * MOBA_BLOCK] = 1.0 / MOBA_BLOCK
    n_cmp = (s - NSA_CMP_BLOCK) // NSA_CMP_STRIDE + 1
    n_slc = s // NSA_SLC_BLOCK
    cmp_start = np.arange(n_cmp) * NSA_CMP_STRIDE
    slc_start = np.arange(n_slc) * NSA_SLC_BLOCK
    ov = np.clip(np.minimum(cmp_start[:, None] + NSA_CMP_BLOCK, slc_start[None, :] + NSA_SLC_BLOCK)
                 - np.maximum(cmp_start[:, None], slc_start[None, :]), 0, None) / NSA_CMP_BLOCK
    overlap_t = np.zeros((n_slc, s // NSA_CMP_STRIDE), np.float32)
    overlap_t[:, :n_cmp] = ov.T
    t = ATT_TILE
    win_far = np.where(np.arange(t)[:, None] > np.arange(t)[None, :], 0.0, NEG_INF).astype(np.float32)
    member = np.zeros((len(_MEMBER_BLOCKS), s, HEAD_DIM), np.float32)
    for kind, block in enumerate(_MEMBER_BLOCKS):
        assert s // block <= HEAD_DIM
        member[kind, np.arange(s), np.arange(s) // block] = 1.0
    return jnp.asarray(avg, BF16), jnp.asarray(overlap_t, BF16), jnp.asarray(win_far), jnp.asarray(member, BF16)


def kernel(x, rel_bias, norm_ffn1, ffn1_gate, ffn1_up, ffn1_down, norm_mix, w_in, diff_lambda, diff_subln, nsa_cmp_pe, nsa_cmp_w1, nsa_cmp_w2, w_out, norm_ffn2, ffn2_gate, ffn2_up, ffn2_down, final_norm):
    b, s, d = x.shape
    depth = w_in.shape[0]
    assert d == D_MODEL
    h0, h1 = MOBA_HEADS, MOBA_HEADS + DIFF_HEADS
    near = _bias_tiles(rel_bias.astype(F32))
    avg, overlap_t, win_far, member = _const_tables(s)
    near = near * LOG2E
    moba_bias = near[:h0]
    diff_bias = near[h0:h1]
    nsa_bias = jnp.concatenate([near[h1:], jnp.broadcast_to(win_far, near[h1:, :1].shape)], axis=1)
    nsa_bias = nsa_bias.reshape((NSA_KV_HEADS, NSA_GROUP) + nsa_bias.shape[1:])
    bf = lambda a: a.astype(BF16)
    ffn1_gate, ffn1_up, ffn1_down, ffn2_gate, ffn2_up, ffn2_down, w_in, w_out = (
        _to_bf16(w) for w in (ffn1_gate, ffn1_up, ffn1_down, ffn2_gate, ffn2_up, ffn2_down, w_in, w_out))
    cmp_w1 = _to_bf16(nsa_cmp_w1.reshape((-1,) + nsa_cmp_w1.shape[2:])).reshape(nsa_cmp_w1.shape)

    x = x.reshape(b * s, d)
    flat = lambda a: a.reshape(b * s, a.shape[-1])
    for l in range(depth):
        lambda_init = 0.8 - 0.6 * math.exp(-0.3 * l)
        x = _ffn(x, l, norm_ffn1[l], ffn1_gate, ffn1_up, ffn1_down)

        p = _inproj(x.reshape(b, s, d), norm_mix[l], *_inproj_weights(w_in[l]), member)
        o_moba = _moba(p["mq"], p["mk"], p["mvT"], moba_bias, avg)
        o_diff = _diff(p["dq"], p["dk"], p["dvT"], diff_bias, diff_lambda[l].astype(F32), diff_subln[l].astype(F32),
                       lambda_init)
        pe4 = nsa_cmp_pe[l].astype(F32).reshape(4, NSA_CMP_STRIDE * HEAD_DIM)
        k_cmp, v_cmp_t = _compress(p["kc"], p["vc"], pe4, cmp_w1[l], bf(nsa_cmp_w2[l, 0]), bf(nsa_cmp_w2[l, 1].T))
        o_nsa = _nsa(p["nq"], k_cmp, v_cmp_t, p["ks"], p["vsT"], p["kw"], p["vwT"], p["gateT"], nsa_bias, overlap_t)
        x = _ffn(x, l, norm_ffn2[l], ffn2_gate, ffn2_up, ffn2_down,
                 mix=(flat(o_moba), flat(o_diff), flat(o_nsa)), w_mix=w_out,
                 post_g=final_norm if l == depth - 1 else None)
    return x.reshape(b, s, d)
```

```python
import functools
import math

import numpy as np
import jax
import jax.numpy as jnp
from jax import lax
from jax.experimental import pallas as pl
from jax.experimental.pallas import tpu as pltpu

F32 = jnp.float32
BF16 = jnp.bfloat16

D_MODEL = 1024
HEAD_DIM = 64
MOBA_HEADS = 4
MOBA_BLOCK = 256
MOBA_TOPK = 3
DIFF_HEADS = 4
DIFF_QK_DIM = HEAD_DIM // 2
NSA_HEADS = 8
NSA_KV_HEADS = 2
NSA_GROUP = NSA_HEADS // NSA_KV_HEADS
NSA_CMP_BLOCK = 32
NSA_CMP_STRIDE = 16
NSA_CMP_HIDDEN = 256
NSA_SLC_BLOCK = 64
NSA_SLC_TOPK = 16
NSA_LOCAL_BLOCKS = 2
NSA_WINDOW = 512
REL_BUCKETS = 32
REL_MAX_DIST = 128
D_FF = 2816
RMS_EPS = 1e-6
NEG_INF = -1e30
FORCE_SCORE = 1e4

NSA_GATE_W = 3 * NSA_HEADS
NSA_GATE_ROWS = 16
LANES = 128
MXU_WIDTH = 256
SUBLANES = 8
ATT_TILE = 256
VMEM_LIMIT = 48 * 1024 * 1024
LOG2E = math.log2(math.e)

NT_DIMS = (((1,), (1,)), ((), ()))


def _rms(x, g):
    return x * lax.rsqrt(jnp.mean(x * x, axis=-1, keepdims=True) + RMS_EPS) * g


def _params(*sem):
    return pltpu.CompilerParams(dimension_semantics=sem, vmem_limit_bytes=VMEM_LIMIT)


def _cast_kernel(x_ref, o_ref):
    o_ref[...] = x_ref[...].astype(o_ref.dtype)


def _to_bf16(w, *, row_blocks=4):
    layers, rows, cols = w.shape
    br = rows // row_blocks
    assert rows % row_blocks == 0 and br % (2 * SUBLANES) == 0
    spec = pl.BlockSpec((1, br, cols), lambda l, r: (l, r, 0))
    return pl.pallas_call(
        _cast_kernel,
        grid=(layers, row_blocks),
        in_specs=[spec],
        out_specs=spec,
        out_shape=jax.ShapeDtypeStruct(w.shape, BF16),
        compiler_params=_params("parallel", "parallel"),
        name="to_bf16",
    )(w)


def _ffn_kernel(*refs, tf, n_mix, post_norm):
    x_ref, refs = refs[0], refs[1:]
    mix_refs, refs = refs[:n_mix], refs[n_mix:]
    if n_mix:
        wmix_ref, refs = refs[0], refs[1:]
    g_ref, wg_ref, wu_ref, wd_ref, refs = refs[0], refs[1], refs[2], refs[3], refs[4:]
    if post_norm:
        post_ref, refs = refs[0], refs[1:]
    o_ref, h_ref = refs

    y = x_ref[...]
    row = 0
    for m_ref in mix_refs:
        y = y + jnp.dot(m_ref[...], wmix_ref[row:row + m_ref.shape[1]], preferred_element_type=F32)
        row += m_ref.shape[1]
    o_ref[...] = y
    xn = _rms(y, g_ref[...]).astype(BF16)
    for f in range(h_ref.shape[1] // tf):
        cols = slice(f * tf, (f + 1) * tf)
        gate = jnp.dot(xn, wg_ref[:, cols], preferred_element_type=F32)
        up = jnp.dot(xn, wu_ref[:, cols], preferred_element_type=F32)
        h_ref[:, cols] = (gate * jax.nn.sigmoid(gate) * up).astype(BF16)
    o = o_ref[...] + 0.5 * jnp.dot(h_ref[...], wd_ref[...], preferred_element_type=F32)
    o_ref[...] = _rms(o, post_ref[...]) if post_norm else o


def _ffn(x2, layer, g, wg, wu, wd, mix=(), w_mix=None, post_g=None, *, tm=1024, tf=256):
    n, d = x2.shape
    ff = wg.shape[2]
    resident = dict(pipeline_mode=pl.Buffered(1))
    row_spec = lambda a: pl.BlockSpec((tm, a.shape[1]), lambda i: (i, 0))
    whole = lambda a: pl.BlockSpec((None,) + a.shape[1:], lambda i: (layer, 0, 0), **resident)
    gain = lambda a: a.reshape(1, d)
    args = [x2, *mix] + ([w_mix] if mix else []) + [gain(g), wg, wu, wd] + ([gain(post_g)] if post_g is not None else [])
    specs = ([row_spec(x2)] + [row_spec(m) for m in mix] + ([whole(w_mix)] if mix else [])
             + [pl.BlockSpec((1, d), lambda i: (0, 0)), whole(wg), whole(wu), whole(wd)]
             + ([pl.BlockSpec((1, d), lambda i: (0, 0))] if post_g is not None else []))
    return pl.pallas_call(
        functools.partial(_ffn_kernel, tf=tf, n_mix=len(mix), post_norm=post_g is not None),
        grid=(n // tm,),
        in_specs=specs,
        out_specs=row_spec(x2),
        out_shape=jax.ShapeDtypeStruct((n, d), F32),
        scratch_shapes=[pltpu.VMEM((tm, ff), BF16)],
        compiler_params=_params("parallel"),
        name="ffn",
    )(*args)


_ROW_OUTS = (
    ("mk", 256, 4, BF16, 1.0, 0),
    ("dk", 1024, 4, BF16, 1.0, None),
    ("kc", 2048, 2, F32, 1.0, None),
    ("vc", 2176, 2, F32, 1.0, None),
    ("ks", 2304, 2, BF16, 1.0, 1),
    ("kw", 2560, 2, BF16, 1.0, None),
)
_MEMBER_BLOCKS = (MOBA_BLOCK, NSA_SLC_BLOCK)
_COL_OUTS = (
    ("mqT", 0, 256, BF16, HEAD_DIM ** -0.5 * LOG2E),
    ("dqT", 768, 256, BF16, DIFF_QK_DIM ** -0.5 * LOG2E),
    ("nqT", 1536, 512, BF16, HEAD_DIM ** -0.5 * LOG2E),
    ("mvT", 512, 256, BF16, 1.0),
    ("dvT", 1280, 256, BF16, 1.0),
    ("vsT", 2432, 128, BF16, 1.0),
    ("vwT", 2688, 128, BF16, 1.0),
    ("gateT", None, NSA_KV_HEADS * NSA_GATE_ROWS, F32, 1.0),
)


def _inproj_kernel(x_ref, g_ref, w_ref, wt_ref, member_ref, *out_refs):
    xn = _rms(x_ref[0], g_ref[...]).astype(BF16)
    col, pending = 0, []
    for out, o_ref in zip(_ROW_OUTS, out_refs):
        pending.append((out, o_ref))
        width = sum(o[2] for o, _ in pending) * HEAD_DIM
        if width % MXU_WIDTH and out is not _ROW_OUTS[-1]:
            continue
        p = jnp.dot(xn, w_ref[:, col:col + width], preferred_element_type=F32)
        col += width
        first = 0
        for (_, _, heads, dtype, scale, member), ref in pending:
            for h in range(first, first + heads):
                head = p[:, h * HEAD_DIM:(h + 1) * HEAD_DIM]
                head = (head if scale == 1.0 else head * scale).astype(dtype)
                ref[0, h - first] = head if member is None else jnp.concatenate([head, member_ref[member]], axis=1)
            first += heads
        pending = []
    pt = lax.dot_general(wt_ref[...], xn, NT_DIMS, preferred_element_type=F32)
    row = 0
    for (_, _, rows, dtype, scale), o_ref in zip(_COL_OUTS, out_refs[len(_ROW_OUTS):]):
        part = pt[row:row + rows]
        o_ref[0] = (part if scale == 1.0 else part * scale).astype(dtype)
        row += rows


def _inproj_weights(w_l):
    w_rows = jnp.concatenate([w_l[:, c:c + h * HEAD_DIM] for _, c, h, _, _, _ in _ROW_OUTS], axis=1)
    gate_cols = w_l[:, w_l.shape[1] - NSA_GATE_W:]
    per_group = 3 * NSA_GROUP
    gate_t = jnp.zeros((NSA_KV_HEADS * NSA_GATE_ROWS, w_l.shape[0]), w_l.dtype)
    for g in range(NSA_KV_HEADS):
        gate_t = gate_t.at[g * NSA_GATE_ROWS:g * NSA_GATE_ROWS + per_group].set(
            gate_cols[:, g * per_group:(g + 1) * per_group].T)
    w_cols = jnp.concatenate([w_l[:, c:c + r].T for _, c, r, _, _ in _COL_OUTS[:-1]] + [gate_t], axis=0)
    return w_rows.astype(BF16), w_cols.astype(BF16)


def _inproj(x, g, w_rows, w_cols, member, *, ts=512):
    b, s, d = x.shape
    widths = [HEAD_DIM if member is None else 2 * HEAD_DIM for _, _, _, _, _, member in _ROW_OUTS]
    out_shape = [jax.ShapeDtypeStruct((b, o[2], s, w), o[3]) for o, w in zip(_ROW_OUTS, widths)]
    out_specs = [pl.BlockSpec((1, o[2], ts, w), lambda bi, i: (bi, 0, i, 0)) for o, w in zip(_ROW_OUTS, widths)]
    out_shape += [jax.ShapeDtypeStruct((b, o[2], s), o[3]) for o in _COL_OUTS]
    out_specs += [pl.BlockSpec((1, o[2], ts), lambda bi, i: (bi, 0, i)) for o in _COL_OUTS]
    outs = pl.pallas_call(
        _inproj_kernel,
        grid=(b, s // ts),
        in_specs=[
            pl.BlockSpec((1, ts, d), lambda bi, i: (bi, i, 0)),
            pl.BlockSpec((1, d), lambda bi, i: (0, 0)),
            pl.BlockSpec(w_rows.shape, lambda bi, i: (0, 0)),
            pl.BlockSpec(w_cols.shape, lambda bi, i: (0, 0)),
            pl.BlockSpec((member.shape[0], ts, HEAD_DIM), lambda bi, i: (0, i, 0)),
        ],
        out_specs=out_specs,
        out_shape=out_shape,
        compiler_params=_params("parallel", "parallel"),
        name="inproj",
    )(x, g.reshape(1, d), w_rows, w_cols, member)
    names = [o[0] for o in _ROW_OUTS] + [o[0] for o in _COL_OUTS]
    return dict(zip(names, outs))


ONES_ROWS = 16
EXP_ROWS = 32
TILES_PER_TRIP = 2
_ADD_TILE = {"own": 0, "near": 1, "edge": 2, "far": None}


def _tile_schedule(n_qt, window=None):
    tiles = [("own", i, i) for i in range(n_qt)] + [("near", i, i - 1) for i in range(1, n_qt)]
    if window is None:
        tiles += [("far", i, j) for j in range(n_qt - 2) for i in range(j + 2, n_qt)]
    else:
        assert window == 2
        tiles += [("edge", i, i - window) for i in range(window, n_qt)]
    kinds = tuple(kind for kind, _, _ in tiles)
    return kinds, jnp.asarray(np.array([[i for _, i, _ in tiles], [j for _, _, j in tiles]], np.int32))


def _attend_tiles(kinds, tab_ref, score_inputs, value_tiles, m_ref, acc_ref, s_refs, p_refs):
    tq = m_ref.shape[-1]

    def scores(t, kind, slot):
        qi, kj = tab_ref[0, t], tab_ref[1, t]
        rows = pl.ds(pl.multiple_of(qi * SUBLANES, SUBLANES), SUBLANES)
        pend = []
        for ci, (k, q_t, adds) in enumerate(score_inputs(qi, kj, kind)):
            s = jnp.dot(k, q_t, preferred_element_type=F32)
            for a in adds:
                s = s + a
            s_refs[slot][ci] = s
            m_old = m_ref[ci, rows, :][0:1, :]
            m_new = jnp.maximum(m_old, jnp.max(s, axis=0, keepdims=True))
            m_ref[ci, rows, :] = jnp.broadcast_to(m_new, (SUBLANES, tq))
            pend.append((m_new, jnp.exp2(m_old - m_new)))
        return qi, kj, tuple(pend)

    def accumulate(pending, slot):
        qi, kj, pend = pending
        s_ref, p_ref = s_refs[slot], p_refs[slot]
        for ci, (v_t, (m_new, alpha)) in enumerate(zip(value_tiles(kj), pend)):
            tk = v_t.shape[1]
            for r in range(tk // EXP_ROWS):
                rows = slice(r * EXP_ROWS, (r + 1) * EXP_ROWS)
                p_ref[ci, rows, :] = jnp.exp2((s_ref[ci, rows, :] - m_new).astype(BF16))
            v_ext = jnp.concatenate([v_t, jnp.ones((ONES_ROWS, tk), BF16)], axis=0)
            acc_ref[ci, qi] = alpha * acc_ref[ci, qi] + jnp.dot(v_ext, p_ref[ci], preferred_element_type=F32)

    m_ref[...] = jnp.full(m_ref.shape, NEG_INF, F32)
    acc_ref[...] = jnp.zeros(acc_ref.shape, F32)
    n = len(kinds)
    pending = scores(0, kinds[0], 0)
    step = 0
    while step < n - 1:
        kind = kinds[step + 1]
        run = 1
        while step + run < n - 1 and kinds[step + run + 1] == kind:
            run += 1

        def group(u, pending, step=step, kind=kind):
            for d in range(TILES_PER_TRIP):
                nxt = scores(step + TILES_PER_TRIP * u + d + 1, kind, (step + d + 1) % 2)
                accumulate(pending, (step + d) % 2)
                pending = nxt
            return pending

        if run // TILES_PER_TRIP:
            pending = lax.fori_loop(0, run // TILES_PER_TRIP, group, pending)
            step += run - run % TILES_PER_TRIP
        for _ in range(run % TILES_PER_TRIP):
            nxt = scores(step + 1, kind, (step + 1) % 2)
            accumulate(pending, step % 2)
            pending = nxt
            step += 1
    accumulate(pending, (n - 1) % 2)


def _rows(ref, j, t=ATT_TILE):
    return ref[pl.ds(pl.multiple_of(j * t, t), t), :]


def _cols(ref, j, t=ATT_TILE):
    return ref[:, pl.ds(pl.multiple_of(j * t, t), t)]


def _normalized(acc, dv=HEAD_DIM):
    return acc[:dv] / acc[dv:dv + 1]


def _topk_rows(score, n_rows, topk):
    row_id = lax.broadcasted_iota(jnp.int32, score.shape, 0)
    rank = jnp.zeros(score.shape, F32)
    for c in range(n_rows):
        other = score[c:c + 1, :]
        beats = (other > score) | ((other == score) & (row_id > c))
        rank = rank + jnp.where(beats, 1.0, 0.0)
    return (rank < topk) & (row_id < n_rows)


def _extend_query(q_t, chosen):
    dh, tq = q_t.shape
    pen = jnp.where(chosen, 0.0, NEG_INF)
    pen = jnp.concatenate([pen, jnp.zeros((dh - pen.shape[0], tq), F32)], axis=0)
    return jnp.concatenate([q_t, pen.astype(q_t.dtype)], axis=0)


def _attention_scratch(n, n_qt, t, dv=HEAD_DIM):
    return [pltpu.VMEM((n, n_qt * SUBLANES, t), F32), pltpu.VMEM((n, n_qt, dv + ONES_ROWS, t), F32),
            pltpu.VMEM((n, t, t), F32), pltpu.VMEM((n, t, t), F32),
            pltpu.VMEM((n, t, t), BF16), pltpu.VMEM((n, t, t), BF16)]


SMEM_SPEC = pl.BlockSpec(memory_space=pltpu.SMEM)


def _moba_kernel(tab_ref, qt_ref, k_ref, vt_ref, bias_ref, avg_ref, o_ref, qx_ref, m_ref, acc_ref,
                 s0_ref, s1_ref, p0_ref, p1_ref, *, kinds):
    t = ATT_TILE
    s = qt_ref.shape[2]
    dh = HEAD_DIM
    n_blk = s // MOBA_BLOCK
    heads = range(MOBA_HEADS)
    blk_rows = -(-n_blk // SUBLANES) * SUBLANES
    ranked = min((MOBA_TOPK + 1) * MOBA_BLOCK, s)
    blk = lax.broadcasted_iota(jnp.int32, (blk_rows, s - ranked), 0)
    own = (ranked + lax.broadcasted_iota(jnp.int32, (blk_rows, s - ranked), 1)) // MOBA_BLOCK
    blk_t = lax.broadcasted_iota(jnp.int32, (blk_rows, t), 0)

    for h in heads:
        if ranked < s:
            k_mean = jnp.dot(avg_ref[...], k_ref[0, h], preferred_element_type=F32)[:, :HEAD_DIM]
            km_hi = k_mean.astype(BF16)
            km_lo = (k_mean - km_hi.astype(F32)).astype(BF16)
            q_t = qt_ref[0, h * dh:(h + 1) * dh, ranked:]
            gate = (jnp.dot(km_hi, q_t, preferred_element_type=F32)
                    + jnp.dot(km_lo, q_t, preferred_element_type=F32))
            gate = jnp.where(blk < own, gate[:blk_rows], NEG_INF)
            chosen = (_topk_rows(gate, n_blk, MOBA_TOPK) & (blk < own)) | (blk == own)
        for qi in range(s // t):
            cols = slice(qi * t, (qi + 1) * t)
            picked = blk_t <= qi * t // MOBA_BLOCK if qi * t < ranked else chosen[:, qi * t - ranked:(qi + 1) * t - ranked]
            qx_ref[h, :, cols] = _extend_query(qt_ref[0, h * dh:(h + 1) * dh, cols], picked)

    def score_inputs(qi, kj, kind):
        tile = _ADD_TILE[kind]
        return [(_rows(k_ref.at[0, h], kj), _cols(qx_ref.at[h], qi), [] if tile is None else [bias_ref[h, tile]])
                for h in heads]

    def value_tiles(kj):
        return [_cols(vt_ref.at[0, h * HEAD_DIM:(h + 1) * HEAD_DIM], kj) for h in heads]

    _attend_tiles(kinds, tab_ref, score_inputs, value_tiles, m_ref, acc_ref, (s0_ref, s1_ref), (p0_ref, p1_ref))

    def finish(qi, carry):
        o_t = jnp.concatenate([_normalized(acc_ref[h, qi]) for h in heads], axis=0)
        o_ref[0, pl.ds(pl.multiple_of(qi * t, t), t), :] = o_t.T.astype(o_ref.dtype)
        return carry

    lax.fori_loop(0, s // t, finish, 0)


def _moba(mq_t, mk, mv_t, bias_near, avg):
    b, h, s, _ = mk.shape
    dh = mq_t.shape[1] // h
    t = ATT_TILE
    assert MOBA_BLOCK == t and s % t == 0 and s // t <= avg.shape[0]
    kinds, table = _tile_schedule(s // t)
    return pl.pallas_call(
        functools.partial(_moba_kernel, kinds=kinds),
        grid=(b,),
        in_specs=[
            SMEM_SPEC,
            pl.BlockSpec((1, h * dh, s), lambda bi: (bi, 0, 0)),
            pl.BlockSpec((1, h, s, mk.shape[3]), lambda bi: (bi, 0, 0, 0)),
            pl.BlockSpec((1, h * dh, s), lambda bi: (bi, 0, 0)),
            pl.BlockSpec(bias_near.shape, lambda bi: (0, 0, 0, 0)),
            pl.BlockSpec(avg.shape, lambda bi: (0, 0)),
        ],
        out_specs=pl.BlockSpec((1, s, h * dh), lambda bi: (bi, 0, 0)),
        out_shape=jax.ShapeDtypeStruct((b, s, h * dh), BF16),
        scratch_shapes=[pltpu.VMEM((h, mk.shape[3], s), BF16)] + _attention_scratch(h, s // t, t),
        compiler_params=_params("parallel"),
        name="moba",
    )(table, mq_t, mk, mv_t, bias_near, avg)


def _diff_kernel(tab_ref, qt_ref, k_ref, vt_ref, bias_ref, lam_ref, subln_ref, o_ref, m_ref, acc_ref,
                 s0_ref, s1_ref, p0_ref, p1_ref, *, kinds, lambda_init):
    t = ATT_TILE
    heads = range(DIFF_HEADS)
    feature = lax.broadcasted_iota(jnp.int32, (HEAD_DIM, t), 0)
    lp = lam_ref[...]
    lam = (jnp.exp(jnp.sum(lp[0:1] * lp[1:2], axis=-1, keepdims=True))
           - jnp.exp(jnp.sum(lp[2:3] * lp[3:4], axis=-1, keepdims=True)) + lambda_init)

    def score_inputs(qi, kj, kind):
        tile = _ADD_TILE[kind]
        out = []
        for h in heads:
            k = _rows(k_ref.at[0, h], kj)
            q_t = _cols(qt_ref.at[0, h * HEAD_DIM:(h + 1) * HEAD_DIM], qi)
            adds = [] if tile is None else [bias_ref[h, tile]]
            out.append((k, jnp.where(feature < DIFF_QK_DIM, q_t, jnp.zeros_like(q_t)), adds))
            out.append((k, jnp.where(feature >= DIFF_QK_DIM, q_t, jnp.zeros_like(q_t)), adds))
        return out

    def value_tiles(kj):
        tiles = [_cols(vt_ref.at[0, h * HEAD_DIM:(h + 1) * HEAD_DIM], kj) for h in heads]
        return [tiles[h] for h in heads for _ in range(2)]

    _attend_tiles(kinds, tab_ref, score_inputs, value_tiles, m_ref, acc_ref, (s0_ref, s1_ref), (p0_ref, p1_ref))

    def finish(qi, carry):
        outs = []
        for h in heads:
            o = _normalized(acc_ref[2 * h, qi]) - lam * _normalized(acc_ref[2 * h + 1, qi])
            o = o * lax.rsqrt(jnp.mean(o * o, axis=0, keepdims=True) + RMS_EPS) * subln_ref[...]
            outs.append(o * (1.0 - lambda_init))
        o_ref[0, pl.ds(pl.multiple_of(qi * t, t), t), :] = jnp.concatenate(outs, axis=0).T.astype(o_ref.dtype)
        return carry

    lax.fori_loop(0, qt_ref.shape[2] // t, finish, 0)


def _diff(dq_t, dk, dv_t, bias_near, lam_params, subln_g, lambda_init):
    b, h, s, dh = dk.shape
    t = ATT_TILE
    kinds, table = _tile_schedule(s // t)
    return pl.pallas_call(
        functools.partial(_diff_kernel, kinds=kinds, lambda_init=lambda_init),
        grid=(b,),
        in_specs=[
            SMEM_SPEC,
            pl.BlockSpec((1, h * dh, s), lambda bi: (bi, 0, 0)),
            pl.BlockSpec((1, h, s, dh), lambda bi: (bi, 0, 0, 0)),
            pl.BlockSpec((1, h * dh, s), lambda bi: (bi, 0, 0)),
            pl.BlockSpec(bias_near.shape, lambda bi: (0, 0, 0, 0)),
            pl.BlockSpec(lam_params.shape, lambda bi: (0, 0)),
            pl.BlockSpec((dh, 1), lambda bi: (0, 0)),
        ],
        out_specs=pl.BlockSpec((1, s, h * dh), lambda bi: (bi, 0, 0)),
        out_shape=jax.ShapeDtypeStruct((b, s, h * dh), BF16),
        scratch_shapes=_attention_scratch(2 * h, s // t, t),
        compiler_params=_params("parallel"),
        name="diff",
    )(table, dq_t, dk, dv_t, bias_near, lam_params, subln_g.reshape(dh, 1))


def _compress_kernel(kc_ref, vc_ref, pe_ref, w1_ref, w2k_ref, w2vt_ref, ko_ref, vo_ref):
    half = NSA_CMP_STRIDE * HEAD_DIM

    n_chunk = kc_ref.shape[2] // NSA_CMP_STRIDE

    def hidden(t, c_ref):
        c = jnp.concatenate([c_ref[0, 0, pl.ds(l, n_chunk, stride=NSA_CMP_STRIDE), :] for l in range(NSA_CMP_STRIDE)], axis=1)
        top = (c + pe_ref[2 * t:2 * t + 1]).astype(BF16)
        bot = (c + pe_ref[2 * t + 1:2 * t + 2]).astype(BF16)
        a = jnp.dot(top, w1_ref[t, :half], preferred_element_type=F32)
        bm = jnp.dot(bot, w1_ref[t, half:], preferred_element_type=F32)
        hid = a + pltpu.roll(bm, bm.shape[0] - 1, 0)
        return jax.nn.gelu(hid).astype(BF16)

    ko_ref[0, 0] = jnp.dot(hidden(0, kc_ref), w2k_ref[...], preferred_element_type=F32)
    vo_ref[0, 0] = lax.dot_general(w2vt_ref[...], hidden(1, vc_ref), NT_DIMS, preferred_element_type=F32)


def _compress(kc, vc, pe4, w1, w2k, w2v_t):
    b, g, s, dh = kc.shape
    n_chunk = s // NSA_CMP_STRIDE
    spec_in = pl.BlockSpec((1, 1, s, dh), lambda bi, gi: (bi, gi, 0, 0))
    return pl.pallas_call(
        _compress_kernel,
        grid=(b, g),
        in_specs=[
            spec_in, spec_in,
            pl.BlockSpec(pe4.shape, lambda bi, gi: (0, 0)),
            pl.BlockSpec(w1.shape, lambda bi, gi: (0, 0, 0)),
            pl.BlockSpec(w2k.shape, lambda bi, gi: (0, 0)),
            pl.BlockSpec(w2v_t.shape, lambda bi, gi: (0, 0)),
        ],
        out_specs=[pl.BlockSpec((1, 1, n_chunk, dh), lambda bi, gi: (bi, gi, 0, 0)),
                   pl.BlockSpec((1, 1, dh, n_chunk), lambda bi, gi: (bi, gi, 0, 0))],
        out_shape=[jax.ShapeDtypeStruct((b, g, n_chunk, dh), F32), jax.ShapeDtypeStruct((b, g, dh, n_chunk), F32)],
        compiler_params=_params("parallel", "parallel"),
        name="nsa_compress",
    )(kc, vc, pe4, w1, w2k, w2v_t)


def _nsa_kernel(slc_tab_ref, win_tab_ref, qt_ref, kcmp_ref, vcmpt_ref, ks_ref, vst_ref, kw_ref, vwt_ref, gate_ref,
                bias_ref, overlap_ref, o_ref, qx_ref, ocmp_ref, m_ref, acc_ref, m2_ref, acc2_ref,
                s0_ref, s1_ref, p0_ref, p1_ref, *, slc_kinds, win_kinds):
    t = ATT_TILE
    heads = range(NSA_GROUP)
    s = qt_ref.shape[2]
    qt_refs = [qt_ref.at[0, r * HEAD_DIM:(r + 1) * HEAD_DIM] for r in heads]
    n_chunk = kcmp_ref.shape[2]
    n_slc = s // NSA_SLC_BLOCK
    ks_g, vst_g = ks_ref.at[0, 0], vst_ref.at[0]
    kw_g, vwt_g = kw_ref.at[0, 0], vwt_ref.at[0]
    k_cmp = kcmp_ref[0, 0].astype(BF16)
    v_cmp_t = vcmpt_ref[0, 0].astype(BF16)
    topk = min(NSA_SLC_TOPK, n_slc)
    width = min(topk * NSA_SLC_BLOCK, s)
    assert s % width == 0 and width % t == 0

    def select(q0):
        cmp_id = lax.broadcasted_iota(jnp.int32, (n_chunk, width), 0)
        q_pos = q0 + lax.broadcasted_iota(jnp.int32, (n_chunk, width), 1)
        cmp_valid = (cmp_id * NSA_CMP_STRIDE + (NSA_CMP_BLOCK - 1) <= q_pos) & (cmp_id < n_chunk - 1)
        p_sum = jnp.zeros((n_chunk, width), F32)
        for r in heads:
            sc = jnp.dot(k_cmp, qt_refs[r][:, q0:q0 + width], preferred_element_type=F32)
            m = jnp.max(jnp.where(cmp_valid, sc, NEG_INF), axis=0, keepdims=True)
            e = jnp.where(cmp_valid, jnp.exp2(sc - m), 0.0)
            l = jnp.sum(e, axis=0, keepdims=True)
            p = e / jnp.where(l > 0.0, l, 1.0)
            p_sum = p_sum + p
            o = jnp.dot(v_cmp_t, p.astype(BF16), preferred_element_type=F32)
            for u in range(width // t):
                ocmp_ref[r, q0 // t + u] = o[:, u * t:(u + 1) * t]
        blk = lax.broadcasted_iota(jnp.int32, (n_slc, width), 0)
        cur = (q0 + lax.broadcasted_iota(jnp.int32, (n_slc, width), 1)) // NSA_SLC_BLOCK
        if q0 + width <= topk * NSA_SLC_BLOCK:
            sel = blk <= cur
        else:
            ps_hi = p_sum.astype(BF16)
            ps_lo = (p_sum - ps_hi.astype(F32)).astype(BF16)
            imp = (jnp.dot(overlap_ref[...], ps_hi, preferred_element_type=F32)
                   + jnp.dot(overlap_ref[...], ps_lo, preferred_element_type=F32))
            forced = (blk == 0) | (blk > cur - NSA_LOCAL_BLOCKS)
            score = jnp.where(blk <= cur, jnp.where(forced, FORCE_SCORE, imp), NEG_INF)
            sel = _topk_rows(score, n_slc, topk)
        for u in range(width // t):
            cols = slice(q0 + u * t, q0 + (u + 1) * t)
            for r in heads:
                qx_ref[r, :, cols] = _extend_query(qt_refs[r][:, cols], sel[:, u * t:(u + 1) * t])

    for q0 in range(0, s, width):
        select(q0)

    def adds(r, kind):
        return [] if _ADD_TILE[kind] is None else [bias_ref[0, r, _ADD_TILE[kind]]]

    def slc_inputs(qi, kj, kind):
        return [(_rows(ks_g, kj), _cols(qx_ref.at[r], qi), adds(r, kind)) for r in heads]

    def win_inputs(qi, kj, kind):
        return [(_rows(kw_g, kj), _cols(qt_refs[r], qi), adds(r, kind)) for r in heads]

    bufs = ((s0_ref, s1_ref), (p0_ref, p1_ref))
    _attend_tiles(slc_kinds, slc_tab_ref, slc_inputs, lambda kj: [_cols(vst_g, kj)] * len(heads), m_ref, acc_ref, *bufs)
    _attend_tiles(win_kinds, win_tab_ref, win_inputs, lambda kj: [_cols(vwt_g, kj)] * len(heads), m2_ref, acc2_ref, *bufs)

    def finish(qi, carry):
        cols = pl.ds(pl.multiple_of(qi * t, t), t)
        gates = jax.nn.sigmoid(gate_ref[0, :, cols])
        outs = []
        for r in heads:
            g_cmp, g_slc, g_win = (gates[3 * r + br:3 * r + br + 1, :] for br in range(3))
            outs.append(g_cmp * ocmp_ref[r, qi] + g_slc * _normalized(acc_ref[r, qi]) + g_win * _normalized(acc2_ref[r, qi]))
        o_ref[0, cols, :] = jnp.concatenate(outs, axis=0).T.astype(o_ref.dtype)
        return carry

    lax.fori_loop(0, s // t, finish, 0)


def _nsa(nq_t, k_cmp, v_cmp_t, ks, vs_t, kw, vw_t, gate_t, bias_tiles, overlap_t):
    b, _, s, dh = kw.shape
    g, r, t = NSA_KV_HEADS, NSA_GROUP, ATT_TILE
    assert NSA_WINDOW == 2 * t and s % t == 0 and t % NSA_SLC_BLOCK == 0
    slc_kinds, slc_table = _tile_schedule(s // t)
    win_kinds, win_table = _tile_schedule(s // t, window=NSA_WINDOW // t)
    k_spec = lambda k: pl.BlockSpec((1, 1, s, k.shape[3]), lambda bi, gi: (bi, gi, 0, 0))
    vt_spec = pl.BlockSpec((1, dh, s), lambda bi, gi: (bi, gi, 0))
    scratch = _attention_scratch(r, s // t, t)
    return pl.pallas_call(
        functools.partial(_nsa_kernel, slc_kinds=slc_kinds, win_kinds=win_kinds),
        grid=(b, g),
        in_specs=[
            SMEM_SPEC, SMEM_SPEC,
            pl.BlockSpec((1, r * dh, s), lambda bi, gi: (bi, gi, 0)),
            pl.BlockSpec((1, 1) + k_cmp.shape[2:], lambda bi, gi: (bi, gi, 0, 0)),
            pl.BlockSpec((1, 1) + v_cmp_t.shape[2:], lambda bi, gi: (bi, gi, 0, 0)),
            k_spec(ks), vt_spec, k_spec(kw), vt_spec,
            pl.BlockSpec((1, NSA_GATE_ROWS, s), lambda bi, gi: (bi, gi, 0)),
            pl.BlockSpec((1,) + bias_tiles.shape[1:], lambda bi, gi: (gi, 0, 0, 0, 0)),
            pl.BlockSpec(overlap_t.shape, lambda bi, gi: (0, 0)),
        ],
        out_specs=pl.BlockSpec((1, s, r * dh), lambda bi, gi: (bi, 0, gi)),
        out_shape=jax.ShapeDtypeStruct((b, s, g * r * dh), BF16),
        scratch_shapes=[pltpu.VMEM((r, ks.shape[3], s), BF16), pltpu.VMEM((r, s // t, dh, t), F32)]
                       + scratch[:2] + scratch,
        compiler_params=_params("parallel", "parallel"),
        name="nsa",
    )(slc_table, win_table, nq_t, k_cmp, v_cmp_t, ks, vs_t, kw, vw_t, gate_t, bias_tiles, overlap_t)


def _rel_bucket(dist):
    n = jnp.maximum(dist, 0)
    max_exact = REL_BUCKETS // 2
    n_f = jnp.maximum(n, max_exact).astype(F32)
    large = max_exact + (jnp.log(n_f / max_exact) / math.log(REL_MAX_DIST / max_exact)
                         * (REL_BUCKETS - max_exact)).astype(jnp.int32)
    return jnp.where(n < max_exact, n, jnp.minimum(large, REL_BUCKETS - 1))


def _bias_tiles(rel_bias):
    t = ATT_TILE
    assert t >= REL_MAX_DIST
    heads = rel_bias.shape[1]
    width = 2 * t + 1
    by_dist = rel_bias[_rel_bucket(jnp.arange(width))] - rel_bias[REL_BUCKETS - 1]
    skew = jnp.broadcast_to(by_dist.T[:, None, :], (heads, t, width)).reshape(heads, t * width)
    skew = skew[:, :t * (width - 1)].reshape(heads, t, width - 1)
    causal = jnp.arange(t)[:, None] <= jnp.arange(t)[None, :]
    own = jnp.where(causal, skew[:, :, :t], NEG_INF)
    return jnp.stack([own, skew[:, :, t:]], axis=1)


def _const_tables(s):
    n_moba = s // MOBA_BLOCK
    avg = np.zeros((2 * SUBLANES, s), np.float32)
    for j in range(n_moba):
        avg[j, j * MOBA_BLOCK:(j + 1) * MOBA_BLOCK] = 1.0 / MOBA_BLOCK
    n_cmp = (s - NSA_CMP_BLOCK) // NSA_CMP_STRIDE + 1
    n_slc = s // NSA_SLC_BLOCK
    cmp_start = np.arange(n_cmp) * NSA_CMP_STRIDE
    slc_start = np.arange(n_slc) * NSA_SLC_BLOCK
    ov = np.clip(np.minimum(cmp_start[:, None] + NSA_CMP_BLOCK, slc_start[None, :] + NSA_SLC_BLOCK)
                 - np.maximum(cmp_start[:, None], slc_start[None, :]), 0, None) / NSA_CMP_BLOCK
    overlap_t = np.zeros((n_slc, s // NSA_CMP_STRIDE), np.float32)
    overlap_t[:, :n_cmp] = ov.T
    t = ATT_TILE
    win_far = np.where(np.arange(t)[:, None] > np.arange(t)[None, :], 0.0, NEG_INF).astype(np.float32)
    member = np.zeros((len(_MEMBER_BLOCKS), s, HEAD_DIM), np.float32)
    for kind, block in enumerate(_MEMBER_BLOCKS):
        assert s // block <= HEAD_DIM
        member[kind, np.arange(s), np.arange(s) // block] = 1.0
    return jnp.asarray(avg, BF16), jnp.asarray(overlap_t, BF16), jnp.asarray(win_far), jnp.asarray(member, BF16)


def kernel(x, rel_bias, norm_ffn1, ffn1_gate, ffn1_up, ffn1_down, norm_mix, w_in, diff_lambda, diff_subln, nsa_cmp_pe, nsa_cmp_w1, nsa_cmp_w2, w_out, norm_ffn2, ffn2_gate, ffn2_up, ffn2_down, final_norm):
    b, s, d = x.shape
    depth = w_in.shape[0]
    assert d == D_MODEL
    h0, h1 = MOBA_HEADS, MOBA_HEADS + DIFF_HEADS
    near = _bias_tiles(rel_bias.astype(F32))
    avg, overlap_t, win_far, member = _const_tables(s)
    near = near * LOG2E
    moba_bias = near[:h0]
    diff_bias = near[h0:h1]
    nsa_bias = jnp.concatenate([near[h1:], jnp.broadcast_to(win_far, near[h1:, :1].shape)], axis=1)
    nsa_bias = nsa_bias.reshape((NSA_KV_HEADS, NSA_GROUP) + nsa_bias.shape[1:])
    bf = lambda a: a.astype(BF16)
    ffn1_gate, ffn1_up, ffn1_down, ffn2_gate, ffn2_up, ffn2_down, w_in, w_out = (
        _to_bf16(w) for w in (ffn1_gate, ffn1_up, ffn1_down, ffn2_gate, ffn2_up, ffn2_down, w_in, w_out))
    cmp_w1 = _to_bf16(nsa_cmp_w1.reshape((-1,) + nsa_cmp_w1.shape[2:])).reshape(nsa_cmp_w1.shape)

    x = x.reshape(b * s, d)
    flat = lambda a: a.reshape(b * s, a.shape[-1])
    for l in range(depth):
        lambda_init = 0.8 - 0.6 * math.exp(-0.3 * l)
        x = _ffn(x, l, norm_ffn1[l], ffn1_gate, ffn1_up, ffn1_down)

        p = _inproj(x.reshape(b, s, d), norm_mix[l], *_inproj_weights(w_in[l]), member)
        o_moba = _moba(p["mqT"], p["mk"], p["mvT"], moba_bias, avg)
        o_diff = _diff(p["dqT"], p["dk"], p["dvT"], diff_bias, diff_lambda[l].astype(F32), diff_subln[l].astype(F32),
                       lambda_init)
        pe4 = nsa_cmp_pe[l].astype(F32).reshape(4, NSA_CMP_STRIDE * HEAD_DIM)
        k_cmp, v_cmp_t = _compress(p["kc"], p["vc"], pe4, cmp_w1[l], bf(nsa_cmp_w2[l, 0]), bf(nsa_cmp_w2[l, 1].T))
        o_nsa = _nsa(p["nqT"], k_cmp, v_cmp_t, p["ks"], p["vsT"], p["kw"], p["vwT"], p["gateT"], nsa_bias, overlap_t)
        x = _ffn(x, l, norm_ffn2[l], ffn2_gate, ffn2_up, ffn2_down,
                 mix=(flat(o_moba), flat(o_diff), flat(o_nsa)), w_mix=w_out,
                 post_g=final_norm if l == depth - 1 else None)
    return x.reshape(b, s, d)
```

```python
import functools
import math

import numpy as np
import jax
import jax.numpy as jnp
from jax import lax
from jax.experimental import pallas as pl
from jax.experimental.pallas import tpu as pltpu

F32 = jnp.float32
BF16 = jnp.bfloat16

D_MODEL = 1024
HEAD_DIM = 64
MOBA_HEADS = 4
MOBA_BLOCK = 256
MOBA_TOPK = 3
DIFF_HEADS = 4
DIFF_QK_DIM = HEAD_DIM // 2
NSA_HEADS = 8
NSA_KV_HEADS = 2
NSA_GROUP = NSA_HEADS // NSA_KV_HEADS
NSA_CMP_BLOCK = 32
NSA_CMP_STRIDE = 16
NSA_CMP_HIDDEN = 256
NSA_SLC_BLOCK = 64
NSA_SLC_TOPK = 16
NSA_LOCAL_BLOCKS = 2
NSA_WINDOW = 512
REL_BUCKETS = 32
REL_MAX_DIST = 128
D_FF = 2816
RMS_EPS = 1e-6
NEG_INF = -1e30
FORCE_SCORE = 1e4

NSA_GATE_W = 3 * NSA_HEADS
NSA_GATE_ROWS = 16
LANES = 128
MXU_WIDTH = 256
SUBLANES = 8
ATT_TILE = 256
VMEM_LIMIT = 48 * 1024 * 1024
LOG2E = math.log2(math.e)

NT_DIMS = (((1,), (1,)), ((), ()))


def _rms(x, g):
    return x * lax.rsqrt(jnp.mean(x * x, axis=-1, keepdims=True) + RMS_EPS) * g


def _params(*sem):
    return pltpu.CompilerParams(dimension_semantics=sem, vmem_limit_bytes=VMEM_LIMIT)


def _cast_kernel(x_ref, o_ref):
    o_ref[...] = x_ref[...].astype(o_ref.dtype)


def _to_bf16(w, *, row_blocks=4):
    layers, rows, cols = w.shape
    br = rows // row_blocks
    assert rows % row_blocks == 0 and br % (2 * SUBLANES) == 0
    spec = pl.BlockSpec((1, br, cols), lambda l, r: (l, r, 0))
    return pl.pallas_call(
        _cast_kernel,
        grid=(layers, row_blocks),
        in_specs=[spec],
        out_specs=spec,
        out_shape=jax.ShapeDtypeStruct(w.shape, BF16),
        compiler_params=_params("parallel", "parallel"),
        name="to_bf16",
    )(w)


def _ffn_kernel(*refs, tf, n_mix, post_norm):
    x_ref, refs = refs[0], refs[1:]
    mix_refs, refs = refs[:n_mix], refs[n_mix:]
    if n_mix:
        wmix_ref, refs = refs[0], refs[1:]
    g_ref, wg_ref, wu_ref, wd_ref, refs = refs[0], refs[1], refs[2], refs[3], refs[4:]
    if post_norm:
        post_ref, refs = refs[0], refs[1:]
    o_ref, h_ref = refs

    y = x_ref[...]
    row = 0
    for m_ref in mix_refs:
        y = y + jnp.dot(m_ref[...], wmix_ref[row:row + m_ref.shape[1]], preferred_element_type=F32)
        row += m_ref.shape[1]
    o_ref[...] = y
    xn = _rms(y, g_ref[...]).astype(BF16)
    for f in range(h_ref.shape[1] // tf):
        cols = slice(f * tf, (f + 1) * tf)
        gate = jnp.dot(xn, wg_ref[:, cols], preferred_element_type=F32)
        up = jnp.dot(xn, wu_ref[:, cols], preferred_element_type=F32)
        h_ref[:, cols] = (gate * jax.nn.sigmoid(gate) * up).astype(BF16)
    o = o_ref[...] + 0.5 * jnp.dot(h_ref[...], wd_ref[...], preferred_element_type=F32)
    o_ref[...] = _rms(o, post_ref[...]) if post_norm else o


def _ffn(x2, layer, g, wg, wu, wd, mix=(), w_mix=None, post_g=None, *, tm=1024, tf=256):
    n, d = x2.shape
    ff = wg.shape[2]
    resident = dict(pipeline_mode=pl.Buffered(1))
    row_spec = lambda a: pl.BlockSpec((tm, a.shape[1]), lambda i: (i, 0))
    whole = lambda a: pl.BlockSpec((None,) + a.shape[1:], lambda i: (layer, 0, 0), **resident)
    gain = lambda a: a.reshape(1, d)
    args = [x2, *mix] + ([w_mix] if mix else []) + [gain(g), wg, wu, wd] + ([gain(post_g)] if post_g is not None else [])
    specs = ([row_spec(x2)] + [row_spec(m) for m in mix] + ([whole(w_mix)] if mix else [])
             + [pl.BlockSpec((1, d), lambda i: (0, 0)), whole(wg), whole(wu), whole(wd)]
             + ([pl.BlockSpec((1, d), lambda i: (0, 0))] if post_g is not None else []))
    return pl.pallas_call(
        functools.partial(_ffn_kernel, tf=tf, n_mix=len(mix), post_norm=post_g is not None),
        grid=(n // tm,),
        in_specs=specs,
        out_specs=row_spec(x2),
        out_shape=jax.ShapeDtypeStruct((n, d), F32),
        scratch_shapes=[pltpu.VMEM((tm, ff), BF16)],
        compiler_params=_params("parallel"),
        name="ffn",
    )(*args)


_ROW_OUTS = (
    ("mk", 256, 4, BF16, 1.0, 0),
    ("dk", 1024, 4, BF16, 1.0, None),
    ("kc", 2048, 2, F32, 1.0, None),
    ("vc", 2176, 2, F32, 1.0, None),
    ("ks", 2304, 2, BF16, 1.0, 1),
    ("kw", 2560, 2, BF16, 1.0, None),
)
_MEMBER_BLOCKS = (MOBA_BLOCK, NSA_SLC_BLOCK)
_COL_OUTS = (
    ("mqT", 0, 256, BF16, HEAD_DIM ** -0.5 * LOG2E),
    ("dqT", 768, 256, BF16, DIFF_QK_DIM ** -0.5 * LOG2E),
    ("nqT", 1536, 512, BF16, HEAD_DIM ** -0.5 * LOG2E),
    ("mvT", 512, 256, BF16, 1.0),
    ("dvT", 1280, 256, BF16, 1.0),
    ("vsT", 2432, 128, BF16, 1.0),
    ("vwT", 2688, 128, BF16, 1.0),
    ("gateT", None, NSA_KV_HEADS * NSA_GATE_ROWS, F32, 1.0),
)


def _inproj_kernel(x_ref, g_ref, w_ref, wt_ref, member_ref, *out_refs):
    xn = _rms(x_ref[0], g_ref[...]).astype(BF16)
    col, pending = 0, []
    for out, o_ref in zip(_ROW_OUTS, out_refs):
        pending.append((out, o_ref))
        width = sum(o[2] for o, _ in pending) * HEAD_DIM
        if width % MXU_WIDTH and out is not _ROW_OUTS[-1]:
            continue
        p = jnp.dot(xn, w_ref[:, col:col + width], preferred_element_type=F32)
        col += width
        first = 0
        for (_, _, heads, dtype, scale, member), ref in pending:
            for h in range(first, first + heads):
                head = p[:, h * HEAD_DIM:(h + 1) * HEAD_DIM]
                head = (head if scale == 1.0 else head * scale).astype(dtype)
                ref[0, h - first] = head if member is None else jnp.concatenate([head, member_ref[member]], axis=1)
            first += heads
        pending = []
    pt = lax.dot_general(wt_ref[...], xn, NT_DIMS, preferred_element_type=F32)
    row = 0
    for (_, _, rows, dtype, scale), o_ref in zip(_COL_OUTS, out_refs[len(_ROW_OUTS):]):
        part = pt[row:row + rows]
        o_ref[0] = (part if scale == 1.0 else part * scale).astype(dtype)
        row += rows


def _inproj_weights(w_l):
    w_rows = jnp.concatenate([w_l[:, c:c + h * HEAD_DIM] for _, c, h, _, _, _ in _ROW_OUTS], axis=1)
    gate_cols = w_l[:, w_l.shape[1] - NSA_GATE_W:]
    per_group = 3 * NSA_GROUP
    gate_t = jnp.zeros((NSA_KV_HEADS * NSA_GATE_ROWS, w_l.shape[0]), w_l.dtype)
    for g in range(NSA_KV_HEADS):
        gate_t = gate_t.at[g * NSA_GATE_ROWS:g * NSA_GATE_ROWS + per_group].set(
            gate_cols[:, g * per_group:(g + 1) * per_group].T)
    w_cols = jnp.concatenate([w_l[:, c:c + r].T for _, c, r, _, _ in _COL_OUTS[:-1]] + [gate_t], axis=0)
    return w_rows.astype(BF16), w_cols.astype(BF16)


def _inproj(x, g, w_rows, w_cols, member, *, ts=512):
    b, s, d = x.shape
    widths = [HEAD_DIM if member is None else 2 * HEAD_DIM for _, _, _, _, _, member in _ROW_OUTS]
    out_shape = [jax.ShapeDtypeStruct((b, o[2], s, w), o[3]) for o, w in zip(_ROW_OUTS, widths)]
    out_specs = [pl.BlockSpec((1, o[2], ts, w), lambda bi, i: (bi, 0, i, 0)) for o, w in zip(_ROW_OUTS, widths)]
    out_shape += [jax.ShapeDtypeStruct((b, o[2], s), o[3]) for o in _COL_OUTS]
    out_specs += [pl.BlockSpec((1, o[2], ts), lambda bi, i: (bi, 0, i)) for o in _COL_OUTS]
    outs = pl.pallas_call(
        _inproj_kernel,
        grid=(b, s // ts),
        in_specs=[
            pl.BlockSpec((1, ts, d), lambda bi, i: (bi, i, 0)),
            pl.BlockSpec((1, d), lambda bi, i: (0, 0)),
            pl.BlockSpec(w_rows.shape, lambda bi, i: (0, 0)),
            pl.BlockSpec(w_cols.shape, lambda bi, i: (0, 0)),
            pl.BlockSpec((member.shape[0], ts, HEAD_DIM), lambda bi, i: (0, i, 0)),
        ],
        out_specs=out_specs,
        out_shape=out_shape,
        compiler_params=_params("parallel", "parallel"),
        name="inproj",
    )(x, g.reshape(1, d), w_rows, w_cols, member)
    names = [o[0] for o in _ROW_OUTS] + [o[0] for o in _COL_OUTS]
    return dict(zip(names, outs))


ONES_ROWS = 16
EXP_ROWS = 32
TILES_PER_TRIP = 2
_ADD_TILE = {"own": 0, "near": 1, "edge": 2, "far": None}


def _tile_schedule(n_qt, window=None):
    runs = [("own", 0, n_qt), ("near", 1, n_qt - 1)]
    if window is None:
        runs += [("far", d, n_qt - d) for d in range(2, n_qt)]
    else:
        assert window == 2
        runs += [("edge", window, n_qt - window)]
    return tuple(run for run in runs if run[2] > 0)


def _attend_tiles(runs, score_inputs, value_tiles, m_ref, acc_ref, s_refs, p_refs):
    tq = m_ref.shape[-1]

    def scores(t, run, slot):
        kind, distance, first = run
        kj = jnp.asarray(t - first, jnp.int32)
        qi = kj + distance
        rows = pl.ds(pl.multiple_of(qi * SUBLANES, SUBLANES), SUBLANES)
        pend = []
        for ci, (k, q_t, adds) in enumerate(score_inputs(qi, kj, kind)):
            s = jnp.dot(k, q_t, preferred_element_type=F32)
            for a in adds:
                s = s + a
            s_refs[slot][ci] = s
            m_old = m_ref[ci, rows, :][0:1, :]
            m_new = jnp.maximum(m_old, jnp.max(s, axis=0, keepdims=True))
            m_ref[ci, rows, :] = jnp.broadcast_to(m_new, (SUBLANES, tq))
            pend.append((m_new, jnp.exp2(m_old - m_new)))
        return qi, kj, tuple(pend)

    def accumulate(pending, slot):
        qi, kj, pend = pending
        s_ref, p_ref = s_refs[slot], p_refs[slot]
        for ci, (v_t, (m_new, alpha)) in enumerate(zip(value_tiles(kj), pend)):
            tk = v_t.shape[1]
            for r in range(tk // EXP_ROWS):
                rows = slice(r * EXP_ROWS, (r + 1) * EXP_ROWS)
                p_ref[ci, rows, :] = jnp.exp2((s_ref[ci, rows, :] - m_new).astype(BF16))
            v_ext = jnp.concatenate([v_t, jnp.ones((ONES_ROWS, tk), BF16)], axis=0)
            acc_ref[ci, qi] = alpha * acc_ref[ci, qi] + jnp.dot(v_ext, p_ref[ci], preferred_element_type=F32)

    m_ref[...] = jnp.full(m_ref.shape, NEG_INF, F32)
    acc_ref[...] = jnp.zeros(acc_ref.shape, F32)
    tile_run = []
    for kind, distance, count in runs:
        tile_run += [(kind, distance, len(tile_run))] * count
    n = len(tile_run)
    pending = scores(0, tile_run[0], 0)
    step = 0
    while step < n - 1:
        run = tile_run[step + 1]
        length = 1
        while step + length < n - 1 and tile_run[step + length + 1] == run:
            length += 1

        def group(u, pending, step=step, run=run):
            for d in range(TILES_PER_TRIP):
                nxt = scores(step + TILES_PER_TRIP * u + d + 1, run, (step + d + 1) % 2)
                accumulate(pending, (step + d) % 2)
                pending = nxt
            return pending

        if length // TILES_PER_TRIP:
            pending = lax.fori_loop(0, length // TILES_PER_TRIP, group, pending)
            step += length - length % TILES_PER_TRIP
        for _ in range(length % TILES_PER_TRIP):
            nxt = scores(step + 1, run, (step + 1) % 2)
            accumulate(pending, step % 2)
            pending = nxt
            step += 1
    accumulate(pending, (n - 1) % 2)


def _rows(ref, j, t=ATT_TILE):
    return ref[pl.ds(pl.multiple_of(j * t, t), t), :]


def _cols(ref, j, t=ATT_TILE):
    return ref[:, pl.ds(pl.multiple_of(j * t, t), t)]


def _normalized(acc, dv=HEAD_DIM):
    return acc[:dv] / acc[dv:dv + 1]


def _topk_rows(score, n_rows, topk):
    row_id = lax.broadcasted_iota(jnp.int32, score.shape, 0)
    rank = jnp.zeros(score.shape, F32)
    for c in range(n_rows):
        other = score[c:c + 1, :]
        beats = (other > score) | ((other == score) & (row_id > c))
        rank = rank + jnp.where(beats, 1.0, 0.0)
    return (rank < topk) & (row_id < n_rows)


def _extend_query(q_t, chosen):
    dh, tq = q_t.shape
    pen = jnp.where(chosen, 0.0, NEG_INF)
    pen = jnp.concatenate([pen, jnp.zeros((dh - pen.shape[0], tq), F32)], axis=0)
    return jnp.concatenate([q_t, pen.astype(q_t.dtype)], axis=0)


def _attention_scratch(n, n_qt, t, dv=HEAD_DIM):
    return [pltpu.VMEM((n, n_qt * SUBLANES, t), F32), pltpu.VMEM((n, n_qt, dv + ONES_ROWS, t), F32),
            pltpu.VMEM((n, t, t), F32), pltpu.VMEM((n, t, t), F32),
            pltpu.VMEM((n, t, t), BF16), pltpu.VMEM((n, t, t), BF16)]


def _moba_kernel(qt_ref, k_ref, vt_ref, bias_ref, avg_ref, o_ref, qx_ref, m_ref, acc_ref,
                 s0_ref, s1_ref, p0_ref, p1_ref, *, runs):
    t = ATT_TILE
    s = qt_ref.shape[2]
    dh = HEAD_DIM
    n_blk = s // MOBA_BLOCK
    heads = range(MOBA_HEADS)
    blk_rows = -(-n_blk // SUBLANES) * SUBLANES
    ranked = min((MOBA_TOPK + 1) * MOBA_BLOCK, s)
    blk = lax.broadcasted_iota(jnp.int32, (blk_rows, s - ranked), 0)
    own = (ranked + lax.broadcasted_iota(jnp.int32, (blk_rows, s - ranked), 1)) // MOBA_BLOCK
    blk_t = lax.broadcasted_iota(jnp.int32, (blk_rows, t), 0)

    for h in heads:
        if ranked < s:
            k_mean = jnp.dot(avg_ref[...], k_ref[0, h], preferred_element_type=F32)[:, :HEAD_DIM]
            km_hi = k_mean.astype(BF16)
            km_lo = (k_mean - km_hi.astype(F32)).astype(BF16)
            q_t = qt_ref[0, h * dh:(h + 1) * dh, ranked:]
            gate = (jnp.dot(km_hi, q_t, preferred_element_type=F32)
                    + jnp.dot(km_lo, q_t, preferred_element_type=F32))
            gate = jnp.where(blk < own, gate[:blk_rows], NEG_INF)
            chosen = (_topk_rows(gate, n_blk, MOBA_TOPK) & (blk < own)) | (blk == own)
        for qi in range(s // t):
            cols = slice(qi * t, (qi + 1) * t)
            picked = blk_t <= qi * t // MOBA_BLOCK if qi * t < ranked else chosen[:, qi * t - ranked:(qi + 1) * t - ranked]
            qx_ref[h, :, cols] = _extend_query(qt_ref[0, h * dh:(h + 1) * dh, cols], picked)

    def score_inputs(qi, kj, kind):
        tile = _ADD_TILE[kind]
        return [(_rows(k_ref.at[0, h], kj), _cols(qx_ref.at[h], qi), [] if tile is None else [bias_ref[h, tile]])
                for h in heads]

    def value_tiles(kj):
        return [_cols(vt_ref.at[0, h * HEAD_DIM:(h + 1) * HEAD_DIM], kj) for h in heads]

    _attend_tiles(runs, score_inputs, value_tiles, m_ref, acc_ref, (s0_ref, s1_ref), (p0_ref, p1_ref))

    def finish(qi, carry):
        o_t = jnp.concatenate([_normalized(acc_ref[h, qi]) for h in heads], axis=0)
        o_ref[0, pl.ds(pl.multiple_of(qi * t, t), t), :] = o_t.T.astype(o_ref.dtype)
        return carry

    lax.fori_loop(0, s // t, finish, 0)


def _moba(mq_t, mk, mv_t, bias_near, avg):
    b, h, s, _ = mk.shape
    dh = mq_t.shape[1] // h
    t = ATT_TILE
    assert MOBA_BLOCK == t and s % t == 0 and s // t <= avg.shape[0]
    return pl.pallas_call(
        functools.partial(_moba_kernel, runs=_tile_schedule(s // t)),
        grid=(b,),
        in_specs=[
            pl.BlockSpec((1, h * dh, s), lambda bi: (bi, 0, 0)),
            pl.BlockSpec((1, h, s, mk.shape[3]), lambda bi: (bi, 0, 0, 0)),
            pl.BlockSpec((1, h * dh, s), lambda bi: (bi, 0, 0)),
            pl.BlockSpec(bias_near.shape, lambda bi: (0, 0, 0, 0)),
            pl.BlockSpec(avg.shape, lambda bi: (0, 0)),
        ],
        out_specs=pl.BlockSpec((1, s, h * dh), lambda bi: (bi, 0, 0)),
        out_shape=jax.ShapeDtypeStruct((b, s, h * dh), BF16),
        scratch_shapes=[pltpu.VMEM((h, mk.shape[3], s), BF16)] + _attention_scratch(h, s // t, t),
        compiler_params=_params("parallel"),
        name="moba",
    )(mq_t, mk, mv_t, bias_near, avg)


def _diff_kernel(qt_ref, k_ref, vt_ref, bias_ref, lam_ref, subln_ref, o_ref, m_ref, acc_ref,
                 s0_ref, s1_ref, p0_ref, p1_ref, *, runs, lambda_init):
    t = ATT_TILE
    heads = range(DIFF_HEADS)
    feature = lax.broadcasted_iota(jnp.int32, (HEAD_DIM, t), 0)
    lp = lam_ref[...]
    lam = (jnp.exp(jnp.sum(lp[0:1] * lp[1:2], axis=-1, keepdims=True))
           - jnp.exp(jnp.sum(lp[2:3] * lp[3:4], axis=-1, keepdims=True)) + lambda_init)

    def score_inputs(qi, kj, kind):
        tile = _ADD_TILE[kind]
        out = []
        for h in heads:
            k = _rows(k_ref.at[0, h], kj)
            q_t = _cols(qt_ref.at[0, h * HEAD_DIM:(h + 1) * HEAD_DIM], qi)
            adds = [] if tile is None else [bias_ref[h, tile]]
            out.append((k, jnp.where(feature < DIFF_QK_DIM, q_t, jnp.zeros_like(q_t)), adds))
            out.append((k, jnp.where(feature >= DIFF_QK_DIM, q_t, jnp.zeros_like(q_t)), adds))
        return out

    def value_tiles(kj):
        tiles = [_cols(vt_ref.at[0, h * HEAD_DIM:(h + 1) * HEAD_DIM], kj) for h in heads]
        return [tiles[h] for h in heads for _ in range(2)]

    _attend_tiles(runs, score_inputs, value_tiles, m_ref, acc_ref, (s0_ref, s1_ref), (p0_ref, p1_ref))

    def finish(qi, carry):
        outs = []
        for h in heads:
            o = _normalized(acc_ref[2 * h, qi]) - lam * _normalized(acc_ref[2 * h + 1, qi])
            o = o * lax.rsqrt(jnp.mean(o * o, axis=0, keepdims=True) + RMS_EPS) * subln_ref[...]
            outs.append(o * (1.0 - lambda_init))
        o_ref[0, pl.ds(pl.multiple_of(qi * t, t), t), :] = jnp.concatenate(outs, axis=0).T.astype(o_ref.dtype)
        return carry

    lax.fori_loop(0, qt_ref.shape[2] // t, finish, 0)


def _diff(dq_t, dk, dv_t, bias_near, lam_params, subln_g, lambda_init):
    b, h, s, dh = dk.shape
    t = ATT_TILE
    return pl.pallas_call(
        functools.partial(_diff_kernel, runs=_tile_schedule(s // t), lambda_init=lambda_init),
        grid=(b,),
        in_specs=[
            pl.BlockSpec((1, h * dh, s), lambda bi: (bi, 0, 0)),
            pl.BlockSpec((1, h, s, dh), lambda bi: (bi, 0, 0, 0)),
            pl.BlockSpec((1, h * dh, s), lambda bi: (bi, 0, 0)),
            pl.BlockSpec(bias_near.shape, lambda bi: (0, 0, 0, 0)),
            pl.BlockSpec(lam_params.shape, lambda bi: (0, 0)),
            pl.BlockSpec((dh, 1), lambda bi: (0, 0)),
        ],
        out_specs=pl.BlockSpec((1, s, h * dh), lambda bi: (bi, 0, 0)),
        out_shape=jax.ShapeDtypeStruct((b, s, h * dh), BF16),
        scratch_shapes=_attention_scratch(2 * h, s // t, t),
        compiler_params=_params("parallel"),
        name="diff",
    )(dq_t, dk, dv_t, bias_near, lam_params, subln_g.reshape(dh, 1))


def _compress_kernel(kc_ref, vc_ref, pe_ref, w1_ref, w2k_ref, w2vt_ref, ko_ref, vo_ref):
    half = NSA_CMP_STRIDE * HEAD_DIM

    n_chunk = kc_ref.shape[2] // NSA_CMP_STRIDE

    def hidden(t, c_ref):
        c = jnp.concatenate([c_ref[0, 0, pl.ds(l, n_chunk, stride=NSA_CMP_STRIDE), :] for l in range(NSA_CMP_STRIDE)], axis=1)
        top = (c + pe_ref[2 * t:2 * t + 1]).astype(BF16)
        bot = (c + pe_ref[2 * t + 1:2 * t + 2]).astype(BF16)
        a = jnp.dot(top, w1_ref[t, :half], preferred_element_type=F32)
        bm = jnp.dot(bot, w1_ref[t, half:], preferred_element_type=F32)
        hid = a + pltpu.roll(bm, bm.shape[0] - 1, 0)
        return jax.nn.gelu(hid).astype(BF16)

    ko_ref[0, 0] = jnp.dot(hidden(0, kc_ref), w2k_ref[...], preferred_element_type=F32)
    vo_ref[0, 0] = lax.dot_general(w2vt_ref[...], hidden(1, vc_ref), NT_DIMS, preferred_element_type=F32)


def _compress(kc, vc, pe4, w1, w2k, w2v_t):
    b, g, s, dh = kc.shape
    n_chunk = s // NSA_CMP_STRIDE
    spec_in = pl.BlockSpec((1, 1, s, dh), lambda bi, gi: (bi, gi, 0, 0))
    return pl.pallas_call(
        _compress_kernel,
        grid=(b, g),
        in_specs=[
            spec_in, spec_in,
            pl.BlockSpec(pe4.shape, lambda bi, gi: (0, 0)),
            pl.BlockSpec(w1.shape, lambda bi, gi: (0, 0, 0)),
            pl.BlockSpec(w2k.shape, lambda bi, gi: (0, 0)),
            pl.BlockSpec(w2v_t.shape, lambda bi, gi: (0, 0)),
        ],
        out_specs=[pl.BlockSpec((1, 1, n_chunk, dh), lambda bi, gi: (bi, gi, 0, 0)),
                   pl.BlockSpec((1, 1, dh, n_chunk), lambda bi, gi: (bi, gi, 0, 0))],
        out_shape=[jax.ShapeDtypeStruct((b, g, n_chunk, dh), F32), jax.ShapeDtypeStruct((b, g, dh, n_chunk), F32)],
        compiler_params=_params("parallel", "parallel"),
        name="nsa_compress",
    )(kc, vc, pe4, w1, w2k, w2v_t)


def _nsa_kernel(qt_ref, kcmp_ref, vcmpt_ref, ks_ref, vst_ref, kw_ref, vwt_ref, gate_ref,
                bias_ref, overlap_ref, o_ref, qx_ref, ocmp_ref, m_ref, acc_ref, m2_ref, acc2_ref,
                s0_ref, s1_ref, p0_ref, p1_ref, *, slc_runs, win_runs):
    t = ATT_TILE
    heads = range(NSA_GROUP)
    s = qt_ref.shape[2]
    qt_refs = [qt_ref.at[0, r * HEAD_DIM:(r + 1) * HEAD_DIM] for r in heads]
    n_chunk = kcmp_ref.shape[2]
    n_slc = s // NSA_SLC_BLOCK
    ks_g, vst_g = ks_ref.at[0, 0], vst_ref.at[0]
    kw_g, vwt_g = kw_ref.at[0, 0], vwt_ref.at[0]
    k_cmp = kcmp_ref[0, 0].astype(BF16)
    v_cmp_t = vcmpt_ref[0, 0].astype(BF16)
    topk = min(NSA_SLC_TOPK, n_slc)
    width = min(topk * NSA_SLC_BLOCK, s)
    assert s % width == 0 and width % t == 0

    def select(q0):
        cmp_id = lax.broadcasted_iota(jnp.int32, (n_chunk, width), 0)
        q_pos = q0 + lax.broadcasted_iota(jnp.int32, (n_chunk, width), 1)
        cmp_valid = (cmp_id * NSA_CMP_STRIDE + (NSA_CMP_BLOCK - 1) <= q_pos) & (cmp_id < n_chunk - 1)
        p_sum = jnp.zeros((n_chunk, width), F32)
        for r in heads:
            sc = jnp.dot(k_cmp, qt_refs[r][:, q0:q0 + width], preferred_element_type=F32)
            m = jnp.max(jnp.where(cmp_valid, sc, NEG_INF), axis=0, keepdims=True)
            e = jnp.where(cmp_valid, jnp.exp2(sc - m), 0.0)
            l = jnp.sum(e, axis=0, keepdims=True)
            p = e / jnp.where(l > 0.0, l, 1.0)
            p_sum = p_sum + p
            o = jnp.dot(v_cmp_t, p.astype(BF16), preferred_element_type=F32)
            for u in range(width // t):
                ocmp_ref[r, q0 // t + u] = o[:, u * t:(u + 1) * t]
        blk = lax.broadcasted_iota(jnp.int32, (n_slc, width), 0)
        cur = (q0 + lax.broadcasted_iota(jnp.int32, (n_slc, width), 1)) // NSA_SLC_BLOCK
        if q0 + width <= topk * NSA_SLC_BLOCK:
            sel = blk <= cur
        else:
            ps_hi = p_sum.astype(BF16)
            ps_lo = (p_sum - ps_hi.astype(F32)).astype(BF16)
            imp = (jnp.dot(overlap_ref[...], ps_hi, preferred_element_type=F32)
                   + jnp.dot(overlap_ref[...], ps_lo, preferred_element_type=F32))
            forced = (blk == 0) | (blk > cur - NSA_LOCAL_BLOCKS)
            score = jnp.where(blk <= cur, jnp.where(forced, FORCE_SCORE, imp), NEG_INF)
            sel = _topk_rows(score, n_slc, topk)
        for u in range(width // t):
            cols = slice(q0 + u * t, q0 + (u + 1) * t)
            for r in heads:
                qx_ref[r, :, cols] = _extend_query(qt_refs[r][:, cols], sel[:, u * t:(u + 1) * t])

    for q0 in range(0, s, width):
        select(q0)

    def adds(r, kind):
        return [] if _ADD_TILE[kind] is None else [bias_ref[0, r, _ADD_TILE[kind]]]

    def slc_inputs(qi, kj, kind):
        return [(_rows(ks_g, kj), _cols(qx_ref.at[r], qi), adds(r, kind)) for r in heads]

    def win_inputs(qi, kj, kind):
        return [(_rows(kw_g, kj), _cols(qt_refs[r], qi), adds(r, kind)) for r in heads]

    bufs = ((s0_ref, s1_ref), (p0_ref, p1_ref))
    _attend_tiles(slc_runs, slc_inputs, lambda kj: [_cols(vst_g, kj)] * len(heads), m_ref, acc_ref, *bufs)
    _attend_tiles(win_runs, win_inputs, lambda kj: [_cols(vwt_g, kj)] * len(heads), m2_ref, acc2_ref, *bufs)

    def finish(qi, carry):
        cols = pl.ds(pl.multiple_of(qi * t, t), t)
        gates = jax.nn.sigmoid(gate_ref[0, :, cols])
        outs = []
        for r in heads:
            g_cmp, g_slc, g_win = (gates[3 * r + br:3 * r + br + 1, :] for br in range(3))
            outs.append(g_cmp * ocmp_ref[r, qi] + g_slc * _normalized(acc_ref[r, qi]) + g_win * _normalized(acc2_ref[r, qi]))
        o_ref[0, cols, :] = jnp.concatenate(outs, axis=0).T.astype(o_ref.dtype)
        return carry

    lax.fori_loop(0, s // t, finish, 0)


def _nsa(nq_t, k_cmp, v_cmp_t, ks, vs_t, kw, vw_t, gate_t, bias_tiles, overlap_t):
    b, _, s, dh = kw.shape
    g, r, t = NSA_KV_HEADS, NSA_GROUP, ATT_TILE
    assert NSA_WINDOW == 2 * t and s % t == 0 and t % NSA_SLC_BLOCK == 0
    slc_runs = _tile_schedule(s // t)
    win_runs = _tile_schedule(s // t, window=NSA_WINDOW // t)
    k_spec = lambda k: pl.BlockSpec((1, 1, s, k.shape[3]), lambda bi, gi: (bi, gi, 0, 0))
    vt_spec = pl.BlockSpec((1, dh, s), lambda bi, gi: (bi, gi, 0))
    scratch = _attention_scratch(r, s // t, t)
    return pl.pallas_call(
        functools.partial(_nsa_kernel, slc_runs=slc_runs, win_runs=win_runs),
        grid=(b, g),
        in_specs=[
            pl.BlockSpec((1, r * dh, s), lambda bi, gi: (bi, gi, 0)),
            pl.BlockSpec((1, 1) + k_cmp.shape[2:], lambda bi, gi: (bi, gi, 0, 0)),
            pl.BlockSpec((1, 1) + v_cmp_t.shape[2:], lambda bi, gi: (bi, gi, 0, 0)),
            k_spec(ks), vt_spec, k_spec(kw), vt_spec,
            pl.BlockSpec((1, NSA_GATE_ROWS, s), lambda bi, gi: (bi, gi, 0)),
            pl.BlockSpec((1,) + bias_tiles.shape[1:], lambda bi, gi: (gi, 0, 0, 0, 0)),
            pl.BlockSpec(overlap_t.shape, lambda bi, gi: (0, 0)),
        ],
        out_specs=pl.BlockSpec((1, s, r * dh), lambda bi, gi: (bi, 0, gi)),
        out_shape=jax.ShapeDtypeStruct((b, s, g * r * dh), BF16),
        scratch_shapes=[pltpu.VMEM((r, ks.shape[3], s), BF16), pltpu.VMEM((r, s // t, dh, t), F32)]
                       + scratch[:2] + scratch,
        compiler_params=_params("parallel", "parallel"),
        name="nsa",
    )(nq_t, k_cmp, v_cmp_t, ks, vs_t, kw, vw_t, gate_t, bias_tiles, overlap_t)


def _rel_bucket(dist):
    n = jnp.maximum(dist, 0)
    max_exact = REL_BUCKETS // 2
    n_f = jnp.maximum(n, max_exact).astype(F32)
    large = max_exact + (jnp.log(n_f / max_exact) / math.log(REL_MAX_DIST / max_exact)
                         * (REL_BUCKETS - max_exact)).astype(jnp.int32)
    return jnp.where(n < max_exact, n, jnp.minimum(large, REL_BUCKETS - 1))


def _bias_tiles(rel_bias):
    t = ATT_TILE
    assert t >= REL_MAX_DIST
    heads = rel_bias.shape[1]
    width = 2 * t + 1
    by_dist = rel_bias[_rel_bucket(jnp.arange(width))] - rel_bias[REL_BUCKETS - 1]
    skew = jnp.broadcast_to(by_dist.T[:, None, :], (heads, t, width)).reshape(heads, t * width)
    skew = skew[:, :t * (width - 1)].reshape(heads, t, width - 1)
    causal = jnp.arange(t)[:, None] <= jnp.arange(t)[None, :]
    own = jnp.where(causal, skew[:, :, :t], NEG_INF)
    return jnp.stack([own, skew[:, :, t:]], axis=1)


def _const_tables(s):
    n_moba = s // MOBA_BLOCK
    avg = np.zeros((2 * SUBLANES, s), np.float32)
    for j in range(n_moba):
        avg[j, j * MOBA_BLOCK:(j + 1) * MOBA_BLOCK] = 1.0 / MOBA_BLOCK
    n_cmp = (s - NSA_CMP_BLOCK) // NSA_CMP_STRIDE + 1
    n_slc = s // NSA_SLC_BLOCK
    cmp_start = np.arange(n_cmp) * NSA_CMP_STRIDE
    slc_start = np.arange(n_slc) * NSA_SLC_BLOCK
    ov = np.clip(np.minimum(cmp_start[:, None] + NSA_CMP_BLOCK, slc_start[None, :] + NSA_SLC_BLOCK)
                 - np.maximum(cmp_start[:, None], slc_start[None, :]), 0, None) / NSA_CMP_BLOCK
    overlap_t = np.zeros((n_slc, s // NSA_CMP_STRIDE), np.float32)
    overlap_t[:, :n_cmp] = ov.T
    t = ATT_TILE
    win_far = np.where(np.arange(t)[:, None] > np.arange(t)[None, :], 0.0, NEG_INF).astype(np.float32)
    member = np.zeros((len(_MEMBER_BLOCKS), s, HEAD_DIM), np.float32)
    for kind, block in enumerate(_MEMBER_BLOCKS):
        assert s // block <= HEAD_DIM
        member[kind, np.arange(s), np.arange(s) // block] = 1.0
    return jnp.asarray(avg, BF16), jnp.asarray(overlap_t, BF16), jnp.asarray(win_far), jnp.asarray(member, BF16)


def kernel(x, rel_bias, norm_ffn1, ffn1_gate, ffn1_up, ffn1_down, norm_mix, w_in, diff_lambda, diff_subln, nsa_cmp_pe, nsa_cmp_w1, nsa_cmp_w2, w_out, norm_ffn2, ffn2_gate, ffn2_up, ffn2_down, final_norm):
    b, s, d = x.shape
    depth = w_in.shape[0]
    assert d == D_MODEL
    h0, h1 = MOBA_HEADS, MOBA_HEADS + DIFF_HEADS
    near = _bias_tiles(rel_bias.astype(F32))
    avg, overlap_t, win_far, member = _const_tables(s)
    near = near * LOG2E
    moba_bias = near[:h0]
    diff_bias = near[h0:h1]
    nsa_bias = jnp.concatenate([near[h1:], jnp.broadcast_to(win_far, near[h1:, :1].shape)], axis=1)
    nsa_bias = nsa_bias.reshape((NSA_KV_HEADS, NSA_GROUP) + nsa_bias.shape[1:])
    bf = lambda a: a.astype(BF16)
    ffn1_gate, ffn1_up, ffn1_down, ffn2_gate, ffn2_up, ffn2_down, w_in, w_out = (
        _to_bf16(w) for w in (ffn1_gate, ffn1_up, ffn1_down, ffn2_gate, ffn2_up, ffn2_down, w_in, w_out))
    cmp_w1 = _to_bf16(nsa_cmp_w1.reshape((-1,) + nsa_cmp_w1.shape[2:])).reshape(nsa_cmp_w1.shape)

    x = x.reshape(b * s, d)
    flat = lambda a: a.reshape(b * s, a.shape[-1])
    for l in range(depth):
        lambda_init = 0.8 - 0.6 * math.exp(-0.3 * l)
        x = _ffn(x, l, norm_ffn1[l], ffn1_gate, ffn1_up, ffn1_down)

        p = _inproj(x.reshape(b, s, d), norm_mix[l], *_inproj_weights(w_in[l]), member)
        o_moba = _moba(p["mqT"], p["mk"], p["mvT"], moba_bias, avg)
        o_diff = _diff(p["dqT"], p["dk"], p["dvT"], diff_bias, diff_lambda[l].astype(F32), diff_subln[l].astype(F32),
                       lambda_init)
        pe4 = nsa_cmp_pe[l].astype(F32).reshape(4, NSA_CMP_STRIDE * HEAD_DIM)
        k_cmp, v_cmp_t = _compress(p["kc"], p["vc"], pe4, cmp_w1[l], bf(nsa_cmp_w2[l, 0]), bf(nsa_cmp_w2[l, 1].T))
        o_nsa = _nsa(p["nqT"], k_cmp, v_cmp_t, p["ks"], p["vsT"], p["kw"], p["vwT"], p["gateT"], nsa_bias, overlap_t)
        x = _ffn(x, l, norm_ffn2[l], ffn2_gate, ffn2_up, ffn2_down,
                 mix=(flat(o_moba), flat(o_diff), flat(o_nsa)), w_mix=w_out,
                 post_g=final_norm if l == depth - 1 else None)
    return x.reshape(b, s, d)
```

```python
import functools
import math

import numpy as np
import jax
import jax.numpy as jnp
from jax import lax
from jax.experimental import pallas as pl
from jax.experimental.pallas import tpu as pltpu

F32 = jnp.float32
BF16 = jnp.bfloat16

D_MODEL = 1024
HEAD_DIM = 64
MOBA_HEADS = 4
MOBA_BLOCK = 256
MOBA_TOPK = 3
DIFF_HEADS = 4
DIFF_QK_DIM = HEAD_DIM // 2
NSA_HEADS = 8
NSA_KV_HEADS = 2
NSA_GROUP = NSA_HEADS // NSA_KV_HEADS
NSA_CMP_BLOCK = 32
NSA_CMP_STRIDE = 16
NSA_CMP_HIDDEN = 256
NSA_SLC_BLOCK = 64
NSA_SLC_TOPK = 16
NSA_LOCAL_BLOCKS = 2
NSA_WINDOW = 512
REL_BUCKETS = 32
REL_MAX_DIST = 128
D_FF = 2816
RMS_EPS = 1e-6
NEG_INF = -1e30
FORCE_SCORE = 1e4

NSA_GATE_W = 3 * NSA_HEADS
NSA_GATE_ROWS = 16
LANES = 128
MXU_WIDTH = 256
SUBLANES = 8
ATT_TILE = 256
VMEM_LIMIT = 48 * 1024 * 1024
LOG2E = math.log2(math.e)

NT_DIMS = (((1,), (1,)), ((), ()))


def _rms(x, g):
    return x * lax.rsqrt(jnp.mean(x * x, axis=-1, keepdims=True) + RMS_EPS) * g


def _params(*sem):
    return pltpu.CompilerParams(dimension_semantics=sem, vmem_limit_bytes=VMEM_LIMIT)


def _cast_kernel(x_ref, o_ref):
    o_ref[...] = x_ref[...].astype(o_ref.dtype)


def _to_bf16(w, *, row_blocks=4):
    layers, rows, cols = w.shape
    br = rows // row_blocks
    assert rows % row_blocks == 0 and br % (2 * SUBLANES) == 0
    spec = pl.BlockSpec((1, br, cols), lambda l, r: (l, r, 0))
    return pl.pallas_call(
        _cast_kernel,
        grid=(layers, row_blocks),
        in_specs=[spec],
        out_specs=spec,
        out_shape=jax.ShapeDtypeStruct(w.shape, BF16),
        compiler_params=_params("parallel", "parallel"),
        name="to_bf16",
    )(w)


def _ffn_kernel(*refs, tf, n_mix, post_norm):
    x_ref, refs = refs[0], refs[1:]
    mix_refs, refs = refs[:n_mix], refs[n_mix:]
    if n_mix:
        wmix_ref, refs = refs[0], refs[1:]
    g_ref, wg_ref, wu_ref, wd_ref, refs = refs[0], refs[1], refs[2], refs[3], refs[4:]
    if post_norm:
        post_ref, refs = refs[0], refs[1:]
    o_ref, h_ref = refs

    y = x_ref[...]
    row = 0
    for m_ref in mix_refs:
        y = y + jnp.dot(m_ref[...], wmix_ref[row:row + m_ref.shape[1]], preferred_element_type=F32)
        row += m_ref.shape[1]
    o_ref[...] = y
    xn = _rms(y, g_ref[...]).astype(BF16)
    for f in range(h_ref.shape[1] // tf):
        cols = slice(f * tf, (f + 1) * tf)
        gate = jnp.dot(xn, wg_ref[:, cols], preferred_element_type=F32)
        up = jnp.dot(xn, wu_ref[:, cols], preferred_element_type=F32)
        h_ref[:, cols] = (gate * jax.nn.sigmoid(gate) * up).astype(BF16)
    o = o_ref[...] + 0.5 * jnp.dot(h_ref[...], wd_ref[...], preferred_element_type=F32)
    o_ref[...] = _rms(o, post_ref[...]) if post_norm else o


def _ffn(x2, layer, g, wg, wu, wd, mix=(), w_mix=None, post_g=None, *, tm=1024, tf=256):
    n, d = x2.shape
    ff = wg.shape[2]
    resident = dict(pipeline_mode=pl.Buffered(1))
    row_spec = lambda a: pl.BlockSpec((tm, a.shape[1]), lambda i: (i, 0))
    whole = lambda a: pl.BlockSpec((None,) + a.shape[1:], lambda i: (layer, 0, 0), **resident)
    gain = lambda a: a.reshape(1, d)
    args = [x2, *mix] + ([w_mix] if mix else []) + [gain(g), wg, wu, wd] + ([gain(post_g)] if post_g is not None else [])
    specs = ([row_spec(x2)] + [row_spec(m) for m in mix] + ([whole(w_mix)] if mix else [])
             + [pl.BlockSpec((1, d), lambda i: (0, 0)), whole(wg), whole(wu), whole(wd)]
             + ([pl.BlockSpec((1, d), lambda i: (0, 0))] if post_g is not None else []))
    return pl.pallas_call(
        functools.partial(_ffn_kernel, tf=tf, n_mix=len(mix), post_norm=post_g is not None),
        grid=(n // tm,),
        in_specs=specs,
        out_specs=row_spec(x2),
        out_shape=jax.ShapeDtypeStruct((n, d), F32),
        scratch_shapes=[pltpu.VMEM((tm, ff), BF16)],
        compiler_params=_params("parallel"),
        name="ffn",
    )(*args)


_ROW_OUTS = (
    ("mk", 256, 4, BF16, 1.0, 0),
    ("dk", 1024, 4, BF16, 1.0, None),
    ("kc", 2048, 2, F32, 1.0, None),
    ("vc", 2176, 2, F32, 1.0, None),
    ("ks", 2304, 2, BF16, 1.0, 1),
    ("kw", 2560, 2, BF16, 1.0, None),
)
_MEMBER_BLOCKS = (MOBA_BLOCK, NSA_SLC_BLOCK)
_COL_OUTS = (
    ("mqT", 0, 256, BF16, HEAD_DIM ** -0.5 * LOG2E),
    ("dqT", 768, 256, BF16, DIFF_QK_DIM ** -0.5 * LOG2E),
    ("nqT", 1536, 512, BF16, HEAD_DIM ** -0.5 * LOG2E),
    ("mvT", 512, 256, BF16, 1.0),
    ("dvT", 1280, 256, BF16, 1.0),
    ("vsT", 2432, 128, BF16, 1.0),
    ("vwT", 2688, 128, BF16, 1.0),
    ("gateT", None, NSA_KV_HEADS * NSA_GATE_ROWS, F32, 1.0),
)


def _inproj_kernel(x_ref, g_ref, w_ref, wt_ref, member_ref, *out_refs):
    xn = _rms(x_ref[0], g_ref[...]).astype(BF16)
    col, pending = 0, []
    for out, o_ref in zip(_ROW_OUTS, out_refs):
        pending.append((out, o_ref))
        width = sum(o[2] for o, _ in pending) * HEAD_DIM
        if width % MXU_WIDTH and out is not _ROW_OUTS[-1]:
            continue
        p = jnp.dot(xn, w_ref[:, col:col + width], preferred_element_type=F32)
        col += width
        first = 0
        for (_, _, heads, dtype, scale, member), ref in pending:
            for h in range(first, first + heads):
                head = p[:, h * HEAD_DIM:(h + 1) * HEAD_DIM]
                head = (head if scale == 1.0 else head * scale).astype(dtype)
                ref[0, h - first] = head if member is None else jnp.concatenate([head, member_ref[member]], axis=1)
            first += heads
        pending = []
    pt = lax.dot_general(wt_ref[...], xn, NT_DIMS, preferred_element_type=F32)
    row = 0
    for (_, _, rows, dtype, scale), o_ref in zip(_COL_OUTS, out_refs[len(_ROW_OUTS):]):
        part = pt[row:row + rows]
        o_ref[0] = (part if scale == 1.0 else part * scale).astype(dtype)
        row += rows


def _inproj_weights(w_l):
    w_rows = jnp.concatenate([w_l[:, c:c + h * HEAD_DIM] for _, c, h, _, _, _ in _ROW_OUTS], axis=1)
    gate_cols = w_l[:, w_l.shape[1] - NSA_GATE_W:]
    per_group = 3 * NSA_GROUP
    gate_t = jnp.zeros((NSA_KV_HEADS * NSA_GATE_ROWS, w_l.shape[0]), w_l.dtype)
    for g in range(NSA_KV_HEADS):
        gate_t = gate_t.at[g * NSA_GATE_ROWS:g * NSA_GATE_ROWS + per_group].set(
            gate_cols[:, g * per_group:(g + 1) * per_group].T)
    w_cols = jnp.concatenate([w_l[:, c:c + r].T for _, c, r, _, _ in _COL_OUTS[:-1]] + [gate_t], axis=0)
    return w_rows.astype(BF16), w_cols.astype(BF16)


def _inproj(x, g, w_rows, w_cols, member, *, ts=512):
    b, s, d = x.shape
    widths = [HEAD_DIM if member is None else 2 * HEAD_DIM for _, _, _, _, _, member in _ROW_OUTS]
    out_shape = [jax.ShapeDtypeStruct((b, o[2], s, w), o[3]) for o, w in zip(_ROW_OUTS, widths)]
    out_specs = [pl.BlockSpec((1, o[2], ts, w), lambda bi, i: (bi, 0, i, 0)) for o, w in zip(_ROW_OUTS, widths)]
    out_shape += [jax.ShapeDtypeStruct((b, o[2], s), o[3]) for o in _COL_OUTS]
    out_specs += [pl.BlockSpec((1, o[2], ts), lambda bi, i: (bi, 0, i)) for o in _COL_OUTS]
    outs = pl.pallas_call(
        _inproj_kernel,
        grid=(b, s // ts),
        in_specs=[
            pl.BlockSpec((1, ts, d), lambda bi, i: (bi, i, 0)),
            pl.BlockSpec((1, d), lambda bi, i: (0, 0)),
            pl.BlockSpec(w_rows.shape, lambda bi, i: (0, 0)),
            pl.BlockSpec(w_cols.shape, lambda bi, i: (0, 0)),
            pl.BlockSpec((member.shape[0], ts, HEAD_DIM), lambda bi, i: (0, i, 0)),
        ],
        out_specs=out_specs,
        out_shape=out_shape,
        compiler_params=_params("parallel", "parallel"),
        name="inproj",
    )(x, g.reshape(1, d), w_rows, w_cols, member)
    names = [o[0] for o in _ROW_OUTS] + [o[0] for o in _COL_OUTS]
    return dict(zip(names, outs))


ONES_ROWS = 16
EXP_ROWS = 32
TILES_PER_TRIP = 2
_ADD_TILE = {"own": 0, "near": 1, "edge": 2, "far": None}


def _tile_schedule(n_qt, window=None):
    runs = [("own", 0, n_qt), ("near", 1, n_qt - 1)]
    if window is None:
        runs += [("far", d, n_qt - d) for d in range(2, n_qt)]
    else:
        assert window == 2
        runs += [("edge", window, n_qt - window)]
    return tuple(run for run in runs if run[2] > 0)


def _attend_tiles(runs, score_inputs, value_tiles, m_ref, acc_ref, s_refs, p_refs):
    tq = m_ref.shape[-1]

    def scores(t, run, slot):
        kind, distance, first = run
        kj = jnp.asarray(t - first, jnp.int32)
        qi = kj + distance
        rows = pl.ds(pl.multiple_of(qi * SUBLANES, SUBLANES), SUBLANES)
        pend = []
        for ci, (k, q_t, adds) in enumerate(score_inputs(qi, kj, kind)):
            s = jnp.dot(k, q_t, preferred_element_type=F32)
            for a in adds:
                s = s + a
            s_refs[slot][ci] = s
            m_old = m_ref[ci, rows, :][0:1, :]
            m_new = jnp.maximum(m_old, jnp.max(s, axis=0, keepdims=True))
            m_ref[ci, rows, :] = jnp.broadcast_to(m_new, (SUBLANES, tq))
            pend.append((m_new, jnp.exp2(m_old - m_new)))
        return qi, kj, tuple(pend)

    def accumulate(pending, slot):
        qi, kj, pend = pending
        s_ref, p_ref = s_refs[slot], p_refs[slot]
        for ci, (v_t, (m_new, alpha)) in enumerate(zip(value_tiles(kj), pend)):
            tk = v_t.shape[1]
            for r in range(tk // EXP_ROWS):
                rows = slice(r * EXP_ROWS, (r + 1) * EXP_ROWS)
                p_ref[ci, rows, :] = jnp.exp2((s_ref[ci, rows, :] - m_new).astype(BF16))
            v_ext = jnp.concatenate([v_t, jnp.ones((ONES_ROWS, tk), BF16)], axis=0)
            acc_ref[ci, qi] = alpha * acc_ref[ci, qi] + jnp.dot(v_ext, p_ref[ci], preferred_element_type=F32)

    m_ref[...] = jnp.full(m_ref.shape, NEG_INF, F32)
    acc_ref[...] = jnp.zeros(acc_ref.shape, F32)
    tile_run = []
    for kind, distance, count in runs:
        tile_run += [(kind, distance, len(tile_run))] * count
    n = len(tile_run)
    pending = scores(0, tile_run[0], 0)
    step = 0
    while step < n - 1:
        run = tile_run[step + 1]
        length = 1
        while step + length < n - 1 and tile_run[step + length + 1] == run:
            length += 1

        def group(u, pending, step=step, run=run):
            for d in range(TILES_PER_TRIP):
                nxt = scores(step + TILES_PER_TRIP * u + d + 1, run, (step + d + 1) % 2)
                accumulate(pending, (step + d) % 2)
                pending = nxt
            return pending

        if length // TILES_PER_TRIP:
            pending = lax.fori_loop(0, length // TILES_PER_TRIP, group, pending)
            step += length - length % TILES_PER_TRIP
        for _ in range(length % TILES_PER_TRIP):
            nxt = scores(step + 1, run, (step + 1) % 2)
            accumulate(pending, step % 2)
            pending = nxt
            step += 1
    accumulate(pending, (n - 1) % 2)


def _rows(ref, j, t=ATT_TILE):
    return ref[pl.ds(pl.multiple_of(j * t, t), t), :]


def _cols(ref, j, t=ATT_TILE):
    return ref[:, pl.ds(pl.multiple_of(j * t, t), t)]


def _normalized(acc, dv=HEAD_DIM):
    return acc[:dv] / acc[dv:dv + 1]


def _topk_rows(score, n_rows, topk):
    row_id = lax.broadcasted_iota(jnp.int32, score.shape, 0)
    rank = jnp.zeros(score.shape, F32)
    for c in range(n_rows):
        other = score[c:c + 1, :]
        beats = (other > score) | ((other == score) & (row_id > c))
        rank = rank + jnp.where(beats, 1.0, 0.0)
    return (rank < topk) & (row_id < n_rows)


def _extend_query(q_t, chosen):
    dh, tq = q_t.shape
    pen = jnp.where(chosen, 0.0, NEG_INF)
    pen = jnp.concatenate([pen, jnp.zeros((dh - pen.shape[0], tq), F32)], axis=0)
    return jnp.concatenate([q_t, pen.astype(q_t.dtype)], axis=0)


def _attention_scratch(n, n_qt, t, dv=HEAD_DIM):
    return [pltpu.VMEM((n, n_qt * SUBLANES, t), F32), pltpu.VMEM((n, n_qt, dv + ONES_ROWS, t), F32),
            pltpu.VMEM((n, t, t), F32), pltpu.VMEM((n, t, t), F32),
            pltpu.VMEM((n, t, t), BF16), pltpu.VMEM((n, t, t), BF16)]


def _moba_kernel(qt_ref, k_ref, vt_ref, bias_ref, avg_ref, o_ref, qx_ref, m_ref, acc_ref,
                 s0_ref, s1_ref, p0_ref, p1_ref, *, runs):
    t = ATT_TILE
    s = qt_ref.shape[2]
    dh = HEAD_DIM
    n_blk = s // MOBA_BLOCK
    heads = range(MOBA_HEADS)
    blk_rows = -(-n_blk // SUBLANES) * SUBLANES
    ranked = min((MOBA_TOPK + 1) * MOBA_BLOCK, s)
    blk = lax.broadcasted_iota(jnp.int32, (blk_rows, s - ranked), 0)
    own = (ranked + lax.broadcasted_iota(jnp.int32, (blk_rows, s - ranked), 1)) // MOBA_BLOCK
    blk_t = lax.broadcasted_iota(jnp.int32, (blk_rows, t), 0)

    for h in heads:
        if ranked < s:
            k_mean = jnp.dot(avg_ref[...], k_ref[0, h], preferred_element_type=F32)[:, :HEAD_DIM]
            km_hi = k_mean.astype(BF16)
            km_lo = (k_mean - km_hi.astype(F32)).astype(BF16)
            q_t = qt_ref[0, h * dh:(h + 1) * dh, ranked:]
            gate = (jnp.dot(km_hi, q_t, preferred_element_type=F32)
                    + jnp.dot(km_lo, q_t, preferred_element_type=F32))
            gate = jnp.where(blk < own, gate[:blk_rows], NEG_INF)
            chosen = (_topk_rows(gate, n_blk, MOBA_TOPK) & (blk < own)) | (blk == own)
        for qi in range(s // t):
            cols = slice(qi * t, (qi + 1) * t)
            picked = blk_t <= qi * t // MOBA_BLOCK if qi * t < ranked else chosen[:, qi * t - ranked:(qi + 1) * t - ranked]
            qx_ref[h, :, cols] = _extend_query(qt_ref[0, h * dh:(h + 1) * dh, cols], picked)

    def score_inputs(qi, kj, kind):
        tile = _ADD_TILE[kind]
        return [(_rows(k_ref.at[0, h], kj), _cols(qx_ref.at[h], qi), [] if tile is None else [bias_ref[h, tile]])
                for h in heads]

    def value_tiles(kj):
        return [_cols(vt_ref.at[0, h * HEAD_DIM:(h + 1) * HEAD_DIM], kj) for h in heads]

    _attend_tiles(runs, score_inputs, value_tiles, m_ref, acc_ref, (s0_ref, s1_ref), (p0_ref, p1_ref))

    def finish(qi, carry):
        o_t = jnp.concatenate([_normalized(acc_ref[h, qi]) for h in heads], axis=0)
        o_ref[0, pl.ds(pl.multiple_of(qi * t, t), t), :] = o_t.T.astype(o_ref.dtype)
        return carry

    lax.fori_loop(0, s // t, finish, 0)


def _moba(mq_t, mk, mv_t, bias_near, avg):
    b, h, s, _ = mk.shape
    dh = mq_t.shape[1] // h
    t = ATT_TILE
    assert MOBA_BLOCK == t and s % t == 0 and s // t <= avg.shape[0]
    return pl.pallas_call(
        functools.partial(_moba_kernel, runs=_tile_schedule(s // t)),
        grid=(b,),
        in_specs=[
            pl.BlockSpec((1, h * dh, s), lambda bi: (bi, 0, 0)),
            pl.BlockSpec((1, h, s, mk.shape[3]), lambda bi: (bi, 0, 0, 0)),
            pl.BlockSpec((1, h * dh, s), lambda bi: (bi, 0, 0)),
            pl.BlockSpec(bias_near.shape, lambda bi: (0, 0, 0, 0)),
            pl.BlockSpec(avg.shape, lambda bi: (0, 0)),
        ],
        out_specs=pl.BlockSpec((1, s, h * dh), lambda bi: (bi, 0, 0)),
        out_shape=jax.ShapeDtypeStruct((b, s, h * dh), BF16),
        scratch_shapes=[pltpu.VMEM((h, mk.shape[3], s), BF16)] + _attention_scratch(h, s // t, t),
        compiler_params=_params("parallel"),
        name="moba",
    )(mq_t, mk, mv_t, bias_near, avg)


def _diff_kernel(qt_ref, k_ref, vt_ref, bias_ref, lam_ref, subln_ref, o_ref, m_ref, acc_ref,
                 s0_ref, s1_ref, p0_ref, p1_ref, *, runs, lambda_init):
    t = ATT_TILE
    heads = range(DIFF_HEADS)
    feature = lax.broadcasted_iota(jnp.int32, (HEAD_DIM, t), 0)
    lp = lam_ref[...]
    lam = (jnp.exp(jnp.sum(lp[0:1] * lp[1:2], axis=-1, keepdims=True))
           - jnp.exp(jnp.sum(lp[2:3] * lp[3:4], axis=-1, keepdims=True)) + lambda_init)

    def score_inputs(qi, kj, kind):
        tile = _ADD_TILE[kind]
        out = []
        for h in heads:
            k = _rows(k_ref.at[0, h], kj)
            q_t = _cols(qt_ref.at[0, h * HEAD_DIM:(h + 1) * HEAD_DIM], qi)
            adds = [] if tile is None else [bias_ref[h, tile]]
            out.append((k, jnp.where(feature < DIFF_QK_DIM, q_t, jnp.zeros_like(q_t)), adds))
            out.append((k, jnp.where(feature >= DIFF_QK_DIM, q_t, jnp.zeros_like(q_t)), adds))
        return out

    def value_tiles(kj):
        tiles = [_cols(vt_ref.at[0, h * HEAD_DIM:(h + 1) * HEAD_DIM], kj) for h in heads]
        return [tiles[h] for h in heads for _ in range(2)]

    _attend_tiles(runs, score_inputs, value_tiles, m_ref, acc_ref, (s0_ref, s1_ref), (p0_ref, p1_ref))

    def finish(qi, carry):
        outs = []
        for h in heads:
            o = _normalized(acc_ref[2 * h, qi]) - lam * _normalized(acc_ref[2 * h + 1, qi])
            o = o * lax.rsqrt(jnp.mean(o * o, axis=0, keepdims=True) + RMS_EPS) * subln_ref[...]
            outs.append(o * (1.0 - lambda_init))
        o_ref[0, pl.ds(pl.multiple_of(qi * t, t), t), :] = jnp.concatenate(outs, axis=0).T.astype(o_ref.dtype)
        return carry

    lax.fori_loop(0, qt_ref.shape[2] // t, finish, 0)


def _diff(dq_t, dk, dv_t, bias_near, lam_params, subln_g, lambda_init):
    b, h, s, dh = dk.shape
    t = ATT_TILE
    return pl.pallas_call(
        functools.partial(_diff_kernel, runs=_tile_schedule(s // t), lambda_init=lambda_init),
        grid=(b,),
        in_specs=[
            pl.BlockSpec((1, h * dh, s), lambda bi: (bi, 0, 0)),
            pl.BlockSpec((1, h, s, dh), lambda bi: (bi, 0, 0, 0)),
            pl.BlockSpec((1, h * dh, s), lambda bi: (bi, 0, 0)),
            pl.BlockSpec(bias_near.shape, lambda bi: (0, 0, 0, 0)),
            pl.BlockSpec(lam_params.shape, lambda bi: (0, 0)),
            pl.BlockSpec((dh, 1), lambda bi: (0, 0)),
        ],
        out_specs=pl.BlockSpec((1, s, h * dh), lambda bi: (bi, 0, 0)),
        out_shape=jax.ShapeDtypeStruct((b, s, h * dh), BF16),
        scratch_shapes=_attention_scratch(2 * h, s // t, t),
        compiler_params=_params("parallel"),
        name="diff",
    )(dq_t, dk, dv_t, bias_near, lam_params, subln_g.reshape(dh, 1))


def _compress_kernel(kc_ref, vc_ref, pe_ref, w1_ref, w2k_ref, w2vt_ref, ko_ref, vo_ref):
    half = NSA_CMP_STRIDE * HEAD_DIM

    n_chunk = kc_ref.shape[2] // NSA_CMP_STRIDE

    def hidden(t, c_ref):
        c = jnp.concatenate([c_ref[0, 0, pl.ds(l, n_chunk, stride=NSA_CMP_STRIDE), :] for l in range(NSA_CMP_STRIDE)], axis=1)
        top = (c + pe_ref[2 * t:2 * t + 1]).astype(BF16)
        bot = (c + pe_ref[2 * t + 1:2 * t + 2]).astype(BF16)
        a = jnp.dot(top, w1_ref[t, :half], preferred_element_type=F32)
        bm = jnp.dot(bot, w1_ref[t, half:], preferred_element_type=F32)
        hid = a + pltpu.roll(bm, bm.shape[0] - 1, 0)
        return jax.nn.gelu(hid).astype(BF16)

    ko_ref[0, 0] = jnp.dot(hidden(0, kc_ref), w2k_ref[...], preferred_element_type=F32)
    vo_ref[0, 0] = lax.dot_general(w2vt_ref[...], hidden(1, vc_ref), NT_DIMS, preferred_element_type=F32)


def _compress(kc, vc, pe4, w1, w2k, w2v_t):
    b, g, s, dh = kc.shape
    n_chunk = s // NSA_CMP_STRIDE
    spec_in = pl.BlockSpec((1, 1, s, dh), lambda bi, gi: (bi, gi, 0, 0))
    return pl.pallas_call(
        _compress_kernel,
        grid=(b, g),
        in_specs=[
            spec_in, spec_in,
            pl.BlockSpec(pe4.shape, lambda bi, gi: (0, 0)),
            pl.BlockSpec(w1.shape, lambda bi, gi: (0, 0, 0)),
            pl.BlockSpec(w2k.shape, lambda bi, gi: (0, 0)),
            pl.BlockSpec(w2v_t.shape, lambda bi, gi: (0, 0)),
        ],
        out_specs=[pl.BlockSpec((1, 1, n_chunk, dh), lambda bi, gi: (bi, gi, 0, 0)),
                   pl.BlockSpec((1, 1, dh, n_chunk), lambda bi, gi: (bi, gi, 0, 0))],
        out_shape=[jax.ShapeDtypeStruct((b, g, n_chunk, dh), F32), jax.ShapeDtypeStruct((b, g, dh, n_chunk), F32)],
        compiler_params=_params("parallel", "parallel"),
        name="nsa_compress",
    )(kc, vc, pe4, w1, w2k, w2v_t)


def _nsa_kernel(qt_ref, kcmp_ref, vcmpt_ref, ks_ref, vst_ref, kw_ref, vwt_ref, gate_ref,
                bias_ref, overlap_ref, o_ref, qx_ref, ocmp_ref, m_ref, acc_ref, m2_ref, acc2_ref,
                s0_ref, s1_ref, p0_ref, p1_ref, *, slc_runs, win_runs):
    t = ATT_TILE
    heads = range(NSA_GROUP)
    s = qt_ref.shape[2]
    qt_refs = [qt_ref.at[0, r * HEAD_DIM:(r + 1) * HEAD_DIM] for r in heads]
    n_chunk = kcmp_ref.shape[2]
    n_slc = s // NSA_SLC_BLOCK
    ks_g, vst_g = ks_ref.at[0, 0], vst_ref.at[0]
    kw_g, vwt_g = kw_ref.at[0, 0], vwt_ref.at[0]
    k_cmp = kcmp_ref[0, 0].astype(BF16)
    v_cmp_t = vcmpt_ref[0, 0].astype(BF16)
    topk = min(NSA_SLC_TOPK, n_slc)
    width = min(topk * NSA_SLC_BLOCK, s)
    assert s % width == 0 and width % t == 0

    def select(q0):
        cmp_id = lax.broadcasted_iota(jnp.int32, (n_chunk, width), 0)
        q_pos = q0 + lax.broadcasted_iota(jnp.int32, (n_chunk, width), 1)
        cmp_valid = (cmp_id * NSA_CMP_STRIDE + (NSA_CMP_BLOCK - 1) <= q_pos) & (cmp_id < n_chunk - 1)
        masked = jnp.where(cmp_valid, 0.0, NEG_INF)
        sees_any = q_pos[0:1, :] >= NSA_CMP_BLOCK - 1
        p_sum = jnp.zeros((n_chunk, width), F32)
        for r in heads:
            sc = jnp.dot(k_cmp, qt_refs[r][:, q0:q0 + width], preferred_element_type=F32) + masked
            e = jnp.exp2(sc - jnp.max(sc, axis=0, keepdims=True))
            l = jnp.sum(e, axis=0, keepdims=True)
            p = e * jnp.where(sees_any, 1.0 / l, 0.0)
            p_sum = p_sum + p
            o = jnp.dot(v_cmp_t, p.astype(BF16), preferred_element_type=F32)
            for u in range(width // t):
                ocmp_ref[r, q0 // t + u] = o[:, u * t:(u + 1) * t]
        blk = lax.broadcasted_iota(jnp.int32, (n_slc, width), 0)
        cur = (q0 + lax.broadcasted_iota(jnp.int32, (n_slc, width), 1)) // NSA_SLC_BLOCK
        if q0 + width <= topk * NSA_SLC_BLOCK:
            sel = blk <= cur
        else:
            ps_hi = p_sum.astype(BF16)
            ps_lo = (p_sum - ps_hi.astype(F32)).astype(BF16)
            imp = (jnp.dot(overlap_ref[...], ps_hi, preferred_element_type=F32)
                   + jnp.dot(overlap_ref[...], ps_lo, preferred_element_type=F32))
            forced = (blk == 0) | (blk > cur - NSA_LOCAL_BLOCKS)
            score = jnp.where(blk <= cur, jnp.where(forced, FORCE_SCORE, imp), NEG_INF)
            sel = _topk_rows(score, n_slc, topk)
        for u in range(width // t):
            cols = slice(q0 + u * t, q0 + (u + 1) * t)
            for r in heads:
                qx_ref[r, :, cols] = _extend_query(qt_refs[r][:, cols], sel[:, u * t:(u + 1) * t])

    for q0 in range(0, s, width):
        select(q0)

    def adds(r, kind):
        return [] if _ADD_TILE[kind] is None else [bias_ref[0, r, _ADD_TILE[kind]]]

    def slc_inputs(qi, kj, kind):
        return [(_rows(ks_g, kj), _cols(qx_ref.at[r], qi), adds(r, kind)) for r in heads]

    def win_inputs(qi, kj, kind):
        return [(_rows(kw_g, kj), _cols(qt_refs[r], qi), adds(r, kind)) for r in heads]

    bufs = ((s0_ref, s1_ref), (p0_ref, p1_ref))
    _attend_tiles(slc_runs, slc_inputs, lambda kj: [_cols(vst_g, kj)] * len(heads), m_ref, acc_ref, *bufs)
    _attend_tiles(win_runs, win_inputs, lambda kj: [_cols(vwt_g, kj)] * len(heads), m2_ref, acc2_ref, *bufs)

    def finish(qi, carry):
        cols = pl.ds(pl.multiple_of(qi * t, t), t)
        gates = jax.nn.sigmoid(gate_ref[0, :, cols])
        outs = []
        for r in heads:
            g_cmp, g_slc, g_win = (gates[3 * r + br:3 * r + br + 1, :] for br in range(3))
            outs.append(g_cmp * ocmp_ref[r, qi] + g_slc * _normalized(acc_ref[r, qi]) + g_win * _normalized(acc2_ref[r, qi]))
        o_ref[0, cols, :] = jnp.concatenate(outs, axis=0).T.astype(o_ref.dtype)
        return carry

    lax.fori_loop(0, s // t, finish, 0)


def _nsa(nq_t, k_cmp, v_cmp_t, ks, vs_t, kw, vw_t, gate_t, bias_tiles, overlap_t):
    b, _, s, dh = kw.shape
    g, r, t = NSA_KV_HEADS, NSA_GROUP, ATT_TILE
    assert NSA_WINDOW == 2 * t and s % t == 0 and t % NSA_SLC_BLOCK == 0
    slc_runs = _tile_schedule(s // t)
    win_runs = _tile_schedule(s // t, window=NSA_WINDOW // t)
    k_spec = lambda k: pl.BlockSpec((1, 1, s, k.shape[3]), lambda bi, gi: (bi, gi, 0, 0))
    vt_spec = pl.BlockSpec((1, dh, s), lambda bi, gi: (bi, gi, 0))
    scratch = _attention_scratch(r, s // t, t)
    return pl.pallas_call(
        functools.partial(_nsa_kernel, slc_runs=slc_runs, win_runs=win_runs),
        grid=(b, g),
        in_specs=[
            pl.BlockSpec((1, r * dh, s), lambda bi, gi: (bi, gi, 0)),
            pl.BlockSpec((1, 1) + k_cmp.shape[2:], lambda bi, gi: (bi, gi, 0, 0)),
            pl.BlockSpec((1, 1) + v_cmp_t.shape[2:], lambda bi, gi: (bi, gi, 0, 0)),
            k_spec(ks), vt_spec, k_spec(kw), vt_spec,
            pl.BlockSpec((1, NSA_GATE_ROWS, s), lambda bi, gi: (bi, gi, 0)),
            pl.BlockSpec((1,) + bias_tiles.shape[1:], lambda bi, gi: (gi, 0, 0, 0, 0)),
            pl.BlockSpec(overlap_t.shape, lambda bi, gi: (0, 0)),
        ],
        out_specs=pl.BlockSpec((1, s, r * dh), lambda bi, gi: (bi, 0, gi)),
        out_shape=jax.ShapeDtypeStruct((b, s, g * r * dh), BF16),
        scratch_shapes=[pltpu.VMEM((r, ks.shape[3], s), BF16), pltpu.VMEM((r, s // t, dh, t), F32)]
                       + scratch[:2] + scratch,
        compiler_params=_params("parallel", "parallel"),
        name="nsa",
    )(nq_t, k_cmp, v_cmp_t, ks, vs_t, kw, vw_t, gate_t, bias_tiles, overlap_t)


def _rel_bucket(dist):
    n = jnp.maximum(dist, 0)
    max_exact = REL_BUCKETS // 2
    n_f = jnp.maximum(n, max_exact).astype(F32)
    large = max_exact + (jnp.log(n_f / max_exact) / math.log(REL_MAX_DIST / max_exact)
                         * (REL_BUCKETS - max_exact)).astype(jnp.int32)
    return jnp.where(n < max_exact, n, jnp.minimum(large, REL_BUCKETS - 1))


def _bias_tiles(rel_bias):
    t = ATT_TILE
    assert t >= REL_MAX_DIST
    heads = rel_bias.shape[1]
    width = 2 * t + 1
    by_dist = rel_bias[_rel_bucket(jnp.arange(width))] - rel_bias[REL_BUCKETS - 1]
    skew = jnp.broadcast_to(by_dist.T[:, None, :], (heads, t, width)).reshape(heads, t * width)
    skew = skew[:, :t * (width - 1)].reshape(heads, t, width - 1)
    causal = jnp.arange(t)[:, None] <= jnp.arange(t)[None, :]
    own = jnp.where(causal, skew[:, :, :t], NEG_INF)
    return jnp.stack([own, skew[:, :, t:]], axis=1)


def _const_tables(s):
    n_moba = s // MOBA_BLOCK
    avg = np.zeros((2 * SUBLANES, s), np.float32)
    for j in range(n_moba):
        avg[j, j * MOBA_BLOCK:(j + 1) * MOBA_BLOCK] = 1.0 / MOBA_BLOCK
    n_cmp = (s - NSA_CMP_BLOCK) // NSA_CMP_STRIDE + 1
    n_slc = s // NSA_SLC_BLOCK
    cmp_start = np.arange(n_cmp) * NSA_CMP_STRIDE
    slc_start = np.arange(n_slc) * NSA_SLC_BLOCK
    ov = np.clip(np.minimum(cmp_start[:, None] + NSA_CMP_BLOCK, slc_start[None, :] + NSA_SLC_BLOCK)
                 - np.maximum(cmp_start[:, None], slc_start[None, :]), 0, None) / NSA_CMP_BLOCK
    overlap_t = np.zeros((n_slc, s // NSA_CMP_STRIDE), np.float32)
    overlap_t[:, :n_cmp] = ov.T
    t = ATT_TILE
    win_far = np.where(np.arange(t)[:, None] > np.arange(t)[None, :], 0.0, NEG_INF).astype(np.float32)
    member = np.zeros((len(_MEMBER_BLOCKS), s, HEAD_DIM), np.float32)
    for kind, block in enumerate(_MEMBER_BLOCKS):
        assert s // block <= HEAD_DIM
        member[kind, np.arange(s), np.arange(s) // block] = 1.0
    return jnp.asarray(avg, BF16), jnp.asarray(overlap_t, BF16), jnp.asarray(win_far), jnp.asarray(member, BF16)


def kernel(x, rel_bias, norm_ffn1, ffn1_gate, ffn1_up, ffn1_down, norm_mix, w_in, diff_lambda, diff_subln, nsa_cmp_pe, nsa_cmp_w1, nsa_cmp_w2, w_out, norm_ffn2, ffn2_gate, ffn2_up, ffn2_down, final_norm):
    b, s, d = x.shape
    depth = w_in.shape[0]
    assert d == D_MODEL
    h0, h1 = MOBA_HEADS, MOBA_HEADS + DIFF_HEADS
    near = _bias_tiles(rel_bias.astype(F32))
    avg, overlap_t, win_far, member = _const_tables(s)
    near = near * LOG2E
    moba_bias = near[:h0]
    diff_bias = near[h0:h1]
    nsa_bias = jnp.concatenate([near[h1:], jnp.broadcast_to(win_far, near[h1:, :1].shape)], axis=1)
    nsa_bias = nsa_bias.reshape((NSA_KV_HEADS, NSA_GROUP) + nsa_bias.shape[1:])
    bf = lambda a: a.astype(BF16)
    ffn1_gate, ffn1_up, ffn1_down, ffn2_gate, ffn2_up, ffn2_down, w_in, w_out = (
        _to_bf16(w) for w in (ffn1_gate, ffn1_up, ffn1_down, ffn2_gate, ffn2_up, ffn2_down, w_in, w_out))
    cmp_w1 = _to_bf16(nsa_cmp_w1.reshape((-1,) + nsa_cmp_w1.shape[2:])).reshape(nsa_cmp_w1.shape)

    x = x.reshape(b * s, d)
    flat = lambda a: a.reshape(b * s, a.shape[-1])
    for l in range(depth):
        lambda_init = 0.8 - 0.6 * math.exp(-0.3 * l)
        x = _ffn(x, l, norm_ffn1[l], ffn1_gate, ffn1_up, ffn1_down)

        p = _inproj(x.reshape(b, s, d), norm_mix[l], *_inproj_weights(w_in[l]), member)
        o_moba = _moba(p["mqT"], p["mk"], p["mvT"], moba_bias, avg)
        o_diff = _diff(p["dqT"], p["dk"], p["dvT"], diff_bias, diff_lambda[l].astype(F32), diff_subln[l].astype(F32),
                       lambda_init)
        pe4 = nsa_cmp_pe[l].astype(F32).reshape(4, NSA_CMP_STRIDE * HEAD_DIM)
        k_cmp, v_cmp_t = _compress(p["kc"], p["vc"], pe4, cmp_w1[l], bf(nsa_cmp_w2[l, 0]), bf(nsa_cmp_w2[l, 1].T))
        o_nsa = _nsa(p["nqT"], k_cmp, v_cmp_t, p["ks"], p["vsT"], p["kw"], p["vwT"], p["gateT"], nsa_bias, overlap_t)
        x = _ffn(x, l, norm_ffn2[l], ffn2_gate, ffn2_up, ffn2_down,
                 mix=(flat(o_moba), flat(o_diff), flat(o_nsa)), w_mix=w_out,
                 post_g=final_norm if l == depth - 1 else None)
    return x.reshape(b, s, d)
```

```python
import functools
import math

import numpy as np
import jax
import jax.numpy as jnp
from jax import lax
from jax.experimental import pallas as pl
from jax.experimental.pallas import tpu as pltpu

F32 = jnp.float32
BF16 = jnp.bfloat16

D_MODEL = 1024
HEAD_DIM = 64
MOBA_HEADS = 4
MOBA_BLOCK = 256
MOBA_TOPK = 3
DIFF_HEADS = 4
DIFF_QK_DIM = HEAD_DIM // 2
NSA_HEADS = 8
NSA_KV_HEADS = 2
NSA_GROUP = NSA_HEADS // NSA_KV_HEADS
NSA_CMP_BLOCK = 32
NSA_CMP_STRIDE = 16
NSA_CMP_HIDDEN = 256
NSA_SLC_BLOCK = 64
NSA_SLC_TOPK = 16
NSA_LOCAL_BLOCKS = 2
NSA_WINDOW = 512
REL_BUCKETS = 32
REL_MAX_DIST = 128
D_FF = 2816
RMS_EPS = 1e-6
NEG_INF = -1e30
FORCE_SCORE = 1e4

NSA_GATE_W = 3 * NSA_HEADS
NSA_GATE_ROWS = 16
LANES = 128
MXU_WIDTH = 256
SUBLANES = 8
ATT_TILE = 256
VMEM_LIMIT = 48 * 1024 * 1024
LOG2E = math.log2(math.e)

NT_DIMS = (((1,), (1,)), ((), ()))


def _rms(x, g):
    return x * lax.rsqrt(jnp.mean(x * x, axis=-1, keepdims=True) + RMS_EPS) * g


def _params(*sem):
    return pltpu.CompilerParams(dimension_semantics=sem, vmem_limit_bytes=VMEM_LIMIT)


def _cast_kernel(x_ref, o_ref):
    o_ref[...] = x_ref[...].astype(o_ref.dtype)


def _to_bf16(w, *, row_blocks=4):
    layers, rows, cols = w.shape
    br = rows // row_blocks
    assert rows % row_blocks == 0 and br % (2 * SUBLANES) == 0
    spec = pl.BlockSpec((1, br, cols), lambda l, r: (l, r, 0))
    return pl.pallas_call(
        _cast_kernel,
        grid=(layers, row_blocks),
        in_specs=[spec],
        out_specs=spec,
        out_shape=jax.ShapeDtypeStruct(w.shape, BF16),
        compiler_params=_params("parallel", "parallel"),
        name="to_bf16",
    )(w)


def _ffn_kernel(*refs, tf, n_mix, post_norm):
    x_ref, refs = refs[0], refs[1:]
    mix_refs, refs = refs[:n_mix], refs[n_mix:]
    if n_mix:
        wmix_ref, refs = refs[0], refs[1:]
    g_ref, wg_ref, wu_ref, wd_ref, refs = refs[0], refs[1], refs[2], refs[3], refs[4:]
    if post_norm:
        post_ref, refs = refs[0], refs[1:]
    o_ref, h_ref = refs

    y = x_ref[...]
    row = 0
    for m_ref in mix_refs:
        y = y + jnp.dot(m_ref[...], wmix_ref[row:row + m_ref.shape[1]], preferred_element_type=F32)
        row += m_ref.shape[1]
    o_ref[...] = y
    xn = _rms(y, g_ref[...]).astype(BF16)
    for f in range(h_ref.shape[1] // tf):
        cols = slice(f * tf, (f + 1) * tf)
        gate = jnp.dot(xn, wg_ref[:, cols], preferred_element_type=F32)
        up = jnp.dot(xn, wu_ref[:, cols], preferred_element_type=F32)
        h_ref[:, cols] = (gate * jax.nn.sigmoid(gate) * up).astype(BF16)
    o = o_ref[...] + 0.5 * jnp.dot(h_ref[...], wd_ref[...], preferred_element_type=F32)
    o_ref[...] = _rms(o, post_ref[...]) if post_norm else o


def _ffn(x2, layer, g, wg, wu, wd, mix=(), w_mix=None, post_g=None, *, tm=1024, tf=256):
    n, d = x2.shape
    ff = wg.shape[2]
    resident = dict(pipeline_mode=pl.Buffered(1))
    row_spec = lambda a: pl.BlockSpec((tm, a.shape[1]), lambda i: (i, 0))
    whole = lambda a: pl.BlockSpec((None,) + a.shape[1:], lambda i: (layer, 0, 0), **resident)
    gain = lambda a: a.reshape(1, d)
    args = [x2, *mix] + ([w_mix] if mix else []) + [gain(g), wg, wu, wd] + ([gain(post_g)] if post_g is not None else [])
    specs = ([row_spec(x2)] + [row_spec(m) for m in mix] + ([whole(w_mix)] if mix else [])
             + [pl.BlockSpec((1, d), lambda i: (0, 0)), whole(wg), whole(wu), whole(wd)]
             + ([pl.BlockSpec((1, d), lambda i: (0, 0))] if post_g is not None else []))
    return pl.pallas_call(
        functools.partial(_ffn_kernel, tf=tf, n_mix=len(mix), post_norm=post_g is not None),
        grid=(n // tm,),
        in_specs=specs,
        out_specs=row_spec(x2),
        out_shape=jax.ShapeDtypeStruct((n, d), F32),
        scratch_shapes=[pltpu.VMEM((tm, ff), BF16)],
        compiler_params=_params("parallel"),
        name="ffn",
    )(*args)


_ROW_OUTS = (
    ("mk", 256, 4, BF16, 1.0, 0),
    ("dk", 1024, 4, BF16, 1.0, None),
    ("kc", 2048, 2, F32, 1.0, None),
    ("vc", 2176, 2, F32, 1.0, None),
    ("ks", 2304, 2, BF16, 1.0, 1),
    ("kw", 2560, 2, BF16, 1.0, None),
)
_MEMBER_BLOCKS = (MOBA_BLOCK, NSA_SLC_BLOCK)
_COL_OUTS = (
    ("mqT", 0, 256, BF16, HEAD_DIM ** -0.5 * LOG2E),
    ("dqT", 768, 256, BF16, DIFF_QK_DIM ** -0.5 * LOG2E),
    ("nqT", 1536, 512, BF16, HEAD_DIM ** -0.5 * LOG2E),
    ("mvT", 512, 256, BF16, 1.0),
    ("dvT", 1280, 256, BF16, 1.0),
    ("vsT", 2432, 128, BF16, 1.0),
    ("vwT", 2688, 128, BF16, 1.0),
    ("gateT", None, NSA_KV_HEADS * NSA_GATE_ROWS, F32, 1.0),
)


def _inproj_kernel(x_ref, g_ref, w_ref, wt_ref, member_ref, *out_refs):
    xn = _rms(x_ref[0], g_ref[...]).astype(BF16)
    col, pending = 0, []
    for out, o_ref in zip(_ROW_OUTS, out_refs):
        pending.append((out, o_ref))
        width = sum(o[2] for o, _ in pending) * HEAD_DIM
        if width % MXU_WIDTH and out is not _ROW_OUTS[-1]:
            continue
        p = jnp.dot(xn, w_ref[:, col:col + width], preferred_element_type=F32)
        col += width
        first = 0
        for (_, _, heads, dtype, scale, member), ref in pending:
            for h in range(first, first + heads):
                head = p[:, h * HEAD_DIM:(h + 1) * HEAD_DIM]
                head = (head if scale == 1.0 else head * scale).astype(dtype)
                ref[0, h - first] = head if member is None else jnp.concatenate([head, member_ref[member]], axis=1)
            first += heads
        pending = []
    pt = lax.dot_general(wt_ref[...], xn, NT_DIMS, preferred_element_type=F32)
    row = 0
    for (_, _, rows, dtype, scale), o_ref in zip(_COL_OUTS, out_refs[len(_ROW_OUTS):]):
        part = pt[row:row + rows]
        o_ref[0] = (part if scale == 1.0 else part * scale).astype(dtype)
        row += rows


def _inproj_weights(w_l):
    w_rows = jnp.concatenate([w_l[:, c:c + h * HEAD_DIM] for _, c, h, _, _, _ in _ROW_OUTS], axis=1)
    gate_cols = w_l[:, w_l.shape[1] - NSA_GATE_W:]
    per_group = 3 * NSA_GROUP
    gate_t = jnp.zeros((NSA_KV_HEADS * NSA_GATE_ROWS, w_l.shape[0]), w_l.dtype)
    for g in range(NSA_KV_HEADS):
        gate_t = gate_t.at[g * NSA_GATE_ROWS:g * NSA_GATE_ROWS + per_group].set(
            gate_cols[:, g * per_group:(g + 1) * per_group].T)
    w_cols = jnp.concatenate([w_l[:, c:c + r].T for _, c, r, _, _ in _COL_OUTS[:-1]] + [gate_t], axis=0)
    return w_rows.astype(BF16), w_cols.astype(BF16)


def _inproj(x, g, w_rows, w_cols, member, *, ts=512):
    b, s, d = x.shape
    widths = [HEAD_DIM if member is None else 2 * HEAD_DIM for _, _, _, _, _, member in _ROW_OUTS]
    out_shape = [jax.ShapeDtypeStruct((b, o[2], s, w), o[3]) for o, w in zip(_ROW_OUTS, widths)]
    out_specs = [pl.BlockSpec((1, o[2], ts, w), lambda bi, i: (bi, 0, i, 0)) for o, w in zip(_ROW_OUTS, widths)]
    out_shape += [jax.ShapeDtypeStruct((b, o[2], s), o[3]) for o in _COL_OUTS]
    out_specs += [pl.BlockSpec((1, o[2], ts), lambda bi, i: (bi, 0, i)) for o in _COL_OUTS]
    outs = pl.pallas_call(
        _inproj_kernel,
        grid=(b, s // ts),
        in_specs=[
            pl.BlockSpec((1, ts, d), lambda bi, i: (bi, i, 0)),
            pl.BlockSpec((1, d), lambda bi, i: (0, 0)),
            pl.BlockSpec(w_rows.shape, lambda bi, i: (0, 0)),
            pl.BlockSpec(w_cols.shape, lambda bi, i: (0, 0)),
            pl.BlockSpec((member.shape[0], ts, HEAD_DIM), lambda bi, i: (0, i, 0)),
        ],
        out_specs=out_specs,
        out_shape=out_shape,
        compiler_params=_params("parallel", "parallel"),
        name="inproj",
    )(x, g.reshape(1, d), w_rows, w_cols, member)
    names = [o[0] for o in _ROW_OUTS] + [o[0] for o in _COL_OUTS]
    return dict(zip(names, outs))


ONES_ROWS = 16
EXP_ROWS = 32
TILES_PER_TRIP = 2
_ADD_TILE = {"own": 0, "near": 1, "edge": 2, "far": None}


def _tile_schedule(n_qt, window=None):
    runs = [("own", 0, n_qt), ("near", 1, n_qt - 1)]
    if window is None:
        runs += [("far", d, n_qt - d) for d in range(2, n_qt)]
    else:
        assert window == 2
        runs += [("edge", window, n_qt - window)]
    return tuple(run for run in runs if run[2] > 0)


def _attend_tiles(runs, score_inputs, value_tiles, m_ref, acc_ref, s_refs, p_refs):
    tq = m_ref.shape[-1]

    def scores(t, run, slot):
        kind, distance, first = run
        kj = jnp.asarray(t - first, jnp.int32)
        qi = kj + distance
        rows = pl.ds(pl.multiple_of(qi * SUBLANES, SUBLANES), SUBLANES)
        pend = []
        for ci, (k, q_t, adds) in enumerate(score_inputs(qi, kj, kind)):
            s = jnp.dot(k, q_t, preferred_element_type=F32)
            for a in adds:
                s = s + a
            s_refs[slot][ci] = s
            m_old = m_ref[ci, rows, :][0:1, :]
            m_new = jnp.maximum(m_old, jnp.max(s, axis=0, keepdims=True))
            m_ref[ci, rows, :] = jnp.broadcast_to(m_new, (SUBLANES, tq))
            pend.append((m_new, jnp.exp2(m_old - m_new)))
        return qi, kj, tuple(pend)

    def accumulate(pending, slot):
        qi, kj, pend = pending
        s_ref, p_ref = s_refs[slot], p_refs[slot]
        for ci, (v_t, (m_new, alpha)) in enumerate(zip(value_tiles(kj), pend)):
            tk = v_t.shape[1]
            for r in range(tk // EXP_ROWS):
                rows = slice(r * EXP_ROWS, (r + 1) * EXP_ROWS)
                p_ref[ci, rows, :] = jnp.exp2((s_ref[ci, rows, :] - m_new).astype(BF16))
            v_ext = jnp.concatenate([v_t, jnp.ones((ONES_ROWS, tk), BF16)], axis=0)
            acc_ref[ci, qi] = alpha * acc_ref[ci, qi] + jnp.dot(v_ext, p_ref[ci], preferred_element_type=F32)

    m_ref[...] = jnp.full(m_ref.shape, NEG_INF, F32)
    acc_ref[...] = jnp.zeros(acc_ref.shape, F32)
    tile_run = []
    for kind, distance, count in runs:
        tile_run += [(kind, distance, len(tile_run))] * count
    n = len(tile_run)
    pending = scores(0, tile_run[0], 0)
    step = 0
    while step < n - 1:
        run = tile_run[step + 1]
        length = 1
        while step + length < n - 1 and tile_run[step + length + 1] == run:
            length += 1

        def group(u, pending, step=step, run=run):
            for d in range(TILES_PER_TRIP):
                nxt = scores(step + TILES_PER_TRIP * u + d + 1, run, (step + d + 1) % 2)
                accumulate(pending, (step + d) % 2)
                pending = nxt
            return pending

        if length // TILES_PER_TRIP:
            pending = lax.fori_loop(0, length // TILES_PER_TRIP, group, pending)
            step += length - length % TILES_PER_TRIP
        for _ in range(length % TILES_PER_TRIP):
            nxt = scores(step + 1, run, (step + 1) % 2)
            accumulate(pending, step % 2)
            pending = nxt
            step += 1
    accumulate(pending, (n - 1) % 2)


def _rows(ref, j, t=ATT_TILE):
    return ref[pl.ds(pl.multiple_of(j * t, t), t), :]


def _cols(ref, j, t=ATT_TILE):
    return ref[:, pl.ds(pl.multiple_of(j * t, t), t)]


def _for_each_tile(n, body):
    lax.fori_loop(0, n // 2, lambda u, carry: body(2 * u + 1, body(2 * u, carry)), 0)
    if n % 2:
        body(n - 1, 0)


def _normalized(acc, dv=HEAD_DIM):
    return acc[:dv] / acc[dv:dv + 1]


def _topk_rows(score, n_rows, topk):
    row_id = lax.broadcasted_iota(jnp.int32, score.shape, 0)
    rank = jnp.zeros(score.shape, F32)
    for c in range(n_rows):
        other = score[c:c + 1, :]
        beats = (other > score) | ((other == score) & (row_id > c))
        rank = rank + jnp.where(beats, 1.0, 0.0)
    return (rank < topk) & (row_id < n_rows)


def _extend_query(q_t, chosen):
    dh, tq = q_t.shape
    pen = jnp.where(chosen, 0.0, NEG_INF)
    pen = jnp.concatenate([pen, jnp.zeros((dh - pen.shape[0], tq), F32)], axis=0)
    return jnp.concatenate([q_t, pen.astype(q_t.dtype)], axis=0)


def _attention_scratch(n, n_qt, t, dv=HEAD_DIM):
    return [pltpu.VMEM((n, n_qt * SUBLANES, t), F32), pltpu.VMEM((n, n_qt, dv + ONES_ROWS, t), F32),
            pltpu.VMEM((n, t, t), F32), pltpu.VMEM((n, t, t), F32),
            pltpu.VMEM((n, t, t), BF16), pltpu.VMEM((n, t, t), BF16)]


def _moba_kernel(qt_ref, k_ref, vt_ref, bias_ref, avg_ref, o_ref, qx_ref, m_ref, acc_ref,
                 s0_ref, s1_ref, p0_ref, p1_ref, *, runs):
    t = ATT_TILE
    s = qt_ref.shape[2]
    dh = HEAD_DIM
    n_blk = s // MOBA_BLOCK
    heads = range(MOBA_HEADS)
    blk_rows = -(-n_blk // SUBLANES) * SUBLANES
    ranked = min((MOBA_TOPK + 1) * MOBA_BLOCK, s)
    blk = lax.broadcasted_iota(jnp.int32, (blk_rows, s - ranked), 0)
    own = (ranked + lax.broadcasted_iota(jnp.int32, (blk_rows, s - ranked), 1)) // MOBA_BLOCK
    blk_t = lax.broadcasted_iota(jnp.int32, (blk_rows, t), 0)

    for h in heads:
        if ranked < s:
            k_mean = jnp.dot(avg_ref[...], k_ref[0, h], preferred_element_type=F32)[:, :HEAD_DIM]
            km_hi = k_mean.astype(BF16)
            km_lo = (k_mean - km_hi.astype(F32)).astype(BF16)
            q_t = qt_ref[0, h * dh:(h + 1) * dh, ranked:]
            gate = (jnp.dot(km_hi, q_t, preferred_element_type=F32)
                    + jnp.dot(km_lo, q_t, preferred_element_type=F32))
            gate = jnp.where(blk < own, gate[:blk_rows], NEG_INF)
            chosen = (_topk_rows(gate, n_blk, MOBA_TOPK) & (blk < own)) | (blk == own)
        for qi in range(s // t):
            cols = slice(qi * t, (qi + 1) * t)
            picked = blk_t <= qi * t // MOBA_BLOCK if qi * t < ranked else chosen[:, qi * t - ranked:(qi + 1) * t - ranked]
            qx_ref[h, :, cols] = _extend_query(qt_ref[0, h * dh:(h + 1) * dh, cols], picked)

    def score_inputs(qi, kj, kind):
        tile = _ADD_TILE[kind]
        return [(_rows(k_ref.at[0, h], kj), _cols(qx_ref.at[h], qi), [] if tile is None else [bias_ref[h, tile]])
                for h in heads]

    def value_tiles(kj):
        return [_cols(vt_ref.at[0, h * HEAD_DIM:(h + 1) * HEAD_DIM], kj) for h in heads]

    _attend_tiles(runs, score_inputs, value_tiles, m_ref, acc_ref, (s0_ref, s1_ref), (p0_ref, p1_ref))

    def finish(qi, carry):
        o_t = jnp.concatenate([_normalized(acc_ref[h, qi]) for h in heads], axis=0)
        o_ref[0, pl.ds(pl.multiple_of(qi * t, t), t), :] = o_t.T.astype(o_ref.dtype)
        return carry

    _for_each_tile(s // t, finish)


def _moba(mq_t, mk, mv_t, bias_near, avg):
    b, h, s, _ = mk.shape
    dh = mq_t.shape[1] // h
    t = ATT_TILE
    assert MOBA_BLOCK == t and s % t == 0 and s // t <= avg.shape[0]
    return pl.pallas_call(
        functools.partial(_moba_kernel, runs=_tile_schedule(s // t)),
        grid=(b,),
        in_specs=[
            pl.BlockSpec((1, h * dh, s), lambda bi: (bi, 0, 0)),
            pl.BlockSpec((1, h, s, mk.shape[3]), lambda bi: (bi, 0, 0, 0)),
            pl.BlockSpec((1, h * dh, s), lambda bi: (bi, 0, 0)),
            pl.BlockSpec(bias_near.shape, lambda bi: (0, 0, 0, 0)),
            pl.BlockSpec(avg.shape, lambda bi: (0, 0)),
        ],
        out_specs=pl.BlockSpec((1, s, h * dh), lambda bi: (bi, 0, 0)),
        out_shape=jax.ShapeDtypeStruct((b, s, h * dh), BF16),
        scratch_shapes=[pltpu.VMEM((h, mk.shape[3], s), BF16)] + _attention_scratch(h, s // t, t),
        compiler_params=_params("parallel"),
        name="moba",
    )(mq_t, mk, mv_t, bias_near, avg)


def _diff_kernel(qt_ref, k_ref, vt_ref, bias_ref, lam_ref, subln_ref, o_ref, m_ref, acc_ref,
                 s0_ref, s1_ref, p0_ref, p1_ref, *, runs, lambda_init):
    t = ATT_TILE
    heads = range(DIFF_HEADS)
    feature = lax.broadcasted_iota(jnp.int32, (HEAD_DIM, t), 0)
    lp = lam_ref[...]
    lam = (jnp.exp(jnp.sum(lp[0:1] * lp[1:2], axis=-1, keepdims=True))
           - jnp.exp(jnp.sum(lp[2:3] * lp[3:4], axis=-1, keepdims=True)) + lambda_init)

    def score_inputs(qi, kj, kind):
        tile = _ADD_TILE[kind]
        out = []
        for h in heads:
            k = _rows(k_ref.at[0, h], kj)
            q_t = _cols(qt_ref.at[0, h * HEAD_DIM:(h + 1) * HEAD_DIM], qi)
            adds = [] if tile is None else [bias_ref[h, tile]]
            out.append((k, jnp.where(feature < DIFF_QK_DIM, q_t, jnp.zeros_like(q_t)), adds))
            out.append((k, jnp.where(feature >= DIFF_QK_DIM, q_t, jnp.zeros_like(q_t)), adds))
        return out

    def value_tiles(kj):
        tiles = [_cols(vt_ref.at[0, h * HEAD_DIM:(h + 1) * HEAD_DIM], kj) for h in heads]
        return [tiles[h] for h in heads for _ in range(2)]

    _attend_tiles(runs, score_inputs, value_tiles, m_ref, acc_ref, (s0_ref, s1_ref), (p0_ref, p1_ref))

    def finish(qi, carry):
        outs = []
        for h in heads:
            o = _normalized(acc_ref[2 * h, qi]) - lam * _normalized(acc_ref[2 * h + 1, qi])
            o = o * lax.rsqrt(jnp.mean(o * o, axis=0, keepdims=True) + RMS_EPS) * subln_ref[...]
            outs.append(o * (1.0 - lambda_init))
        o_ref[0, pl.ds(pl.multiple_of(qi * t, t), t), :] = jnp.concatenate(outs, axis=0).T.astype(o_ref.dtype)
        return carry

    _for_each_tile(qt_ref.shape[2] // t, finish)


def _diff(dq_t, dk, dv_t, bias_near, lam_params, subln_g, lambda_init):
    b, h, s, dh = dk.shape
    t = ATT_TILE
    return pl.pallas_call(
        functools.partial(_diff_kernel, runs=_tile_schedule(s // t), lambda_init=lambda_init),
        grid=(b,),
        in_specs=[
            pl.BlockSpec((1, h * dh, s), lambda bi: (bi, 0, 0)),
            pl.BlockSpec((1, h, s, dh), lambda bi: (bi, 0, 0, 0)),
            pl.BlockSpec((1, h * dh, s), lambda bi: (bi, 0, 0)),
            pl.BlockSpec(bias_near.shape, lambda bi: (0, 0, 0, 0)),
            pl.BlockSpec(lam_params.shape, lambda bi: (0, 0)),
            pl.BlockSpec((dh, 1), lambda bi: (0, 0)),
        ],
        out_specs=pl.BlockSpec((1, s, h * dh), lambda bi: (bi, 0, 0)),
        out_shape=jax.ShapeDtypeStruct((b, s, h * dh), BF16),
        scratch_shapes=_attention_scratch(2 * h, s // t, t),
        compiler_params=_params("parallel"),
        name="diff",
    )(dq_t, dk, dv_t, bias_near, lam_params, subln_g.reshape(dh, 1))


def _compress_kernel(kc_ref, vc_ref, pe_ref, w1_ref, w2k_ref, w2vt_ref, ko_ref, vo_ref):
    half = NSA_CMP_STRIDE * HEAD_DIM

    n_chunk = kc_ref.shape[2] // NSA_CMP_STRIDE

    def hidden(t, c_ref):
        c = jnp.concatenate([c_ref[0, 0, pl.ds(l, n_chunk, stride=NSA_CMP_STRIDE), :] for l in range(NSA_CMP_STRIDE)], axis=1)
        top = (c + pe_ref[2 * t:2 * t + 1]).astype(BF16)
        bot = (c + pe_ref[2 * t + 1:2 * t + 2]).astype(BF16)
        a = jnp.dot(top, w1_ref[t, :half], preferred_element_type=F32)
        bm = jnp.dot(bot, w1_ref[t, half:], preferred_element_type=F32)
        hid = a + pltpu.roll(bm, bm.shape[0] - 1, 0)
        return jax.nn.gelu(hid).astype(BF16)

    ko_ref[0, 0] = jnp.dot(hidden(0, kc_ref), w2k_ref[...], preferred_element_type=F32)
    vo_ref[0, 0] = lax.dot_general(w2vt_ref[...], hidden(1, vc_ref), NT_DIMS, preferred_element_type=F32)


def _compress(kc, vc, pe4, w1, w2k, w2v_t):
    b, g, s, dh = kc.shape
    n_chunk = s // NSA_CMP_STRIDE
    spec_in = pl.BlockSpec((1, 1, s, dh), lambda bi, gi: (bi, gi, 0, 0))
    return pl.pallas_call(
        _compress_kernel,
        grid=(b, g),
        in_specs=[
            spec_in, spec_in,
            pl.BlockSpec(pe4.shape, lambda bi, gi: (0, 0)),
            pl.BlockSpec(w1.shape, lambda bi, gi: (0, 0, 0)),
            pl.BlockSpec(w2k.shape, lambda bi, gi: (0, 0)),
            pl.BlockSpec(w2v_t.shape, lambda bi, gi: (0, 0)),
        ],
        out_specs=[pl.BlockSpec((1, 1, n_chunk, dh), lambda bi, gi: (bi, gi, 0, 0)),
                   pl.BlockSpec((1, 1, dh, n_chunk), lambda bi, gi: (bi, gi, 0, 0))],
        out_shape=[jax.ShapeDtypeStruct((b, g, n_chunk, dh), F32), jax.ShapeDtypeStruct((b, g, dh, n_chunk), F32)],
        compiler_params=_params("parallel", "parallel"),
        name="nsa_compress",
    )(kc, vc, pe4, w1, w2k, w2v_t)


def _nsa_kernel(qt_ref, kcmp_ref, vcmpt_ref, ks_ref, vst_ref, kw_ref, vwt_ref, gate_ref,
                bias_ref, overlap_ref, o_ref, qx_ref, ocmp_ref, m_ref, acc_ref, m2_ref, acc2_ref,
                s0_ref, s1_ref, p0_ref, p1_ref, *, slc_runs, win_runs):
    t = ATT_TILE
    heads = range(NSA_GROUP)
    s = qt_ref.shape[2]
    qt_refs = [qt_ref.at[0, r * HEAD_DIM:(r + 1) * HEAD_DIM] for r in heads]
    n_chunk = kcmp_ref.shape[2]
    n_slc = s // NSA_SLC_BLOCK
    ks_g, vst_g = ks_ref.at[0, 0], vst_ref.at[0]
    kw_g, vwt_g = kw_ref.at[0, 0], vwt_ref.at[0]
    k_cmp = kcmp_ref[0, 0].astype(BF16)
    v_cmp_t = vcmpt_ref[0, 0].astype(BF16)
    topk = min(NSA_SLC_TOPK, n_slc)
    width = min(topk * NSA_SLC_BLOCK, s)
    assert s % width == 0 and width % t == 0

    def select(q0):
        cmp_id = lax.broadcasted_iota(jnp.int32, (n_chunk, width), 0)
        q_pos = q0 + lax.broadcasted_iota(jnp.int32, (n_chunk, width), 1)
        cmp_valid = (cmp_id * NSA_CMP_STRIDE + (NSA_CMP_BLOCK - 1) <= q_pos) & (cmp_id < n_chunk - 1)
        masked = jnp.where(cmp_valid, 0.0, NEG_INF)
        sees_any = q_pos[0:1, :] >= NSA_CMP_BLOCK - 1
        p_sum = jnp.zeros((n_chunk, width), F32)
        for r in heads:
            sc = jnp.dot(k_cmp, qt_refs[r][:, q0:q0 + width], preferred_element_type=F32) + masked
            e = jnp.exp2(sc - jnp.max(sc, axis=0, keepdims=True))
            l = jnp.sum(e, axis=0, keepdims=True)
            p = e * jnp.where(sees_any, 1.0 / l, 0.0)
            p_sum = p_sum + p
            o = jnp.dot(v_cmp_t, p.astype(BF16), preferred_element_type=F32)
            for u in range(width // t):
                ocmp_ref[r, q0 // t + u] = o[:, u * t:(u + 1) * t]
        blk = lax.broadcasted_iota(jnp.int32, (n_slc, width), 0)
        cur = (q0 + lax.broadcasted_iota(jnp.int32, (n_slc, width), 1)) // NSA_SLC_BLOCK
        if q0 + width <= topk * NSA_SLC_BLOCK:
            sel = blk <= cur
        else:
            ps_hi = p_sum.astype(BF16)
            ps_lo = (p_sum - ps_hi.astype(F32)).astype(BF16)
            imp = (jnp.dot(overlap_ref[...], ps_hi, preferred_element_type=F32)
                   + jnp.dot(overlap_ref[...], ps_lo, preferred_element_type=F32))
            forced = (blk == 0) | (blk > cur - NSA_LOCAL_BLOCKS)
            score = jnp.where(blk <= cur, jnp.where(forced, FORCE_SCORE, imp), NEG_INF)
            sel = _topk_rows(score, n_slc, topk)
        for u in range(width // t):
            cols = slice(q0 + u * t, q0 + (u + 1) * t)
            for r in heads:
                qx_ref[r, :, cols] = _extend_query(qt_refs[r][:, cols], sel[:, u * t:(u + 1) * t])

    for q0 in range(0, s, width):
        select(q0)

    def adds(r, kind):
        return [] if _ADD_TILE[kind] is None else [bias_ref[0, r, _ADD_TILE[kind]]]

    def slc_inputs(qi, kj, kind):
        return [(_rows(ks_g, kj), _cols(qx_ref.at[r], qi), adds(r, kind)) for r in heads]

    def win_inputs(qi, kj, kind):
        return [(_rows(kw_g, kj), _cols(qt_refs[r], qi), adds(r, kind)) for r in heads]

    bufs = ((s0_ref, s1_ref), (p0_ref, p1_ref))
    _attend_tiles(slc_runs, slc_inputs, lambda kj: [_cols(vst_g, kj)] * len(heads), m_ref, acc_ref, *bufs)
    _attend_tiles(win_runs, win_inputs, lambda kj: [_cols(vwt_g, kj)] * len(heads), m2_ref, acc2_ref, *bufs)

    def finish(qi, carry):
        cols = pl.ds(pl.multiple_of(qi * t, t), t)
        gates = jax.nn.sigmoid(gate_ref[0, :, cols])
        outs = []
        for r in heads:
            g_cmp, g_slc, g_win = (gates[3 * r + br:3 * r + br + 1, :] for br in range(3))
            outs.append(g_cmp * ocmp_ref[r, qi] + g_slc * _normalized(acc_ref[r, qi]) + g_win * _normalized(acc2_ref[r, qi]))
        o_ref[0, cols, :] = jnp.concatenate(outs, axis=0).T.astype(o_ref.dtype)
        return carry

    _for_each_tile(s // t, finish)


def _nsa(nq_t, k_cmp, v_cmp_t, ks, vs_t, kw, vw_t, gate_t, bias_tiles, overlap_t):
    b, _, s, dh = kw.shape
    g, r, t = NSA_KV_HEADS, NSA_GROUP, ATT_TILE
    assert NSA_WINDOW == 2 * t and s % t == 0 and t % NSA_SLC_BLOCK == 0
    slc_runs = _tile_schedule(s // t)
    win_runs = _tile_schedule(s // t, window=NSA_WINDOW // t)
    k_spec = lambda k: pl.BlockSpec((1, 1, s, k.shape[3]), lambda bi, gi: (bi, gi, 0, 0))
    vt_spec = pl.BlockSpec((1, dh, s), lambda bi, gi: (bi, gi, 0))
    scratch = _attention_scratch(r, s // t, t)
    return pl.pallas_call(
        functools.partial(_nsa_kernel, slc_runs=slc_runs, win_runs=win_runs),
        grid=(b, g),
        in_specs=[
            pl.BlockSpec((1, r * dh, s), lambda bi, gi: (bi, gi, 0)),
            pl.BlockSpec((1, 1) + k_cmp.shape[2:], lambda bi, gi: (bi, gi, 0, 0)),
            pl.BlockSpec((1, 1) + v_cmp_t.shape[2:], lambda bi, gi: (bi, gi, 0, 0)),
            k_spec(ks), vt_spec, k_spec(kw), vt_spec,
            pl.BlockSpec((1, NSA_GATE_ROWS, s), lambda bi, gi: (bi, gi, 0)),
            pl.BlockSpec((1,) + bias_tiles.shape[1:], lambda bi, gi: (gi, 0, 0, 0, 0)),
            pl.BlockSpec(overlap_t.shape, lambda bi, gi: (0, 0)),
        ],
        out_specs=pl.BlockSpec((1, s, r * dh), lambda bi, gi: (bi, 0, gi)),
        out_shape=jax.ShapeDtypeStruct((b, s, g * r * dh), BF16),
        scratch_shapes=[pltpu.VMEM((r, ks.shape[3], s), BF16), pltpu.VMEM((r, s // t, dh, t), F32)]
                       + scratch[:2] + scratch,
        compiler_params=_params("parallel", "parallel"),
        name="nsa",
    )(nq_t, k_cmp, v_cmp_t, ks, vs_t, kw, vw_t, gate_t, bias_tiles, overlap_t)


def _rel_bucket(dist):
    n = jnp.maximum(dist, 0)
    max_exact = REL_BUCKETS // 2
    n_f = jnp.maximum(n, max_exact).astype(F32)
    large = max_exact + (jnp.log(n_f / max_exact) / math.log(REL_MAX_DIST / max_exact)
                         * (REL_BUCKETS - max_exact)).astype(jnp.int32)
    return jnp.where(n < max_exact, n, jnp.minimum(large, REL_BUCKETS - 1))


def _bias_tiles(rel_bias):
    t = ATT_TILE
    assert t >= REL_MAX_DIST
    heads = rel_bias.shape[1]
    width = 2 * t + 1
    by_dist = rel_bias[_rel_bucket(jnp.arange(width))] - rel_bias[REL_BUCKETS - 1]
    skew = jnp.broadcast_to(by_dist.T[:, None, :], (heads, t, width)).reshape(heads, t * width)
    skew = skew[:, :t * (width - 1)].reshape(heads, t, width - 1)
    causal = jnp.arange(t)[:, None] <= jnp.arange(t)[None, :]
    own = jnp.where(causal, skew[:, :, :t], NEG_INF)
    return jnp.stack([own, skew[:, :, t:]], axis=1)


def _const_tables(s):
    n_moba = s // MOBA_BLOCK
    avg = np.zeros((2 * SUBLANES, s), np.float32)
    for j in range(n_moba):
        avg[j, j * MOBA_BLOCK:(j + 1) * MOBA_BLOCK] = 1.0 / MOBA_BLOCK
    n_cmp = (s - NSA_CMP_BLOCK) // NSA_CMP_STRIDE + 1
    n_slc = s // NSA_SLC_BLOCK
    cmp_start = np.arange(n_cmp) * NSA_CMP_STRIDE
    slc_start = np.arange(n_slc) * NSA_SLC_BLOCK
    ov = np.clip(np.minimum(cmp_start[:, None] + NSA_CMP_BLOCK, slc_start[None, :] + NSA_SLC_BLOCK)
                 - np.maximum(cmp_start[:, None], slc_start[None, :]), 0, None) / NSA_CMP_BLOCK
    overlap_t = np.zeros((n_slc, s // NSA_CMP_STRIDE), np.float32)
    overlap_t[:, :n_cmp] = ov.T
    t = ATT_TILE
    win_far = np.where(np.arange(t)[:, None] > np.arange(t)[None, :], 0.0, NEG_INF).astype(np.float32)
    member = np.zeros((len(_MEMBER_BLOCKS), s, HEAD_DIM), np.float32)
    for kind, block in enumerate(_MEMBER_BLOCKS):
        assert s // block <= HEAD_DIM
        member[kind, np.arange(s), np.arange(s) // block] = 1.0
    return jnp.asarray(avg, BF16), jnp.asarray(overlap_t, BF16), jnp.asarray(win_far), jnp.asarray(member, BF16)


def kernel(x, rel_bias, norm_ffn1, ffn1_gate, ffn1_up, ffn1_down, norm_mix, w_in, diff_lambda, diff_subln, nsa_cmp_pe, nsa_cmp_w1, nsa_cmp_w2, w_out, norm_ffn2, ffn2_gate, ffn2_up, ffn2_down, final_norm):
    b, s, d = x.shape
    depth = w_in.shape[0]
    assert d == D_MODEL
    h0, h1 = MOBA_HEADS, MOBA_HEADS + DIFF_HEADS
    near = _bias_tiles(rel_bias.astype(F32))
    avg, overlap_t, win_far, member = _const_tables(s)
    near = near * LOG2E
    moba_bias = near[:h0]
    diff_bias = near[h0:h1]
    nsa_bias = jnp.concatenate([near[h1:], jnp.broadcast_to(win_far, near[h1:, :1].shape)], axis=1)
    nsa_bias = nsa_bias.reshape((NSA_KV_HEADS, NSA_GROUP) + nsa_bias.shape[1:])
    bf = lambda a: a.astype(BF16)
    ffn1_gate, ffn1_up, ffn1_down, ffn2_gate, ffn2_up, ffn2_down, w_in, w_out = (
        _to_bf16(w) for w in (ffn1_gate, ffn1_up, ffn1_down, ffn2_gate, ffn2_up, ffn2_down, w_in, w_out))
    cmp_w1 = _to_bf16(nsa_cmp_w1.reshape((-1,) + nsa_cmp_w1.shape[2:])).reshape(nsa_cmp_w1.shape)

    x = x.reshape(b * s, d)
    flat = lambda a: a.reshape(b * s, a.shape[-1])
    for l in range(depth):
        lambda_init = 0.8 - 0.6 * math.exp(-0.3 * l)
        x = _ffn(x, l, norm_ffn1[l], ffn1_gate, ffn1_up, ffn1_down)

        p = _inproj(x.reshape(b, s, d), norm_mix[l], *_inproj_weights(w_in[l]), member)
        o_moba = _moba(p["mqT"], p["mk"], p["mvT"], moba_bias, avg)
        o_diff = _diff(p["dqT"], p["dk"], p["dvT"], diff_bias, diff_lambda[l].astype(F32), diff_subln[l].astype(F32),
                       lambda_init)
        pe4 = nsa_cmp_pe[l].astype(F32).reshape(4, NSA_CMP_STRIDE * HEAD_DIM)
        k_cmp, v_cmp_t = _compress(p["kc"], p["vc"], pe4, cmp_w1[l], bf(nsa_cmp_w2[l, 0]), bf(nsa_cmp_w2[l, 1].T))
        o_nsa = _nsa(p["nqT"], k_cmp, v_cmp_t, p["ks"], p["vsT"], p["kw"], p["vwT"], p["gateT"], nsa_bias, overlap_t)
        x = _ffn(x, l, norm_ffn2[l], ffn2_gate, ffn2_up, ffn2_down,
                 mix=(flat(o_moba), flat(o_diff), flat(o_nsa)), w_mix=w_out,
                 post_g=final_norm if l == depth - 1 else None)
    return x.reshape(b, s, d)
```
